```python
import jax, jax.numpy as jnp
from jax import lax
import numpy as np

D_MODEL = 1024
BATCH = 8
SEQ = 2048
DEPTH = 1

ATTN_GROUPS = ((128, 1), (512, 4), (2048, 16))
N_ATTN_GROUPS = len(ATTN_GROUPS)
ATTN_HEADS_PER_GROUP = 4
ATTN_HEAD_DIM = 128
ATTN_WIDTH = N_ATTN_GROUPS * ATTN_HEADS_PER_GROUP * ATTN_HEAD_DIM
ATTN_OUT_WIDTH = ATTN_HEADS_PER_GROUP * ATTN_HEAD_DIM
ROPE_THETA = 500000.0
ROPE_DIM = ATTN_HEAD_DIM // 4
MLSTM_HEADS = 4
MLSTM_WIDTH = D_MODEL
MLSTM_HEAD_DIM = MLSTM_WIDTH // MLSTM_HEADS
MLSTM_CHUNK = 64
CONV_WIDTH = 4
N_EXPERTS = 32
TOP_K = 4
D_FF = D_MODEL
SWIGLU_LIMIT = 7.0
SWIGLU_ALPHA = 1.702
MOE_BLOCK = 128
RMS_EPS = 1e-5
LN_EPS = 1e-5

IN_SPLIT_SIZES = (ATTN_WIDTH, ATTN_WIDTH, ATTN_WIDTH,
                  MLSTM_WIDTH, MLSTM_WIDTH, MLSTM_WIDTH,
                  MLSTM_WIDTH,
                  MLSTM_HEADS, MLSTM_HEADS,
                  D_MODEL, D_MODEL)
D_IN = sum(IN_SPLIT_SIZES)

kernel_name = 'hybrid_dilated_attn_mlstm_moe_block'


def rms_norm(x, g):
    x32 = x.astype(jnp.float32)
    y = x32 * lax.rsqrt(jnp.mean(x32 * x32, axis=-1, keepdims=True) + RMS_EPS)
    return (y * g.astype(jnp.float32)).astype(x.dtype)


def partial_rotary(x, positions):
    half = ROPE_DIM // 2
    inv_freq = ROPE_THETA ** (-jnp.arange(half, dtype=jnp.float32) / half)
    ang = positions.astype(jnp.float32)[..., None] * inv_freq
    ang = ang.reshape(ang.shape[:2] + (1,) * (x.ndim - 3) + (half,))
    cos, sin = jnp.cos(ang), jnp.sin(ang)
    x32 = x.astype(jnp.float32)
    x1, x2 = x32[..., :half], x32[..., half:ROPE_DIM]
    out = jnp.concatenate([x1 * cos - x2 * sin, x2 * cos + x1 * sin, x32[..., ROPE_DIM:]], axis=-1)
    return out.astype(x.dtype)


def dilated_window_attention(q, k, v, window, dilation):
    B_, S_, H, Dh = q.shape
    sub = window // dilation
    L = S_ // dilation
    nblk = -(-L // sub)
    Lp = nblk * sub

    def to_sub(t):
        t = t.reshape(B_, L, dilation, H, Dh).transpose(0, 2, 3, 1, 4)
        t = jnp.pad(t, ((0, 0), (0, 0), (0, 0), (0, Lp - L), (0, 0)))
        return t.reshape(B_, dilation, H, nblk, sub, Dh)

    def with_prev(t):
        prev = jnp.pad(t, ((0, 0), (0, 0), (0, 0), (1, 0), (0, 0), (0, 0)))[:, :, :, :-1]
        return jnp.concatenate([prev, t], axis=-2)

    qb = to_sub(q)
    kk = with_prev(to_sub(k))
    vv = with_prev(to_sub(v))
    s = jnp.einsum('bdhnqe,bdhnke->bdhnqk', qb, kk, preferred_element_type=jnp.float32) * (Dh ** -0.5)
    blk = jnp.arange(nblk)[:, None, None] * sub
    q_pos = blk + jnp.arange(sub)[None, :, None]
    k_pos = blk - sub + jnp.arange(2 * sub)[None, None, :]
    rel = q_pos - k_pos
    valid = (rel >= 0) & (rel <= sub) & (k_pos >= 0)
    s = jnp.where(valid, s, -jnp.inf)
    m = jnp.max(s, axis=-1, keepdims=True)
    p = jnp.exp(s - m)
    den = jnp.sum(p, axis=-1)
    out = jnp.einsum('bdhnqk,bdhnke->bdhnqe', p, vv.astype(jnp.float32)) / den[..., None]
    lse = m[..., 0] + jnp.log(den)
    out = out.reshape(B_, dilation, H, Lp, Dh)[:, :, :, :L].transpose(0, 3, 1, 2, 4).reshape(B_, S_, H, Dh)
    lse = lse.reshape(B_, dilation, H, Lp)[..., :L].transpose(0, 3, 1, 2).reshape(B_, S_, H)
    return out, lse


def dilated_attention_mixer(aq, ak, av, positions):
    B_, S_, _ = aq.shape
    shp = (B_, S_, N_ATTN_GROUPS, ATTN_HEADS_PER_GROUP, ATTN_HEAD_DIM)
    q = partial_rotary(aq.reshape(shp), positions)
    k = partial_rotary(ak.reshape(shp), positions)
    v = av.reshape(shp)
    outs, lses = [], []
    for g, (window, dilation) in enumerate(ATTN_GROUPS):
        o, lse = dilated_window_attention(q[:, :, g], k[:, :, g], v[:, :, g], window, dilation)
        outs.append(o)
        lses.append(lse)
    w = jax.nn.softmax(jnp.stack(lses), axis=0)
    out = jnp.sum(w[..., None] * jnp.stack(outs), axis=0)
    return out.reshape(B_, S_, ATTN_OUT_WIDTH).astype(aq.dtype)


def causal_depthwise_conv(x, w):
    K, C = w.shape
    return lax.conv_general_dilated(x, w[:, None, :].astype(x.dtype), window_strides=(1,),
                                    padding=[(K - 1, 0)], dimension_numbers=('NWC', 'WIO', 'NWC'),
                                    feature_group_count=C)


def mlstm_chunkwise(q, k, v, i_raw, f_raw):
    B_, S_, H, Dk = q.shape
    Dv = v.shape[-1]
    LC = MLSTM_CHUNK
    nC = S_ // LC

    def chunks(t):
        t = t.astype(jnp.float32)
        if t.ndim == 4:
            return t.reshape(B_, nC, LC, H, t.shape[-1]).transpose(1, 0, 3, 2, 4)
        return t.reshape(B_, nC, LC, H).transpose(1, 0, 3, 2)

    logf = jax.nn.log_sigmoid(f_raw.astype(jnp.float32))
    xs = (chunks(q), chunks(k), chunks(v), chunks(i_raw), chunks(logf))
    causal = jnp.tril(jnp.ones((LC, LC), dtype=bool))

    def step(carry, inp):
        C, n, m = carry
        qc, kc, vc, ic, fc = inp
        b = jnp.cumsum(fc, axis=-1)
        Dm = jnp.where(causal, b[..., :, None] - b[..., None, :] + ic[..., None, :], -jnp.inf)
        inter = b + m[..., None]
        m_t = jnp.maximum(inter, jnp.max(Dm, axis=-1))
        w_intra = jnp.exp(Dm - m_t[..., None])
        w_inter = jnp.exp(inter - m_t)
        A = jnp.einsum('bhtd,bhsd->bhts', qc, kc) * w_intra
        num = jnp.einsum('bhts,bhsv->bhtv', A, vc) + w_inter[..., None] * jnp.einsum('bhtd,bhdv->bhtv', qc, C)
        den = jnp.sum(A, axis=-1) + w_inter * jnp.einsum('bhtd,bhd->bht', qc, n)
        h = num / jnp.maximum(jnp.abs(den), jnp.exp(-m_t))[..., None]
        bL = b[..., -1]
        g = bL[..., None] - b + ic
        m_new = jnp.maximum(bL + m, jnp.max(g, axis=-1))
        decay = jnp.exp(bL + m - m_new)
        wk = jnp.exp(g - m_new[..., None])
        kw = kc * wk[..., None]
        C_new = decay[..., None, None] * C + jnp.einsum('bhsd,bhsv->bhdv', kw, vc)
        n_new = decay[..., None] * n + jnp.sum(kw, axis=-2)
        return (C_new, n_new, m_new), h

    init = (jnp.zeros((B_, H, Dk, Dv), jnp.float32), jnp.zeros((B_, H, Dk), jnp.float32),
            jnp.zeros((B_, H), jnp.float32))
    _, hs = lax.scan(step, init, xs)
    return hs.transpose(1, 0, 3, 2, 4).reshape(B_, S_, H, Dv)


def mlstm_mixer(mq, mk, mv, mo, mi, mf, conv_w, b_if, g_norm):
    B_, S_, _ = mq.shape
    H, Dh = MLSTM_HEADS, MLSTM_HEAD_DIM
    qk = jax.nn.silu(causal_depthwise_conv(jnp.concatenate([mq, mk], axis=-1), conv_w))
    q = qk[..., :MLSTM_WIDTH].reshape(B_, S_, H, Dh)
    k = qk[..., MLSTM_WIDTH:].reshape(B_, S_, H, Dh) * (Dh ** -0.5)
    v = mv.reshape(B_, S_, H, Dh)
    i_raw = mi.astype(jnp.float32) + b_if[:H].astype(jnp.float32)
    f_raw = mf.astype(jnp.float32) + b_if[H:].astype(jnp.float32)
    h = mlstm_chunkwise(q, k, v, i_raw, f_raw)
    mu = jnp.mean(h, axis=-1, keepdims=True)
    var = jnp.mean(jnp.square(h - mu), axis=-1, keepdims=True)
    hn = (h - mu) * lax.rsqrt(var + LN_EPS) * g_norm.astype(jnp.float32).reshape(H, Dh)
    out = jax.nn.sigmoid(mo.astype(jnp.float32)) * hn.reshape(B_, S_, MLSTM_WIDTH)
    return out.astype(mq.dtype)


def moe_ffn(xn, w_router, b_router, w_gate_up, b_gate_up, w_down, b_down):
    B_, S_, D = xn.shape
    T = B_ * S_
    A = T * TOP_K
    xf = xn.reshape(T, D)
    logits = (xf @ w_router + b_router).astype(jnp.float32)
    top_logits, top_idx = lax.top_k(logits, TOP_K)
    gates = jax.nn.softmax(top_logits, axis=-1)
    flat_e = top_idx.reshape(A)
    flat_tok = jnp.repeat(jnp.arange(T, dtype=jnp.int32), TOP_K)
    flat_w = gates.reshape(A)
    order = jnp.argsort(flat_e)
    sorted_e = flat_e[order]
    counts = jnp.bincount(flat_e, length=N_EXPERTS)
    starts = jnp.cumsum(counts) - counts
    padded = (counts + MOE_BLOCK - 1) // MOE_BLOCK * MOE_BLOCK
    padded_ends = jnp.cumsum(padded)
    padded_starts = padded_ends - padded
    rank = jnp.arange(A, dtype=jnp.int32) - starts[sorted_e]
    dest = padded_starts[sorted_e] + rank
    n_blocks = -(-A // MOE_BLOCK) + N_EXPERTS
    n_rows = n_blocks * MOE_BLOCK
    row_tok = jnp.zeros((n_rows,), jnp.int32).at[dest].set(flat_tok[order])
    row_w = jnp.zeros((n_rows,), jnp.float32).at[dest].set(flat_w[order])
    block_e = jnp.minimum(jnp.searchsorted(padded_ends, jnp.arange(n_blocks, dtype=jnp.int32) * MOE_BLOCK,
                                           side='right'), N_EXPERTS - 1)
    xs = xf[row_tok].reshape(n_blocks, MOE_BLOCK, D)

    def expert_block(args):
        xb, e = args
        hg = xb @ w_gate_up[e] + b_gate_up[e]
        gate = jnp.minimum(hg[:, 0::2], SWIGLU_LIMIT)
        up = jnp.clip(hg[:, 1::2], -SWIGLU_LIMIT, SWIGLU_LIMIT)
        act = (up + 1.0) * (gate * jax.nn.sigmoid(SWIGLU_ALPHA * gate))
        return act @ w_down[e] + b_down[e]

    ys = lax.map(expert_block, (xs, block_e)).reshape(n_rows, D).astype(jnp.float32) * row_w[:, None]
    out = jnp.zeros((T, D), jnp.float32).at[row_tok].add(ys)
    return out.reshape(B_, S_, D).astype(xn.dtype)


def setup_inputs(seed: int = 0) -> dict:
    key = jax.random.key(seed)
    ks = jax.random.split(key, 20)
    f32 = jnp.float32
    nrm = lambda k, shp: jax.random.normal(k, shp, f32)
    H = MLSTM_HEADS
    b_i = 0.1 * nrm(ks[5], (DEPTH, H))
    b_f = jnp.linspace(3.0, 6.0, H, dtype=f32)[None, :] + 0.1 * nrm(ks[6], (DEPTH, H))
    return {
        'x': nrm(ks[0], (BATCH, SEQ, D_MODEL)),
        'positions': (jnp.arange(SEQ, dtype=jnp.int32)[None, :]
                      + jax.random.randint(ks[1], (BATCH, 1), 0, 4096, dtype=jnp.int32)),
        'g_mix': 1.0 + 0.05 * nrm(ks[2], (DEPTH, D_MODEL)),
        'w_in': nrm(ks[3], (DEPTH, D_MODEL, D_IN)) * D_MODEL ** -0.5,
        'conv_qk': nrm(ks[4], (DEPTH, CONV_WIDTH, 2 * MLSTM_WIDTH)) * CONV_WIDTH ** -0.5,
        'b_if': jnp.concatenate([b_i, b_f], axis=-1),
        'g_mlstm_norm': 1.0 + 0.05 * nrm(ks[7], (DEPTH, MLSTM_WIDTH)),
        'w_attn_out': nrm(ks[8], (DEPTH, ATTN_OUT_WIDTH, D_MODEL)) * ATTN_OUT_WIDTH ** -0.5,
        'w_mlstm_out': nrm(ks[9], (DEPTH, MLSTM_WIDTH, D_MODEL)) * MLSTM_WIDTH ** -0.5,
        'w_mix_out': nrm(ks[10], (DEPTH, D_MODEL, D_MODEL)) * D_MODEL ** -0.5,
        'g_ffn': 1.0 + 0.05 * nrm(ks[11], (DEPTH, D_MODEL)),
        'w_router': nrm(ks[12], (DEPTH, D_MODEL, N_EXPERTS)) * D_MODEL ** -0.5,
        'b_router': 0.01 * nrm(ks[13], (DEPTH, N_EXPERTS)),
        'w_gate_up': nrm(ks[14], (DEPTH, N_EXPERTS, D_MODEL, 2 * D_FF)) * D_MODEL ** -0.5,
        'b_gate_up': 0.02 * nrm(ks[15], (DEPTH, N_EXPERTS, 2 * D_FF)),
        'w_down': nrm(ks[16], (DEPTH, N_EXPERTS, D_FF, D_MODEL)) * D_FF ** -0.5,
        'b_down': 0.02 * nrm(ks[17], (DEPTH, N_EXPERTS, D_MODEL)),
        'g_final': 1.0 + 0.05 * nrm(ks[18], (D_MODEL,)),
    }


def reference(x, positions, g_mix, w_in, conv_qk, b_if, g_mlstm_norm, w_attn_out, w_mlstm_out,
              w_mix_out, g_ffn, w_router, b_router, w_gate_up, b_gate_up, w_down, b_down, g_final):
    offsets = [int(o) for o in np.cumsum(IN_SPLIT_SIZES)[:-1]]
    h = x
    for l in range(DEPTH):
        xn = rms_norm(h, g_mix[l])
        proj = xn @ w_in[l]
        aq, ak, av, mq, mk, mv, mo, mi, mf, ga, gm = jnp.split(proj, offsets, axis=-1)
        attn = dilated_attention_mixer(aq, ak, av, positions)
        mem = mlstm_mixer(mq, mk, mv, mo, mi, mf, conv_qk[l], b_if[l], g_mlstm_norm[l])
        merged = (jax.nn.sigmoid(ga) * (attn @ w_attn_out[l])
                  + jax.nn.sigmoid(gm) * (mem @ w_mlstm_out[l]))
        h = h + merged @ w_mix_out[l]
        h = h + moe_ffn(rms_norm(h, g_ffn[l]), w_router[l], b_router[l], w_gate_up[l],
                        b_gate_up[l], w_down[l], b_down[l])
    return rms_norm(h, g_final)
```

```python
import functools

import jax
import jax.numpy as jnp
import numpy as np
from jax import lax
from jax.experimental import pallas as pl
from jax.experimental.pallas import tpu as pltpu

F32 = jnp.float32
BF16 = jnp.bfloat16

D_MODEL = 1024
N_GROUPS = 3
GROUP_DILATION = (1, 4, 16)
HEADS_PER_GROUP = 4
HEAD_DIM = 128
ATTN_WIDTH = N_GROUPS * HEADS_PER_GROUP * HEAD_DIM
ATTN_OUT = HEADS_PER_GROUP * HEAD_DIM
ATTN_BLOCK = 128
ROPE_THETA = 500000.0
ROPE_DIM = HEAD_DIM // 4
M_HEADS = 4
M_WIDTH = D_MODEL
M_HEAD_DIM = M_WIDTH // M_HEADS
CONV_WIDTH = 4
N_EXPERTS = 32
TOP_K = 4
D_FF = D_MODEL
SWIGLU_LIMIT = 7.0
SWIGLU_ALPHA = 1.702
RMS_EPS = 1e-5
LN_EPS = 1e-5

LANES = 128
SUBLANES = 8
VMEM_LIMIT = 56 * 1024 * 1024

PROJ_TM = 1024
PROJ_TN = 1536
MLSTM_CHUNK = 128
POST_TM = 512
MOE_BLOCK = 256
FINAL_TM = 512
NEG = -1e30

COL_AQ = 0
COL_AK = ATTN_WIDTH
COL_GA = 2 * ATTN_WIDTH
COL_GM = COL_GA + D_MODEL
COL_AV = COL_GM + D_MODEL
COL_MQ = COL_AV + ATTN_WIDTH
COL_MK = COL_MQ + M_WIDTH
COL_MV = COL_MK + M_WIDTH
COL_MO = COL_MV + M_WIDTH
N_MAIN = COL_MO + M_WIDTH


def _params(sem):
    return pltpu.CompilerParams(dimension_semantics=sem, vmem_limit_bytes=VMEM_LIMIT)


def _proj_kernel(x_ref, g_ref, w_ref, wif_ref, cos_ref, sa_ref, sb_ref, o_ref, if_ref, xn_ref, *, rope_tiles):
    j = pl.program_id(1)

    @pl.when(j == 0)
    def _():
        x = x_ref[...]
        ms = jnp.mean(x * x, axis=-1, keepdims=True)
        xn = (x * lax.rsqrt(ms + RMS_EPS) * g_ref[...]).astype(BF16)
        xn_ref[...] = xn
        if_ref[...] = jnp.dot(xn, wif_ref[...], preferred_element_type=F32)

    acc = jnp.dot(xn_ref[...], w_ref[...], preferred_element_type=F32)

    @pl.when(j < rope_tiles)
    def _():
        c = cos_ref[...]
        sa = sa_ref[...]
        sb = sb_ref[...]
        for h in range(acc.shape[1] // HEAD_DIM):
            a = acc[:, h * HEAD_DIM:(h + 1) * HEAD_DIM]
            y = a * c + pltpu.roll(a, HEAD_DIM - ROPE_DIM // 2, 1) * sa + pltpu.roll(a, ROPE_DIM // 2, 1) * sb
            o_ref[:, h * HEAD_DIM:(h + 1) * HEAD_DIM] = y.astype(o_ref.dtype)

    @pl.when(j >= rope_tiles)
    def _():
        o_ref[...] = acc.astype(o_ref.dtype)


def _proj(x2, g_mix, w_main, w_if, cos_t, sa_t, sb_t):
    T, D = x2.shape
    N = w_main.shape[1]
    tm, tn = PROJ_TM, PROJ_TN
    grid = (T // tm, N // tn)
    return pl.pallas_call(
        functools.partial(_proj_kernel, rope_tiles=(2 * ATTN_WIDTH) // tn),
        grid=grid,
        in_specs=[
            pl.BlockSpec((tm, D), lambda i, j: (i, 0)),
            pl.BlockSpec((1, D), lambda i, j: (0, 0)),
            pl.BlockSpec((D, tn), lambda i, j: (0, j)),
            pl.BlockSpec((D, LANES), lambda i, j: (0, 0)),
            pl.BlockSpec((tm, LANES), lambda i, j: (i, 0)),
            pl.BlockSpec((tm, LANES), lambda i, j: (i, 0)),
            pl.BlockSpec((tm, LANES), lambda i, j: (i, 0)),
        ],
        out_specs=[
            pl.BlockSpec((tm, tn), lambda i, j: (i, j)),
            pl.BlockSpec((tm, LANES), lambda i, j: (i, 0)),
        ],
        out_shape=[jax.ShapeDtypeStruct((T, N), BF16), jax.ShapeDtypeStruct((T, LANES), F32)],
        scratch_shapes=[pltpu.VMEM((tm, D), BF16)],
        compiler_params=_params(("parallel", "arbitrary")),
        name="proj",
    )(x2, g_mix, w_main, w_if, cos_t, sa_t, sb_t)


def _attn_kernel(q_ref, k_ref, v_ref, o_ref, lse_ref):
    S = q_ref.shape[2]
    nblk = S // ATTN_BLOCK
    g = pl.program_id(1) // HEADS_PER_GROUP
    per_class = lax.shift_right_logical(jnp.int32(nblk), 2 * g)
    row = lax.broadcasted_iota(jnp.int32, (ATTN_BLOCK, ATTN_BLOCK), 0)
    col = lax.broadcasted_iota(jnp.int32, (ATTN_BLOCK, ATTN_BLOCK), 1)
    bias_cur = jnp.where(col <= row, 0.0, NEG).astype(F32)
    bias_prev = jnp.where(col >= row, 0.0, NEG).astype(F32)
    eye = col == row
    scale = HEAD_DIM ** -0.5
    dn = (((1,), (1,)), ((), ()))
    for n in range(nblk):
        sl = slice(n * ATTN_BLOCK, (n + 1) * ATTN_BLOCK)
        q = q_ref[0, 0, sl, :]
        s_c = lax.dot_general(q, k_ref[0, 0, sl, :], dn, preferred_element_type=F32) * scale + bias_cur
        m = jnp.max(s_c, axis=1, keepdims=True)
        if n > 0:
            sp = slice((n - 1) * ATTN_BLOCK, n * ATTN_BLOCK)
            has_prev = (jnp.int32(n) & (per_class - 1)) != 0
            gate = jnp.where(has_prev, 0.0, NEG).astype(F32)
            s_p = lax.dot_general(q, k_ref[0, 0, sp, :], dn, preferred_element_type=F32) * scale + (bias_prev + gate)
            m = jnp.maximum(m, jnp.max(s_p, axis=1, keepdims=True))
        p_c = jnp.exp(s_c - m)
        den = jnp.sum(p_c, axis=1, keepdims=True)
        acc = jnp.dot(p_c.astype(BF16), v_ref[0, 0, sl, :], preferred_element_type=F32)
        if n > 0:
            p_p = jnp.exp(s_p - m)
            den = den + jnp.sum(p_p, axis=1, keepdims=True)
            acc = acc + jnp.dot(p_p.astype(BF16), v_ref[0, 0, sp, :], preferred_element_type=F32)
        o_ref[0, 0, sl, :] = (acc / den).astype(o_ref.dtype)
        lse = m + jnp.log(den)
        lse_ref[0, 0, n:n + 1, :] = jnp.sum(jnp.where(eye, lse, 0.0), axis=0, keepdims=True)


def _attention(q, k, v):
    B, GH, S, Dh = q.shape
    nblk = S // ATTN_BLOCK
    spec = pl.BlockSpec((1, 1, S, Dh), lambda b, h: (b, h, 0, 0))
    return pl.pallas_call(
        _attn_kernel,
        grid=(B, GH),
        in_specs=[spec, spec, spec],
        out_specs=[spec, pl.BlockSpec((1, 1, nblk, ATTN_BLOCK), lambda b, h: (b, h, 0, 0))],
        out_shape=[jax.ShapeDtypeStruct((B, GH, S, Dh), BF16),
                   jax.ShapeDtypeStruct((B, GH, nblk, ATTN_BLOCK), F32)],
        compiler_params=_params(("parallel", "parallel")),
        name="attn",
    )(q, k, v)


def _mlstm_kernel(bif_ref, mq_ref, mk_ref, mv_ref, mo_ref, ig_ref, fg_ref, cwq_ref, cwk_ref, gn_ref, o_ref,
                  pad_ref, qs_ref, ks_ref, c_ref, n_ref):
    S = mq_ref.shape[1]
    LC = MLSTM_CHUNK
    Dh = M_HEAD_DIM
    h = pl.program_id(1)
    b_i = bif_ref[h]
    b_f = bif_ref[M_HEADS + h]

    def conv_silu(src_ref, w_ref):
        pad_ref[0:SUBLANES, :] = jnp.zeros((SUBLANES, Dh), F32)
        pad_ref[SUBLANES:SUBLANES + S, :] = src_ref[0].astype(F32)
        acc = None
        for j in range(CONV_WIDTH):
            off = SUBLANES - (CONV_WIDTH - 1) + j
            term = w_ref[j:j + 1, :] * pad_ref[off:off + S, :]
            acc = term if acc is None else acc + term
        return acc * jax.nn.sigmoid(acc)

    qs_ref[...] = conv_silu(mq_ref, cwq_ref).astype(BF16)
    ks_ref[...] = conv_silu(mk_ref, cwk_ref) * (Dh ** -0.5)

    c_ref[...] = jnp.zeros_like(c_ref)
    n_ref[...] = jnp.zeros_like(n_ref)

    row = lax.broadcasted_iota(jnp.int32, (LC, LC), 0)
    col = lax.broadcasted_iota(jnp.int32, (LC, LC), 1)
    causal = col <= row
    eye = col == row
    lane8 = lax.broadcasted_iota(jnp.int32, (SUBLANES, LC), 1)
    gn = gn_ref[...]

    def chunk(c, m):
        r0 = pl.multiple_of(c * LC, LC)
        q = qs_ref[pl.ds(r0, LC), :]
        k = ks_ref[pl.ds(r0, LC), :]
        v = mv_ref[0, pl.ds(r0, LC), :]
        i_row = ig_ref[0, 0, pl.ds(c, 1), :] + b_i
        f_row = fg_ref[0, 0, pl.ds(c, 1), :] + b_f
        logf = jnp.minimum(f_row, 0.0) - jnp.log(1.0 + jnp.exp(-jnp.abs(f_row)))
        b8 = jnp.broadcast_to(logf, (SUBLANES, LC))
        s = 1
        while s < LC:
            b8 = b8 + jnp.where(lane8 >= s, pltpu.roll(b8, s, 1), 0.0)
            s *= 2
        b_row = b8[0:1, :]
        d_row = i_row - b_row
        b_col = jnp.sum(jnp.where(eye, b_row, 0.0), axis=1, keepdims=True)
        d_col = jnp.sum(jnp.where(eye, d_row, 0.0), axis=1, keepdims=True)
        dm = jnp.where(causal, b_col + d_row, NEG)
        inter = b_col + m
        m_t = jnp.maximum(inter, jnp.max(dm, axis=1, keepdims=True))
        w_intra = jnp.exp(dm - m_t)
        w_inter = jnp.exp(inter - m_t)
        kb = k.astype(BF16)
        a = lax.dot_general(q, kb, (((1,), (1,)), ((), ())), preferred_element_type=F32) * w_intra
        num = (jnp.dot(a.astype(BF16), v, preferred_element_type=F32)
               + w_inter * jnp.dot(q, c_ref[...].astype(BF16), preferred_element_type=F32))
        den = (jnp.sum(a, axis=1, keepdims=True)
               + w_inter * jnp.sum(q.astype(F32) * n_ref[...], axis=1, keepdims=True))
        hh = num / jnp.maximum(jnp.abs(den), jnp.exp(-m_t))
        mu = jnp.mean(hh, axis=1, keepdims=True)
        xc = hh - mu
        var = jnp.mean(xc * xc, axis=1, keepdims=True)
        hn = xc * lax.rsqrt(var + LN_EPS) * gn
        gate = jax.nn.sigmoid(mo_ref[0, pl.ds(r0, LC), :].astype(F32))
        o_ref[0, pl.ds(r0, LC), :] = (gate * hn).astype(o_ref.dtype)
        b_last = b_row[:, LC - 1:LC]
        m_new = jnp.maximum(b_last + m, jnp.max(b_last + d_row, axis=1, keepdims=True))
        decay = jnp.exp(b_last + m - m_new)
        kw = k * jnp.exp(b_last + d_col - m_new)
        c_ref[...] = decay * c_ref[...] + lax.dot_general(
            kw.astype(BF16), v, (((0,), (0,)), ((), ())), preferred_element_type=F32)
        n_ref[...] = decay * n_ref[...] + jnp.sum(kw, axis=0, keepdims=True)
        return m_new

    lax.fori_loop(0, S // LC, chunk, jnp.zeros((1, 1), F32))


def _mlstm(proj3, gates, conv_q, conv_k, b_if, g_norm):
    B, S, _ = proj3.shape
    Dh = M_HEAD_DIM
    nC = S // MLSTM_CHUNK

    def col(base):
        return pl.BlockSpec((1, S, Dh), lambda b, h, s: (b, 0, base // Dh + h))

    grid_spec = pltpu.PrefetchScalarGridSpec(
        num_scalar_prefetch=1,
        grid=(B, M_HEADS),
        in_specs=[
            col(COL_MQ), col(COL_MK), col(COL_MV), col(COL_MO),
            pl.BlockSpec((1, 1, nC, MLSTM_CHUNK), lambda b, h, s: (b, h, 0, 0)),
            pl.BlockSpec((1, 1, nC, MLSTM_CHUNK), lambda b, h, s: (b, M_HEADS + h, 0, 0)),
            pl.BlockSpec((CONV_WIDTH, Dh), lambda b, h, s: (0, h)),
            pl.BlockSpec((CONV_WIDTH, Dh), lambda b, h, s: (0, h)),
            pl.BlockSpec((1, Dh), lambda b, h, s: (0, h)),
        ],
        out_specs=pl.BlockSpec((1, S, Dh), lambda b, h, s: (b, 0, h)),
        scratch_shapes=[
            pltpu.VMEM((S + SUBLANES, Dh), F32),
            pltpu.VMEM((S, Dh), BF16),
            pltpu.VMEM((S, Dh), F32),
            pltpu.VMEM((Dh, Dh), F32),
            pltpu.VMEM((1, Dh), F32),
        ],
    )
    return pl.pallas_call(
        _mlstm_kernel,
        grid_spec=grid_spec,
        out_shape=jax.ShapeDtypeStruct((B, S, M_WIDTH), BF16),
        compiler_params=_params(("parallel", "parallel")),
        name="mlstm",
    )(b_if, proj3, proj3, proj3, proj3, gates, gates, conv_q, conv_k, g_norm)


def _post_kernel(x_ref, ao_ref, lse_ref, mem_ref, ga_ref, gm_ref, wa_ref, wm_ref, wo_ref, gf_ref, wr_ref, br_ref,
                 h1_ref, xn_ref, idx_ref, gate_ref):
    tm = x_ref.shape[0]
    lse = lse_ref[...]
    heads = []
    for h in range(HEADS_PER_GROUP):
        ls = [lse[:, g * HEADS_PER_GROUP + h:g * HEADS_PER_GROUP + h + 1] for g in range(N_GROUPS)]
        mx = jnp.maximum(jnp.maximum(ls[0], ls[1]), ls[2])
        es = [jnp.exp(l - mx) for l in ls]
        tot = es[0] + es[1] + es[2]
        acc = None
        for g in range(N_GROUPS):
            c0 = g * ATTN_OUT + h * HEAD_DIM
            term = (es[g] / tot) * ao_ref[:, c0:c0 + HEAD_DIM].astype(F32)
            acc = term if acc is None else acc + term
        heads.append(acc.astype(BF16))
    attn = jnp.concatenate(heads, axis=1)
    ya = jnp.dot(attn, wa_ref[...], preferred_element_type=F32)
    ym = jnp.dot(mem_ref[...], wm_ref[...], preferred_element_type=F32)
    merged = (jax.nn.sigmoid(ga_ref[...].astype(F32)) * ya + jax.nn.sigmoid(gm_ref[...].astype(F32)) * ym)
    h1 = x_ref[...] + jnp.dot(merged.astype(BF16), wo_ref[...], preferred_element_type=F32)
    h1_ref[...] = h1
    ms = jnp.mean(h1 * h1, axis=-1, keepdims=True)
    xn = h1 * lax.rsqrt(ms + RMS_EPS) * gf_ref[...]
    xn_ref[...] = xn
    logits = jnp.dot(xn, wr_ref[...], preferred_element_type=F32, precision=lax.Precision.HIGHEST) + br_ref[...]
    lane = lax.broadcasted_iota(jnp.int32, (tm, LANES), 1).astype(F32)
    vals = logits
    tops, idxs = [], []
    for _ in range(TOP_K):
        mx = jnp.max(vals, axis=1, keepdims=True)
        ix = jnp.min(jnp.where(vals == mx, lane, float(LANES)), axis=1, keepdims=True)
        tops.append(mx)
        idxs.append(ix)
        vals = jnp.where(lane == ix, NEG, vals)
    es = [jnp.exp(t - tops[0]) for t in tops]
    tot = es[0] + es[1] + es[2] + es[3]
    idx_out = jnp.zeros((tm, LANES), F32)
    gate_out = jnp.zeros((tm, LANES), F32)
    for kk in range(TOP_K):
        idx_out = jnp.where(lane == float(kk), idxs[kk], idx_out)
        gate_out = jnp.where(lane == float(kk), es[kk] / tot, gate_out)
    idx_ref[...] = idx_out.astype(jnp.int32)
    gate_ref[...] = gate_out


def _post(x2, ao, lse, mem, proj, w_a, w_m, w_o, g_ffn, w_r, b_r):
    T, D = x2.shape
    tm = POST_TM
    row = lambda w: pl.BlockSpec((tm, w), lambda i: (i, 0))
    full = lambda a: pl.BlockSpec(a.shape, lambda i: (0,) * a.ndim)
    return pl.pallas_call(
        _post_kernel,
        grid=(T // tm,),
        in_specs=[
            row(D), row(N_GROUPS * ATTN_OUT), row(N_GROUPS * HEADS_PER_GROUP), row(M_WIDTH),
            pl.BlockSpec((tm, D), lambda i: (i, COL_GA // D)),
            pl.BlockSpec((tm, D), lambda i: (i, COL_GM // D)),
            full(w_a), full(w_m), full(w_o), full(g_ffn), full(w_r), full(b_r),
        ],
        out_specs=[row(D), row(D), row(LANES), row(LANES)],
        out_shape=[jax.ShapeDtypeStruct((T, D), F32), jax.ShapeDtypeStruct((T, D), F32),
                   jax.ShapeDtypeStruct((T, LANES), jnp.int32), jax.ShapeDtypeStruct((T, LANES), F32)],
        compiler_params=_params(("parallel",)),
        name="post",
    )(x2, ao, lse, mem, proj, proj, w_a, w_m, w_o, g_ffn, w_r, b_r)


def _expert_kernel(be_ref, nv_ref, nu_ref, idx_hbm, xn_hbm, wg_ref, wu_ref, wd_ref, bg_ref, bu_ref, bd_ref, y_hbm,
                   idx_smem, xbuf, ybuf, isem, gsem, ssem):
    i = pl.program_id(0)
    nb = pl.num_programs(0)
    n_used = nu_ref[0]
    BLK = MOE_BLOCK

    def idx_copy(blk, slot):
        return pltpu.make_async_copy(idx_hbm.at[blk], idx_smem.at[slot], isem.at[slot])

    def gather_start(islot, bslot):
        def body(r, carry):
            tok = idx_smem[islot, r]
            pltpu.make_async_copy(xn_hbm.at[pl.ds(tok, 1)], xbuf.at[bslot, pl.ds(r, 1)], gsem.at[bslot]).start()
            return carry
        lax.fori_loop(0, BLK, body, 0)

    def gather_wait(bslot):
        pltpu.make_async_copy(xn_hbm.at[pl.ds(0, BLK)], xbuf.at[bslot], gsem.at[bslot]).wait()

    def scatter_start(islot, bslot, n):
        def body(r, carry):
            dst = idx_smem[islot, BLK + r]
            pltpu.make_async_copy(ybuf.at[bslot, pl.ds(r, 1)], y_hbm.at[pl.ds(dst, 1)], ssem.at[bslot]).start()
            return carry
        lax.fori_loop(0, n, body, 0)

    def scatter_wait(bslot, n):
        n8 = pl.multiple_of((n // SUBLANES) * SUBLANES, SUBLANES)

        @pl.when(n8 > 0)
        def _():
            pltpu.make_async_copy(ybuf.at[bslot, pl.ds(0, n8)], y_hbm.at[pl.ds(0, n8)], ssem.at[bslot]).wait()

        def body(r, carry):
            pltpu.make_async_copy(ybuf.at[bslot, pl.ds(0, 1)], y_hbm.at[pl.ds(0, 1)], ssem.at[bslot]).wait()
            return carry
        lax.fori_loop(0, n - n8, body, 0)

    @pl.when(i == 0)
    def _():
        idx_copy(0, 0).start()
        idx_copy(1, 1).start()
        idx_copy(0, 0).wait()
        gather_start(0, 0)

    @pl.when(i + 2 < nb)
    def _():
        idx_copy(i + 2, (i + 2) % 3).start()

    @pl.when(i + 1 < nb)
    def _():
        idx_copy(i + 1, (i + 1) % 3).wait()

    @pl.when(i + 1 < n_used)
    def _():
        gather_start((i + 1) % 3, (i + 1) % 2)

    @pl.when(i < n_used)
    def _():
        bslot = i % 2
        gather_wait(bslot)
        xb = xbuf[bslot].astype(BF16)
        hg = jnp.dot(xb, wg_ref[0], preferred_element_type=F32) + bg_ref[0]
        hu = jnp.dot(xb, wu_ref[0], preferred_element_type=F32) + bu_ref[0]
        gate = jnp.minimum(hg, SWIGLU_LIMIT)
        up = jnp.clip(hu, -SWIGLU_LIMIT, SWIGLU_LIMIT)
        act = (up + 1.0) * (gate * jax.nn.sigmoid(SWIGLU_ALPHA * gate))
        ybuf[bslot] = jnp.dot(act.astype(BF16), wd_ref[0], preferred_element_type=F32) + bd_ref[0]
        scatter_start(i % 3, bslot, nv_ref[i])

        @pl.when(i >= 1)
        def _():
            scatter_wait(1 - bslot, nv_ref[jnp.maximum(i - 1, 0)])

        @pl.when(i == n_used - 1)
        def _():
            scatter_wait(bslot, nv_ref[i])


def _experts(block_e, n_valid, n_used, idx, xn2, wg, wu, wd, bg, bu, bd, n_out_rows):
    nb = idx.shape[0]
    D = xn2.shape[1]
    wspec = lambda: pl.BlockSpec((1, D, D), lambda i, be, nv, nu: (be[i], 0, 0))
    bspec = lambda: pl.BlockSpec((1, 1, D), lambda i, be, nv, nu: (be[i], 0, 0))
    grid_spec = pltpu.PrefetchScalarGridSpec(
        num_scalar_prefetch=3,
        grid=(nb,),
        in_specs=[
            pl.BlockSpec(memory_space=pl.ANY),
            pl.BlockSpec(memory_space=pl.ANY),
            wspec(), wspec(), wspec(), bspec(), bspec(), bspec(),
        ],
        out_specs=pl.BlockSpec(memory_space=pl.ANY),
        scratch_shapes=[
            pltpu.SMEM((3, 2 * MOE_BLOCK), jnp.int32),
            pltpu.VMEM((2, MOE_BLOCK, D), F32),
            pltpu.VMEM((2, MOE_BLOCK, D), F32),
            pltpu.SemaphoreType.DMA((3,)),
            pltpu.SemaphoreType.DMA((2,)),
            pltpu.SemaphoreType.DMA((2,)),
        ],
    )
    return pl.pallas_call(
        _expert_kernel,
        grid_spec=grid_spec,
        out_shape=jax.ShapeDtypeStruct((n_out_rows, D), F32),
        compiler_params=_params(("arbitrary",)),
        name="experts",
    )(block_e, n_valid, n_used, idx, xn2, wg, wu, wd, bg, bu, bd)


def _final_kernel(h1_ref, y0_ref, y1_ref, y2_ref, y3_ref, gate_ref, g_ref, o_ref):
    gts = gate_ref[...]
    h = h1_ref[...]
    for kk, y_ref in enumerate((y0_ref, y1_ref, y2_ref, y3_ref)):
        h = h + gts[:, kk:kk + 1] * y_ref[...]
    ms = jnp.mean(h * h, axis=-1, keepdims=True)
    o_ref[...] = h * lax.rsqrt(ms + RMS_EPS) * g_ref[...]


def _final(h1, y4, gates, g_final):
    T, D = h1.shape
    tm = FINAL_TM
    nt = T // tm
    yspec = lambda kk: pl.BlockSpec((tm, D), lambda i: (kk * nt + i, 0))
    return pl.pallas_call(
        _final_kernel,
        grid=(nt,),
        in_specs=[pl.BlockSpec((tm, D), lambda i: (i, 0)), yspec(0), yspec(1), yspec(2), yspec(3),
                  pl.BlockSpec((tm, LANES), lambda i: (i, 0)), pl.BlockSpec((1, D), lambda i: (0, 0))],
        out_specs=pl.BlockSpec((tm, D), lambda i: (i, 0)),
        out_shape=jax.ShapeDtypeStruct((T, D), F32),
        compiler_params=_params(("parallel",)),
        name="final",
    )(h1, y4, y4, y4, y4, gates, g_final)


def _rope_tables(positions):
    half = ROPE_DIM // 2
    inv_freq = ROPE_THETA ** (-jnp.arange(half, dtype=F32) / half)
    ang = positions.astype(F32).reshape(-1)[:, None] * inv_freq
    cos, sin = jnp.cos(ang), jnp.sin(ang)
    T = ang.shape[0]
    cos_t = jnp.concatenate([cos, cos, jnp.ones((T, HEAD_DIM - ROPE_DIM), F32)], axis=1)
    sa_t = jnp.concatenate([-sin, jnp.zeros((T, HEAD_DIM - half), F32)], axis=1)
    sb_t = jnp.concatenate([jnp.zeros((T, half), F32), sin, jnp.zeros((T, HEAD_DIM - ROPE_DIM), F32)], axis=1)
    return cos_t, sa_t, sb_t


def _to_residue_major(t, B, S):
    t = t.reshape(B, S, N_GROUPS, HEADS_PER_GROUP, HEAD_DIM)
    outs = []
    for g, d in enumerate(GROUP_DILATION):
        tg = t[:, :, g].reshape(B, S // d, d, HEADS_PER_GROUP, HEAD_DIM).transpose(0, 3, 2, 1, 4)
        outs.append(tg.reshape(B, HEADS_PER_GROUP, S, HEAD_DIM))
    return jnp.concatenate(outs, axis=1)


def _from_residue_major(o, lse, B, S):
    outs, lses = [], []
    for g, d in enumerate(GROUP_DILATION):
        og = o[:, g * HEADS_PER_GROUP:(g + 1) * HEADS_PER_GROUP].reshape(B, HEADS_PER_GROUP, d, S // d, HEAD_DIM)
        outs.append(og.transpose(0, 3, 2, 1, 4).reshape(B * S, ATTN_OUT))
        lg = lse[:, g * HEADS_PER_GROUP:(g + 1) * HEADS_PER_GROUP].reshape(B, HEADS_PER_GROUP, d, S // d)
        lses.append(lg.transpose(0, 3, 2, 1).reshape(B * S, HEADS_PER_GROUP))
    return jnp.concatenate(outs, axis=1), jnp.concatenate(lses, axis=1)


def _routing(top_idx, T):
    A = T * TOP_K
    BLK = MOE_BLOCK
    n_blocks = A // BLK + N_EXPERTS
    n_rows = n_blocks * BLK
    flat_e = top_idx.reshape(A)
    onehot = (flat_e[:, None] == jnp.arange(N_EXPERTS, dtype=jnp.int32)[None, :]).astype(jnp.int32)
    csum = jnp.cumsum(onehot, axis=0)
    rank = jnp.take_along_axis(csum, flat_e[:, None], axis=1)[:, 0] - 1
    counts = csum[-1]
    padded = (counts + BLK - 1) // BLK * BLK
    padded_ends = jnp.cumsum(padded)
    padded_starts = padded_ends - padded
    dest = padded_starts[flat_e] + rank
    flat_tok = jnp.repeat(jnp.arange(T, dtype=jnp.int32), TOP_K)
    flat_k = jnp.tile(jnp.arange(TOP_K, dtype=jnp.int32), T)
    row_tok = jnp.zeros((n_rows,), jnp.int32).at[dest].set(flat_tok)
    row_slot = jnp.zeros((n_rows,), jnp.int32).at[dest].set(flat_k * T + flat_tok)
    block_start = jnp.arange(n_blocks, dtype=jnp.int32) * BLK
    block_e = jnp.minimum(jnp.sum((padded_ends[None, :] <= block_start[:, None]).astype(jnp.int32), axis=1),
                          N_EXPERTS - 1)
    n_valid = jnp.clip(padded_starts[block_e] + counts[block_e] - block_start, 0, BLK).astype(jnp.int32)
    n_used = (padded_ends[-1] // BLK).astype(jnp.int32).reshape(1)
    n_valid = jnp.where(block_start < padded_ends[-1], n_valid, 0)
    idx = jnp.concatenate([row_tok.reshape(n_blocks, BLK), row_slot.reshape(n_blocks, BLK)], axis=1)
    return block_e, n_valid, n_used, idx, A


def kernel(x, positions, g_mix, w_in, conv_qk, b_if, g_mlstm_norm, w_attn_out, w_mlstm_out, w_mix_out, g_ffn,
           w_router, b_router, w_gate_up, b_gate_up, w_down, b_down, g_final):
    B, S, D = x.shape
    T = B * S
    l = 0
    x2 = x.reshape(T, D)

    w = w_in[l]
    o = np.cumsum((0, ATTN_WIDTH, ATTN_WIDTH, ATTN_WIDTH, M_WIDTH, M_WIDTH, M_WIDTH, M_WIDTH, M_HEADS, M_HEADS,
                   D_MODEL, D_MODEL))
    seg = lambda a: w[:, int(o[a]):int(o[a + 1])]
    w_main = jnp.concatenate([seg(0), seg(1), seg(9), seg(10), seg(2), seg(3), seg(4), seg(5), seg(6)],
                             axis=1).astype(BF16)
    w_if = jnp.pad(w[:, int(o[7]):int(o[9])], ((0, 0), (0, LANES - 2 * M_HEADS))).astype(BF16)
    cos_t, sa_t, sb_t = _rope_tables(positions)

    proj, ifo = _proj(x2, g_mix[l][None, :], w_main, w_if, cos_t, sa_t, sb_t)

    q = _to_residue_major(proj[:, COL_AQ:COL_AQ + ATTN_WIDTH], B, S)
    k = _to_residue_major(proj[:, COL_AK:COL_AK + ATTN_WIDTH], B, S)
    v = _to_residue_major(proj[:, COL_AV:COL_AV + ATTN_WIDTH], B, S)
    ao_rm, lse_rm = _attention(q, k, v)
    ao, lse = _from_residue_major(ao_rm, lse_rm.reshape(B, N_GROUPS * HEADS_PER_GROUP, S), B, S)

    nC = S // MLSTM_CHUNK
    gates = ifo[:, :2 * M_HEADS].reshape(B, S, 2 * M_HEADS).transpose(0, 2, 1).reshape(B, 2 * M_HEADS, nC, MLSTM_CHUNK)
    mem = _mlstm(proj.reshape(B, S, N_MAIN), gates, conv_qk[l][:, :M_WIDTH], conv_qk[l][:, M_WIDTH:], b_if[l],
                 g_mlstm_norm[l][None, :])

    w_r = jnp.pad(w_router[l], ((0, 0), (0, LANES - N_EXPERTS)))
    b_r = jnp.concatenate([b_router[l], jnp.full((LANES - N_EXPERTS,), NEG, F32)])[None, :]
    h1, xn2, top_idx, gates4 = _post(x2, ao, lse, mem.reshape(T, M_WIDTH), proj, w_attn_out[l].astype(BF16),
                                     w_mlstm_out[l].astype(BF16), w_mix_out[l].astype(BF16), g_ffn[l][None, :],
                                     w_r, b_r)

    block_e, n_valid, n_used, idx, n_out_rows = _routing(top_idx[:, :TOP_K], T)
    wg = w_gate_up[l][:, :, 0::2].astype(BF16)
    wu = w_gate_up[l][:, :, 1::2].astype(BF16)
    bg = b_gate_up[l][:, None, 0::2]
    bu = b_gate_up[l][:, None, 1::2]
    y4 = _experts(block_e, n_valid, n_used, idx, xn2, wg, wu, w_down[l].astype(BF16), bg, bu, b_down[l][:, None, :],
                  n_out_rows)

    out = _final(h1, y4, gates4, g_final[None, :])
    return out.reshape(B, S, D)
```

```python
import functools

import jax
import jax.numpy as jnp
import numpy as np
from jax import lax
from jax.experimental import pallas as pl
from jax.experimental.pallas import tpu as pltpu

F32 = jnp.float32
BF16 = jnp.bfloat16

D_MODEL = 1024
N_GROUPS = 3
GROUP_DILATION = (1, 4, 16)
HEADS_PER_GROUP = 4
HEAD_DIM = 128
ATTN_WIDTH = N_GROUPS * HEADS_PER_GROUP * HEAD_DIM
ATTN_OUT = HEADS_PER_GROUP * HEAD_DIM
ATTN_BLOCK = 128
ROPE_THETA = 500000.0
ROPE_DIM = HEAD_DIM // 4
M_HEADS = 4
M_WIDTH = D_MODEL
M_HEAD_DIM = M_WIDTH // M_HEADS
CONV_WIDTH = 4
N_EXPERTS = 32
TOP_K = 4
D_FF = D_MODEL
SWIGLU_LIMIT = 7.0
SWIGLU_ALPHA = 1.702
RMS_EPS = 1e-5
LN_EPS = 1e-5

LANES = 128
SUBLANES = 8
VMEM_LIMIT = 56 * 1024 * 1024

PROJ_TM = 1024
PROJ_TN = 1536
MLSTM_CHUNK = 128
POST_TM = 512
MOE_BLOCK = 256
FINAL_TM = 512
NEG = -1e30

COL_AQ = 0
COL_AK = ATTN_WIDTH
COL_GA = 2 * ATTN_WIDTH
COL_GM = COL_GA + D_MODEL
COL_AV = COL_GM + D_MODEL
COL_MQ = COL_AV + ATTN_WIDTH
COL_MK = COL_MQ + M_WIDTH
COL_MV = COL_MK + M_WIDTH
COL_MO = COL_MV + M_WIDTH
N_MAIN = COL_MO + M_WIDTH


def _params(sem):
    return pltpu.CompilerParams(dimension_semantics=sem, vmem_limit_bytes=VMEM_LIMIT)


def _proj_kernel(x_ref, g_ref, w_ref, wif_ref, cos_ref, sa_ref, sb_ref, o_ref, if_ref, xn_ref, *, rope_tiles):
    j = pl.program_id(1)

    @pl.when(j == 0)
    def _():
        x = x_ref[...]
        ms = jnp.mean(x * x, axis=-1, keepdims=True)
        xn = (x * lax.rsqrt(ms + RMS_EPS) * g_ref[...]).astype(BF16)
        xn_ref[...] = xn
        if_ref[...] = jnp.dot(xn, wif_ref[...], preferred_element_type=F32)

    acc = jnp.dot(xn_ref[...], w_ref[...], preferred_element_type=F32)

    @pl.when(j < rope_tiles)
    def _():
        c = cos_ref[...]
        sa = sa_ref[...]
        sb = sb_ref[...]
        for h in range(acc.shape[1] // HEAD_DIM):
            a = acc[:, h * HEAD_DIM:(h + 1) * HEAD_DIM]
            y = a * c + pltpu.roll(a, HEAD_DIM - ROPE_DIM // 2, 1) * sa + pltpu.roll(a, ROPE_DIM // 2, 1) * sb
            o_ref[:, h * HEAD_DIM:(h + 1) * HEAD_DIM] = y.astype(o_ref.dtype)

    @pl.when(j >= rope_tiles)
    def _():
        o_ref[...] = acc.astype(o_ref.dtype)


def _proj(x2, g_mix, w_main, w_if, cos_t, sa_t, sb_t):
    T, D = x2.shape
    N = w_main.shape[1]
    tm, tn = PROJ_TM, PROJ_TN
    grid = (T // tm, N // tn)
    return pl.pallas_call(
        functools.partial(_proj_kernel, rope_tiles=(2 * ATTN_WIDTH) // tn),
        grid=grid,
        in_specs=[
            pl.BlockSpec((tm, D), lambda i, j: (i, 0)),
            pl.BlockSpec((1, D), lambda i, j: (0, 0)),
            pl.BlockSpec((D, tn), lambda i, j: (0, j)),
            pl.BlockSpec((D, LANES), lambda i, j: (0, 0)),
            pl.BlockSpec((tm, LANES), lambda i, j: (i, 0)),
            pl.BlockSpec((tm, LANES), lambda i, j: (i, 0)),
            pl.BlockSpec((tm, LANES), lambda i, j: (i, 0)),
        ],
        out_specs=[
            pl.BlockSpec((tm, tn), lambda i, j: (i, j)),
            pl.BlockSpec((tm, LANES), lambda i, j: (i, 0)),
        ],
        out_shape=[jax.ShapeDtypeStruct((T, N), BF16), jax.ShapeDtypeStruct((T, LANES), F32)],
        scratch_shapes=[pltpu.VMEM((tm, D), BF16)],
        compiler_params=_params(("parallel", "arbitrary")),
        name="proj",
    )(x2, g_mix, w_main, w_if, cos_t, sa_t, sb_t)


def _attn_kernel(q_ref, k_ref, v_ref, o_ref, lse_ref):
    S = q_ref.shape[2]
    nblk = S // ATTN_BLOCK
    g = pl.program_id(1) // HEADS_PER_GROUP
    per_class = lax.shift_right_logical(jnp.int32(nblk), 2 * g)
    row = lax.broadcasted_iota(jnp.int32, (ATTN_BLOCK, ATTN_BLOCK), 0)
    col = lax.broadcasted_iota(jnp.int32, (ATTN_BLOCK, ATTN_BLOCK), 1)
    bias_cur = jnp.where(col <= row, 0.0, NEG).astype(F32)
    bias_prev = jnp.where(col >= row, 0.0, NEG).astype(F32)
    eye = col == row
    scale = HEAD_DIM ** -0.5
    dn = (((1,), (1,)), ((), ()))
    for n in range(nblk):
        sl = slice(n * ATTN_BLOCK, (n + 1) * ATTN_BLOCK)
        q = q_ref[0, 0, sl, :]
        s_c = lax.dot_general(q, k_ref[0, 0, sl, :], dn, preferred_element_type=F32) * scale + bias_cur
        m = jnp.max(s_c, axis=1, keepdims=True)
        if n > 0:
            sp = slice((n - 1) * ATTN_BLOCK, n * ATTN_BLOCK)
            has_prev = (jnp.int32(n) & (per_class - 1)) != 0
            gate = jnp.where(has_prev, 0.0, NEG).astype(F32)
            s_p = lax.dot_general(q, k_ref[0, 0, sp, :], dn, preferred_element_type=F32) * scale + (bias_prev + gate)
            m = jnp.maximum(m, jnp.max(s_p, axis=1, keepdims=True))
        p_c = jnp.exp(s_c - m)
        den = jnp.sum(p_c, axis=1, keepdims=True)
        acc = jnp.dot(p_c.astype(BF16), v_ref[0, 0, sl, :], preferred_element_type=F32)
        if n > 0:
            p_p = jnp.exp(s_p - m)
            den = den + jnp.sum(p_p, axis=1, keepdims=True)
            acc = acc + jnp.dot(p_p.astype(BF16), v_ref[0, 0, sp, :], preferred_element_type=F32)
        o_ref[0, 0, sl, :] = (acc / den).astype(o_ref.dtype)
        lse = m + jnp.log(den)
        lse_ref[0, 0, n:n + 1, :] = jnp.sum(jnp.where(eye, lse, 0.0), axis=0, keepdims=True)


def _attention(q, k, v):
    B, GH, S, Dh = q.shape
    nblk = S // ATTN_BLOCK
    spec = pl.BlockSpec((1, 1, S, Dh), lambda b, h: (b, h, 0, 0))
    return pl.pallas_call(
        _attn_kernel,
        grid=(B, GH),
        in_specs=[spec, spec, spec],
        out_specs=[spec, pl.BlockSpec((1, 1, nblk, ATTN_BLOCK), lambda b, h: (b, h, 0, 0))],
        out_shape=[jax.ShapeDtypeStruct((B, GH, S, Dh), BF16),
                   jax.ShapeDtypeStruct((B, GH, nblk, ATTN_BLOCK), F32)],
        compiler_params=_params(("parallel", "parallel")),
        name="attn",
    )(q, k, v)


def _mlstm_kernel(bif_ref, mq_ref, mk_ref, mv_ref, mo_ref, ig_ref, fg_ref, cwq_ref, cwk_ref, gn_ref, o_ref,
                  pad_ref, qs_ref, ks_ref, c_ref, n_ref):
    S = mq_ref.shape[1]
    LC = MLSTM_CHUNK
    Dh = M_HEAD_DIM
    h = pl.program_id(1)
    b_i = bif_ref[h]
    b_f = bif_ref[M_HEADS + h]

    def conv_silu(src_ref, w_ref):
        pad_ref[0:SUBLANES, :] = jnp.zeros((SUBLANES, Dh), F32)
        pad_ref[SUBLANES:SUBLANES + S, :] = src_ref[0].astype(F32)
        acc = None
        for j in range(CONV_WIDTH):
            off = SUBLANES - (CONV_WIDTH - 1) + j
            term = w_ref[j:j + 1, :] * pad_ref[off:off + S, :]
            acc = term if acc is None else acc + term
        return acc * jax.nn.sigmoid(acc)

    qs_ref[...] = conv_silu(mq_ref, cwq_ref).astype(BF16)
    ks_ref[...] = conv_silu(mk_ref, cwk_ref) * (Dh ** -0.5)

    c_ref[...] = jnp.zeros_like(c_ref)
    n_ref[...] = jnp.zeros_like(n_ref)

    row = lax.broadcasted_iota(jnp.int32, (LC, LC), 0)
    col = lax.broadcasted_iota(jnp.int32, (LC, LC), 1)
    causal = col <= row
    eye = col == row
    lane8 = lax.broadcasted_iota(jnp.int32, (SUBLANES, LC), 1)
    gn = gn_ref[...]

    def chunk(c, m):
        r0 = pl.multiple_of(c * LC, LC)
        q = qs_ref[pl.ds(r0, LC), :]
        k = ks_ref[pl.ds(r0, LC), :]
        v = mv_ref[0, pl.ds(r0, LC), :]
        i_row = ig_ref[0, 0, pl.ds(c, 1), :] + b_i
        f_row = fg_ref[0, 0, pl.ds(c, 1), :] + b_f
        logf = jnp.minimum(f_row, 0.0) - jnp.log(1.0 + jnp.exp(-jnp.abs(f_row)))
        b8 = jnp.broadcast_to(logf, (SUBLANES, LC))
        s = 1
        while s < LC:
            b8 = b8 + jnp.where(lane8 >= s, pltpu.roll(b8, s, 1), 0.0)
            s *= 2
        b_row = b8[0:1, :]
        d_row = i_row - b_row
        b_col = jnp.sum(jnp.where(eye, b_row, 0.0), axis=1, keepdims=True)
        d_col = jnp.sum(jnp.where(eye, d_row, 0.0), axis=1, keepdims=True)
        dm = jnp.where(causal, b_col + d_row, NEG)
        inter = b_col + m
        m_t = jnp.maximum(inter, jnp.max(dm, axis=1, keepdims=True))
        w_intra = jnp.exp(dm - m_t)
        w_inter = jnp.exp(inter - m_t)
        kb = k.astype(BF16)
        a = lax.dot_general(q, kb, (((1,), (1,)), ((), ())), preferred_element_type=F32) * w_intra
        num = (jnp.dot(a.astype(BF16), v, preferred_element_type=F32)
               + w_inter * jnp.dot(q, c_ref[...].astype(BF16), preferred_element_type=F32))
        den = (jnp.sum(a, axis=1, keepdims=True)
               + w_inter * jnp.sum(q.astype(F32) * n_ref[...], axis=1, keepdims=True))
        hh = num / jnp.maximum(jnp.abs(den), jnp.exp(-m_t))
        mu = jnp.mean(hh, axis=1, keepdims=True)
        xc = hh - mu
        var = jnp.mean(xc * xc, axis=1, keepdims=True)
        hn = xc * lax.rsqrt(var + LN_EPS) * gn
        gate = jax.nn.sigmoid(mo_ref[0, pl.ds(r0, LC), :].astype(F32))
        o_ref[0, pl.ds(r0, LC), :] = (gate * hn).astype(o_ref.dtype)
        b_last = b_row[:, LC - 1:LC]
        m_new = jnp.maximum(b_last + m, jnp.max(b_last + d_row, axis=1, keepdims=True))
        decay = jnp.exp(b_last + m - m_new)
        kw = k * jnp.exp(b_last + d_col - m_new)
        c_ref[...] = decay * c_ref[...] + lax.dot_general(
            kw.astype(BF16), v, (((0,), (0,)), ((), ())), preferred_element_type=F32)
        n_ref[...] = decay * n_ref[...] + jnp.sum(kw, axis=0, keepdims=True)
        return m_new

    lax.fori_loop(0, S // LC, chunk, jnp.zeros((1, 1), F32))


def _mlstm(proj3, gates, conv_q, conv_k, b_if, g_norm):
    B, S, _ = proj3.shape
    Dh = M_HEAD_DIM
    nC = S // MLSTM_CHUNK

    def col(base):
        return pl.BlockSpec((1, S, Dh), lambda b, h, s: (b, 0, base // Dh + h))

    grid_spec = pltpu.PrefetchScalarGridSpec(
        num_scalar_prefetch=1,
        grid=(B, M_HEADS),
        in_specs=[
            col(COL_MQ), col(COL_MK), col(COL_MV), col(COL_MO),
            pl.BlockSpec((1, 1, nC, MLSTM_CHUNK), lambda b, h, s: (b, h, 0, 0)),
            pl.BlockSpec((1, 1, nC, MLSTM_CHUNK), lambda b, h, s: (b, M_HEADS + h, 0, 0)),
            pl.BlockSpec((CONV_WIDTH, Dh), lambda b, h, s: (0, h)),
            pl.BlockSpec((CONV_WIDTH, Dh), lambda b, h, s: (0, h)),
            pl.BlockSpec((1, Dh), lambda b, h, s: (0, h)),
        ],
        out_specs=pl.BlockSpec((1, S, Dh), lambda b, h, s: (b, 0, h)),
        scratch_shapes=[
            pltpu.VMEM((S + SUBLANES, Dh), F32),
            pltpu.VMEM((S, Dh), BF16),
            pltpu.VMEM((S, Dh), F32),
            pltpu.VMEM((Dh, Dh), F32),
            pltpu.VMEM((1, Dh), F32),
        ],
    )
    return pl.pallas_call(
        _mlstm_kernel,
        grid_spec=grid_spec,
        out_shape=jax.ShapeDtypeStruct((B, S, M_WIDTH), BF16),
        compiler_params=_params(("parallel", "parallel")),
        name="mlstm",
    )(b_if, proj3, proj3, proj3, proj3, gates, gates, conv_q, conv_k, g_norm)


def _post_kernel(x_ref, ao_ref, lse_ref, mem_ref, ga_ref, gm_ref, wa_ref, wm_ref, wo_ref, gf_ref, wr_ref, br_ref,
                 h1_ref, xn_ref, idx_ref, gate_ref):
    tm = x_ref.shape[0]
    lse = lse_ref[...]
    heads = []
    for h in range(HEADS_PER_GROUP):
        ls = [lse[:, g * HEADS_PER_GROUP + h:g * HEADS_PER_GROUP + h + 1] for g in range(N_GROUPS)]
        mx = jnp.maximum(jnp.maximum(ls[0], ls[1]), ls[2])
        es = [jnp.exp(l - mx) for l in ls]
        tot = es[0] + es[1] + es[2]
        acc = None
        for g in range(N_GROUPS):
            c0 = g * ATTN_OUT + h * HEAD_DIM
            term = (es[g] / tot) * ao_ref[:, c0:c0 + HEAD_DIM].astype(F32)
            acc = term if acc is None else acc + term
        heads.append(acc.astype(BF16))
    attn = jnp.concatenate(heads, axis=1)
    ya = jnp.dot(attn, wa_ref[...], preferred_element_type=F32)
    ym = jnp.dot(mem_ref[...], wm_ref[...], preferred_element_type=F32)
    merged = (jax.nn.sigmoid(ga_ref[...].astype(F32)) * ya + jax.nn.sigmoid(gm_ref[...].astype(F32)) * ym)
    h1 = x_ref[...] + jnp.dot(merged.astype(BF16), wo_ref[...], preferred_element_type=F32)
    h1_ref[...] = h1
    ms = jnp.mean(h1 * h1, axis=-1, keepdims=True)
    xn = h1 * lax.rsqrt(ms + RMS_EPS) * gf_ref[...]
    for s in range(SUBLANES):
        xn_ref[pl.ds(s, tm, stride=SUBLANES), :] = xn[:, s * LANES:(s + 1) * LANES]
    logits = jnp.dot(xn, wr_ref[...], preferred_element_type=F32, precision=lax.Precision.HIGHEST) + br_ref[...]
    lane = lax.broadcasted_iota(jnp.int32, (tm, LANES), 1).astype(F32)
    vals = logits
    tops, idxs = [], []
    for _ in range(TOP_K):
        mx = jnp.max(vals, axis=1, keepdims=True)
        ix = jnp.min(jnp.where(vals == mx, lane, float(LANES)), axis=1, keepdims=True)
        tops.append(mx)
        idxs.append(ix)
        vals = jnp.where(lane == ix, NEG, vals)
    es = [jnp.exp(t - tops[0]) for t in tops]
    tot = es[0] + es[1] + es[2] + es[3]
    idx_out = jnp.zeros((tm, LANES), F32)
    gate_out = jnp.zeros((tm, LANES), F32)
    for kk in range(TOP_K):
        idx_out = jnp.where(lane == float(kk), idxs[kk], idx_out)
        gate_out = jnp.where(lane == float(kk), es[kk] / tot, gate_out)
    idx_ref[...] = idx_out.astype(jnp.int32)
    gate_ref[...] = gate_out


def _post(x2, ao, lse, mem, proj, w_a, w_m, w_o, g_ffn, w_r, b_r):
    T, D = x2.shape
    tm = POST_TM
    row = lambda w: pl.BlockSpec((tm, w), lambda i: (i, 0))
    full = lambda a: pl.BlockSpec(a.shape, lambda i: (0,) * a.ndim)
    return pl.pallas_call(
        _post_kernel,
        grid=(T // tm,),
        in_specs=[
            row(D), row(N_GROUPS * ATTN_OUT), row(N_GROUPS * HEADS_PER_GROUP), row(M_WIDTH),
            pl.BlockSpec((tm, D), lambda i: (i, COL_GA // D)),
            pl.BlockSpec((tm, D), lambda i: (i, COL_GM // D)),
            full(w_a), full(w_m), full(w_o), full(g_ffn), full(w_r), full(b_r),
        ],
        out_specs=[row(D), pl.BlockSpec((tm * SUBLANES, LANES), lambda i: (i, 0)), row(LANES), row(LANES)],
        out_shape=[jax.ShapeDtypeStruct((T, D), F32), jax.ShapeDtypeStruct((T * SUBLANES, LANES), F32),
                   jax.ShapeDtypeStruct((T, LANES), jnp.int32), jax.ShapeDtypeStruct((T, LANES), F32)],
        compiler_params=_params(("parallel",)),
        name="post",
    )(x2, ao, lse, mem, proj, proj, w_a, w_m, w_o, g_ffn, w_r, b_r)


def _expert_kernel(be_ref, nv_ref, nu_ref, idx_hbm, xn_hbm, wgu_ref, wd_ref, bgu_ref, bd_ref, y_hbm,
                   idx_smem, xbuf, ybuf, xb_ref, wgu_bf, wdx_bf, wd_tmp, isem, gsem, ssem):
    i = pl.program_id(0)
    n_idx = idx_hbm.shape[0]
    n_used = nu_ref[0]
    BLK = MOE_BLOCK
    n_islots = idx_smem.shape[0]

    def tile(t):
        return pl.ds(pl.multiple_of(t * SUBLANES, SUBLANES), SUBLANES)

    def tiles(n):
        return pl.ds(0, pl.multiple_of(n * SUBLANES, SUBLANES))

    def idx_copy(blk):
        slot = blk % n_islots
        return pltpu.make_async_copy(idx_hbm.at[blk], idx_smem.at[slot], isem.at[slot])

    def gather_start(blk):
        islot, bslot = blk % n_islots, blk % 2
        for r in range(BLK):
            tok = idx_smem[islot, r]
            pltpu.make_async_copy(xn_hbm.at[tile(tok)], xbuf.at[bslot, pl.ds(r * SUBLANES, SUBLANES)],
                                  gsem.at[bslot]).start()

    def gather_wait(blk):
        bslot = blk % 2
        pltpu.make_async_copy(xn_hbm.at[tiles(BLK)], xbuf.at[bslot], gsem.at[bslot]).wait()

    def scatter_start(blk, n):
        islot, bslot = blk % n_islots, blk % 2
        for r in range(BLK):
            @pl.when(r < n)
            def _():
                dst = idx_smem[islot, BLK + r]
                pltpu.make_async_copy(ybuf.at[bslot, pl.ds(r * SUBLANES, SUBLANES)], y_hbm.at[tile(dst)],
                                      ssem.at[bslot]).start()

    def scatter_wait(blk, n):
        bslot = blk % 2

        @pl.when(n > 0)
        def _():
            pltpu.make_async_copy(ybuf.at[bslot, tiles(n)], y_hbm.at[tiles(n)], ssem.at[bslot]).wait()

    def n_rows(blk):
        return jnp.where(blk >= 0, nv_ref[jnp.maximum(blk, 0)], 0)

    @pl.when(i == 0)
    def _():
        idx_copy(0).start()
        idx_copy(1).start()
        idx_copy(0).wait()
        gather_start(0)
        wd_tmp[...] = jnp.zeros_like(wd_tmp)

    @pl.when(i + 2 < n_idx)
    def _():
        idx_copy(i + 2).start()

    @pl.when(i + 1 < n_idx)
    def _():
        idx_copy(i + 1).wait()

    @pl.when(jnp.logical_and(i < n_used, jnp.logical_or(i == 0, be_ref[i] != be_ref[jnp.maximum(i - 1, 0)])))
    def _():
        wgu_bf[...] = wgu_ref[0].astype(BF16)
        rows = wd_tmp.shape[1] // 2
        for c in range(wd_ref.shape[1] // rows):
            for s in range(wd_ref.shape[2] // LANES):
                wd_tmp.at[s][pl.ds(0, rows, stride=2), :] = wd_ref[0, c * rows:(c + 1) * rows,
                                                                   s * LANES:(s + 1) * LANES]
            for s in range(wd_ref.shape[2] // LANES):
                wdx_bf[2 * c * rows:2 * (c + 1) * rows, s * LANES:(s + 1) * LANES] = wd_tmp[s].astype(BF16)

    @pl.when(i <= n_used)
    def _():
        gather_wait(i)
        scatter_wait(i - 2, n_rows(i - 2))

    @pl.when(i < n_used)
    def _():
        bslot = i % 2
        gather_start(i + 1)
        for s in range(SUBLANES):
            xb_ref[:, s * LANES:(s + 1) * LANES] = xbuf[bslot, pl.ds(s, BLK, stride=SUBLANES), :].astype(BF16)
        hg = jnp.dot(xb_ref[...], wgu_bf[...], preferred_element_type=F32) + bgu_ref[0]
        nxt = pltpu.roll(hg, hg.shape[1] - 1, 1)
        gate = jnp.minimum(hg, SWIGLU_LIMIT)
        up = jnp.clip(nxt, -SWIGLU_LIMIT, SWIGLU_LIMIT)
        act = (up + 1.0) * (gate * jax.nn.sigmoid(SWIGLU_ALPHA * gate))
        y = jnp.dot(act.astype(BF16), wdx_bf[...], preferred_element_type=F32) + bd_ref[0]
        for s in range(SUBLANES):
            ybuf[bslot, pl.ds(s, BLK, stride=SUBLANES), :] = y[:, s * LANES:(s + 1) * LANES]
        scatter_start(i - 1, n_rows(i - 1))

    @pl.when(i == n_used)
    def _():
        scatter_start(i - 1, n_rows(i - 1))
        scatter_wait(i - 1, n_rows(i - 1))


def _experts(block_e, n_valid, n_used, idx, xn2, wgu, wd, bgu, bd, n_out_rows):
    nb = idx.shape[0]
    D = wd.shape[2]
    F2 = wgu.shape[2]
    emap = lambda i, be, nv, nu: (be[i], 0, 0)
    grid_spec = pltpu.PrefetchScalarGridSpec(
        num_scalar_prefetch=3,
        grid=(nb,),
        in_specs=[
            pl.BlockSpec(memory_space=pl.ANY),
            pl.BlockSpec(memory_space=pl.ANY),
            pl.BlockSpec((1, D, F2), emap),
            pl.BlockSpec((1, F2 // 2, D), emap),
            pl.BlockSpec((1, 1, F2), emap),
            pl.BlockSpec((1, 1, D), emap),
        ],
        out_specs=pl.BlockSpec(memory_space=pl.ANY),
        scratch_shapes=[
            pltpu.SMEM((4, 2 * MOE_BLOCK), jnp.int32),
            pltpu.VMEM((2, MOE_BLOCK * SUBLANES, LANES), F32),
            pltpu.VMEM((2, MOE_BLOCK * SUBLANES, LANES), F32),
            pltpu.VMEM((MOE_BLOCK, D), BF16),
            pltpu.VMEM((D, F2), BF16),
            pltpu.VMEM((F2, D), BF16),
            pltpu.VMEM((D // LANES, F2 // 4, LANES), F32),
            pltpu.SemaphoreType.DMA((4,)),
            pltpu.SemaphoreType.DMA((2,)),
            pltpu.SemaphoreType.DMA((2,)),
        ],
    )
    return pl.pallas_call(
        _expert_kernel,
        grid_spec=grid_spec,
        out_shape=jax.ShapeDtypeStruct((n_out_rows * SUBLANES, LANES), F32),
        compiler_params=_params(("arbitrary",)),
        name="experts",
    )(block_e, n_valid, n_used, idx, xn2, wgu, wd, bgu, bd)


def _final_kernel(h1_ref, y0_ref, y1_ref, y2_ref, y3_ref, gate_ref, g_ref, o_ref):
    tm = h1_ref.shape[0]
    gts = gate_ref[...]
    g4 = [gts[:, kk:kk + 1] for kk in range(TOP_K)]
    pieces = []
    ss = None
    for s in range(SUBLANES):
        h = h1_ref[:, s * LANES:(s + 1) * LANES]
        for kk, y_ref in enumerate((y0_ref, y1_ref, y2_ref, y3_ref)):
            h = h + g4[kk] * y_ref[pl.ds(s, tm, stride=SUBLANES), :]
        pieces.append(h)
        sq = jnp.sum(h * h, axis=-1, keepdims=True)
        ss = sq if ss is None else ss + sq
    inv = lax.rsqrt(ss / h1_ref.shape[1] + RMS_EPS)
    for s in range(SUBLANES):
        o_ref[:, s * LANES:(s + 1) * LANES] = pieces[s] * inv * g_ref[:, s * LANES:(s + 1) * LANES]


def _final(h1, y4, gates, g_final):
    T, D = h1.shape
    tm = FINAL_TM
    nt = T // tm
    yspec = lambda kk: pl.BlockSpec((tm * SUBLANES, LANES), lambda i: (kk * nt + i, 0))
    return pl.pallas_call(
        _final_kernel,
        grid=(nt,),
        in_specs=[pl.BlockSpec((tm, D), lambda i: (i, 0)), yspec(0), yspec(1), yspec(2), yspec(3),
                  pl.BlockSpec((tm, LANES), lambda i: (i, 0)), pl.BlockSpec((1, D), lambda i: (0, 0))],
        out_specs=pl.BlockSpec((tm, D), lambda i: (i, 0)),
        out_shape=jax.ShapeDtypeStruct((T, D), F32),
        compiler_params=_params(("parallel",)),
        name="final",
    )(h1, y4, y4, y4, y4, gates, g_final)


def _rope_tables(positions):
    half = ROPE_DIM // 2
    inv_freq = ROPE_THETA ** (-jnp.arange(half, dtype=F32) / half)
    ang = positions.astype(F32).reshape(-1)[:, None] * inv_freq
    cos, sin = jnp.cos(ang), jnp.sin(ang)
    T = ang.shape[0]
    cos_t = jnp.concatenate([cos, cos, jnp.ones((T, HEAD_DIM - ROPE_DIM), F32)], axis=1)
    sa_t = jnp.concatenate([-sin, jnp.zeros((T, HEAD_DIM - half), F32)], axis=1)
    sb_t = jnp.concatenate([jnp.zeros((T, half), F32), sin, jnp.zeros((T, HEAD_DIM - ROPE_DIM), F32)], axis=1)
    return cos_t, sa_t, sb_t


def _to_residue_major(t, B, S):
    t = t.reshape(B, S, N_GROUPS, HEADS_PER_GROUP, HEAD_DIM)
    outs = []
    for g, d in enumerate(GROUP_DILATION):
        tg = t[:, :, g].reshape(B, S // d, d, HEADS_PER_GROUP, HEAD_DIM).transpose(0, 3, 2, 1, 4)
        outs.append(tg.reshape(B, HEADS_PER_GROUP, S, HEAD_DIM))
    return jnp.concatenate(outs, axis=1)


def _from_residue_major(o, lse, B, S):
    outs, lses = [], []
    for g, d in enumerate(GROUP_DILATION):
        og = o[:, g * HEADS_PER_GROUP:(g + 1) * HEADS_PER_GROUP].reshape(B, HEADS_PER_GROUP, d, S // d, HEAD_DIM)
        outs.append(og.transpose(0, 3, 2, 1, 4).reshape(B * S, ATTN_OUT))
        lg = lse[:, g * HEADS_PER_GROUP:(g + 1) * HEADS_PER_GROUP].reshape(B, HEADS_PER_GROUP, d, S // d)
        lses.append(lg.transpose(0, 3, 2, 1).reshape(B * S, HEADS_PER_GROUP))
    return jnp.concatenate(outs, axis=1), jnp.concatenate(lses, axis=1)


def _routing(top_idx, T):
    A = T * TOP_K
    BLK = MOE_BLOCK
    n_blocks = A // BLK + N_EXPERTS
    n_rows = n_blocks * BLK
    flat_e = top_idx.reshape(A)
    onehot = (flat_e[:, None] == jnp.arange(N_EXPERTS, dtype=jnp.int32)[None, :]).astype(jnp.int32)
    csum = jnp.cumsum(onehot, axis=0)
    rank = jnp.take_along_axis(csum, flat_e[:, None], axis=1)[:, 0] - 1
    counts = csum[-1]
    padded = (counts + BLK - 1) // BLK * BLK
    padded_ends = jnp.cumsum(padded)
    padded_starts = padded_ends - padded
    dest = padded_starts[flat_e] + rank
    row_a = jnp.zeros((n_rows,), jnp.int32).at[dest].set(jnp.arange(A, dtype=jnp.int32), unique_indices=True)
    row_tok = row_a // TOP_K
    row_slot = (row_a % TOP_K) * T + row_tok
    block_start = jnp.arange(n_blocks, dtype=jnp.int32) * BLK
    block_e = jnp.minimum(jnp.sum((padded_ends[None, :] <= block_start[:, None]).astype(jnp.int32), axis=1),
                          N_EXPERTS - 1)
    n_valid = jnp.clip(padded_starts[block_e] + counts[block_e] - block_start, 0, BLK).astype(jnp.int32)
    n_used = (padded_ends[-1] // BLK).astype(jnp.int32).reshape(1)
    n_valid = jnp.where(block_start < padded_ends[-1], n_valid, 0)
    idx = jnp.concatenate([row_tok.reshape(n_blocks, BLK), row_slot.reshape(n_blocks, BLK)], axis=1)
    block_e = jnp.concatenate([block_e, block_e[-1:]])
    n_valid = jnp.concatenate([n_valid, jnp.zeros((1,), jnp.int32)])
    idx = jnp.concatenate([idx, jnp.zeros((1, 2 * BLK), jnp.int32)], axis=0)
    return block_e, n_valid, n_used, idx, A


def kernel(x, positions, g_mix, w_in, conv_qk, b_if, g_mlstm_norm, w_attn_out, w_mlstm_out, w_mix_out, g_ffn,
           w_router, b_router, w_gate_up, b_gate_up, w_down, b_down, g_final):
    B, S, D = x.shape
    T = B * S
    l = 0
    x2 = x.reshape(T, D)

    w = w_in[l]
    o = np.cumsum((0, ATTN_WIDTH, ATTN_WIDTH, ATTN_WIDTH, M_WIDTH, M_WIDTH, M_WIDTH, M_WIDTH, M_HEADS, M_HEADS,
                   D_MODEL, D_MODEL))
    seg = lambda a: w[:, int(o[a]):int(o[a + 1])]
    w_main = jnp.concatenate([seg(0), seg(1), seg(9), seg(10), seg(2), seg(3), seg(4), seg(5), seg(6)],
                             axis=1).astype(BF16)
    w_if = jnp.pad(w[:, int(o[7]):int(o[9])], ((0, 0), (0, LANES - 2 * M_HEADS))).astype(BF16)
    cos_t, sa_t, sb_t = _rope_tables(positions)

    proj, ifo = _proj(x2, g_mix[l][None, :], w_main, w_if, cos_t, sa_t, sb_t)

    q = _to_residue_major(proj[:, COL_AQ:COL_AQ + ATTN_WIDTH], B, S)
    k = _to_residue_major(proj[:, COL_AK:COL_AK + ATTN_WIDTH], B, S)
    v = _to_residue_major(proj[:, COL_AV:COL_AV + ATTN_WIDTH], B, S)
    ao_rm, lse_rm = _attention(q, k, v)
    ao, lse = _from_residue_major(ao_rm, lse_rm.reshape(B, N_GROUPS * HEADS_PER_GROUP, S), B, S)

    nC = S // MLSTM_CHUNK
    gates = ifo[:, :2 * M_HEADS].reshape(B, S, 2 * M_HEADS).transpose(0, 2, 1).reshape(B, 2 * M_HEADS, nC, MLSTM_CHUNK)
    mem = _mlstm(proj.reshape(B, S, N_MAIN), gates, conv_qk[l][:, :M_WIDTH], conv_qk[l][:, M_WIDTH:], b_if[l],
                 g_mlstm_norm[l][None, :])

    w_r = jnp.pad(w_router[l], ((0, 0), (0, LANES - N_EXPERTS)))
    b_r = jnp.concatenate([b_router[l], jnp.full((LANES - N_EXPERTS,), NEG, F32)])[None, :]
    h1, xn2, top_idx, gates4 = _post(x2, ao, lse, mem.reshape(T, M_WIDTH), proj, w_attn_out[l].astype(BF16),
                                     w_mlstm_out[l].astype(BF16), w_mix_out[l].astype(BF16), g_ffn[l][None, :],
                                     w_r, b_r)

    block_e, n_valid, n_used, idx, n_out_rows = _routing(top_idx[:, :TOP_K], T)
    y4 = _experts(block_e, n_valid, n_used, idx, xn2, w_gate_up[l], w_down[l], b_gate_up[l][:, None, :],
                  b_down[l][:, None, :], n_out_rows)

    out = _final(h1, y4, gates4, g_final[None, :])
    return out.reshape(B, S, D)
```

```python
import functools

import jax
import jax.numpy as jnp
import numpy as np
from jax import lax
from jax.experimental import pallas as pl
from jax.experimental.pallas import tpu as pltpu

F32 = jnp.float32
BF16 = jnp.bfloat16

D_MODEL = 1024
N_GROUPS = 3
GROUP_DILATION = (1, 4, 16)
HEADS_PER_GROUP = 4
HEAD_DIM = 128
ATTN_WIDTH = N_GROUPS * HEADS_PER_GROUP * HEAD_DIM
ATTN_OUT = HEADS_PER_GROUP * HEAD_DIM
ATTN_BLOCK = 128
ROPE_THETA = 500000.0
ROPE_DIM = HEAD_DIM // 4
M_HEADS = 4
M_WIDTH = D_MODEL
M_HEAD_DIM = M_WIDTH // M_HEADS
CONV_WIDTH = 4
N_EXPERTS = 32
TOP_K = 4
D_FF = D_MODEL
SWIGLU_LIMIT = 7.0
SWIGLU_ALPHA = 1.702
RMS_EPS = 1e-5
LN_EPS = 1e-5

LANES = 128
SUBLANES = 8
VMEM_LIMIT = 56 * 1024 * 1024

PROJ_TM = 1024
PROJ_TN = 1536
MLSTM_CHUNK = 128
POST_TM = 512
MOE_BLOCK = 256
FINAL_TM = 512
NEG = -1e30

COL_AQ = 0
COL_AK = ATTN_WIDTH
COL_GA = 2 * ATTN_WIDTH
COL_GM = COL_GA + D_MODEL
COL_AV = COL_GM + D_MODEL
COL_MQ = COL_AV + ATTN_WIDTH
COL_MK = COL_MQ + M_WIDTH
COL_MV = COL_MK + M_WIDTH
COL_MO = COL_MV + M_WIDTH
N_MAIN = COL_MO + M_WIDTH


def _params(sem):
    return pltpu.CompilerParams(dimension_semantics=sem, vmem_limit_bytes=VMEM_LIMIT)


def _proj_kernel(x_ref, g_ref, w_ref, wif_ref, cos_ref, sa_ref, sb_ref, o_ref, if_ref, xn_ref, *, rope_tiles):
    j = pl.program_id(1)

    @pl.when(j == 0)
    def _():
        x = x_ref[...]
        ms = jnp.mean(x * x, axis=-1, keepdims=True)
        xn = (x * lax.rsqrt(ms + RMS_EPS) * g_ref[...]).astype(BF16)
        xn_ref[...] = xn
        if_ref[...] = jnp.dot(xn, wif_ref[...], preferred_element_type=F32)

    acc = jnp.dot(xn_ref[...], w_ref[...], preferred_element_type=F32)

    @pl.when(j < rope_tiles)
    def _():
        c = cos_ref[...]
        sa = sa_ref[...]
        sb = sb_ref[...]
        for h in range(acc.shape[1] // HEAD_DIM):
            a = acc[:, h * HEAD_DIM:(h + 1) * HEAD_DIM]
            y = a * c + pltpu.roll(a, HEAD_DIM - ROPE_DIM // 2, 1) * sa + pltpu.roll(a, ROPE_DIM // 2, 1) * sb
            o_ref[:, h * HEAD_DIM:(h + 1) * HEAD_DIM] = y.astype(o_ref.dtype)

    @pl.when(j >= rope_tiles)
    def _():
        o_ref[...] = acc.astype(o_ref.dtype)


def _proj(x2, g_mix, w_main, w_if, cos_t, sa_t, sb_t):
    T, D = x2.shape
    N = w_main.shape[1]
    tm, tn = PROJ_TM, PROJ_TN
    grid = (T // tm, N // tn)
    return pl.pallas_call(
        functools.partial(_proj_kernel, rope_tiles=(2 * ATTN_WIDTH) // tn),
        grid=grid,
        in_specs=[
            pl.BlockSpec((tm, D), lambda i, j: (i, 0)),
            pl.BlockSpec((1, D), lambda i, j: (0, 0)),
            pl.BlockSpec((D, tn), lambda i, j: (0, j)),
            pl.BlockSpec((D, LANES), lambda i, j: (0, 0)),
            pl.BlockSpec((tm, LANES), lambda i, j: (i, 0)),
            pl.BlockSpec((tm, LANES), lambda i, j: (i, 0)),
            pl.BlockSpec((tm, LANES), lambda i, j: (i, 0)),
        ],
        out_specs=[
            pl.BlockSpec((tm, tn), lambda i, j: (i, j)),
            pl.BlockSpec((tm, LANES), lambda i, j: (i, 0)),
        ],
        out_shape=[jax.ShapeDtypeStruct((T, N), BF16), jax.ShapeDtypeStruct((T, LANES), F32)],
        scratch_shapes=[pltpu.VMEM((tm, D), BF16)],
        compiler_params=_params(("parallel", "arbitrary")),
        name="proj",
    )(x2, g_mix, w_main, w_if, cos_t, sa_t, sb_t)


def _attn_kernel(q_ref, k_ref, v_ref, o_ref, lse_ref, stage_ref, q_rm, k_rm, v_rm, o_rm, *, dilation):
    S = q_ref.shape[1]
    d = dilation
    L = S // d
    per_class = L // ATTN_BLOCK
    nblk = S // ATTN_BLOCK
    h = pl.program_id(1)

    if d == 1:
        q_of = lambda sl: q_ref[0, sl, :]
        k_of = lambda sl: k_ref[0, sl, :]
        v_of = lambda sl: v_ref[0, sl, :]
    else:
        for src, dst in ((q_ref, q_rm), (k_ref, k_rm), (v_ref, v_rm)):
            stage_ref[...] = src[0].astype(F32)
            if d <= SUBLANES:
                for r in range(d):
                    dst[r * L:(r + 1) * L, :] = stage_ref[pl.ds(r, L, stride=d), :].astype(BF16)
            else:
                d0 = int(round(d ** 0.5))
                assert d0 * d0 == d and d0 <= SUBLANES
                for a in range(d0):
                    o_rm[a * (S // d0):(a + 1) * (S // d0), :] = stage_ref[pl.ds(a, S // d0, stride=d0), :]
                for a in range(d0):
                    for b in range(d0):
                        r = a + d0 * b
                        dst[r * L:(r + 1) * L, :] = o_rm[pl.ds(a * (S // d0) + b, L, stride=d0), :].astype(BF16)
        q_of = lambda sl: q_rm[sl, :]
        k_of = lambda sl: k_rm[sl, :]
        v_of = lambda sl: v_rm[sl, :]

    @pl.when(h == 0)
    def _():
        lse_ref[...] = jnp.zeros_like(lse_ref)

    row = lax.broadcasted_iota(jnp.int32, (ATTN_BLOCK, ATTN_BLOCK), 0)
    col = lax.broadcasted_iota(jnp.int32, (ATTN_BLOCK, ATTN_BLOCK), 1)
    bias_cur = jnp.where(col <= row, 0.0, NEG).astype(F32)
    bias_prev = jnp.where(col >= row, 0.0, NEG).astype(F32)
    my_lane = col == h
    scale = HEAD_DIM ** -0.5
    dn = (((1,), (1,)), ((), ()))
    for n in range(nblk):
        sl = slice(n * ATTN_BLOCK, (n + 1) * ATTN_BLOCK)
        has_prev = n % per_class != 0
        q = q_of(sl)
        s_c = lax.dot_general(q, k_of(sl), dn, preferred_element_type=F32) * scale + bias_cur
        m = jnp.max(s_c, axis=1, keepdims=True)
        if has_prev:
            sp = slice((n - 1) * ATTN_BLOCK, n * ATTN_BLOCK)
            s_p = lax.dot_general(q, k_of(sp), dn, preferred_element_type=F32) * scale + bias_prev
            m = jnp.maximum(m, jnp.max(s_p, axis=1, keepdims=True))
        p_c = jnp.exp(s_c - m)
        den = jnp.sum(p_c, axis=1, keepdims=True)
        acc = jnp.dot(p_c.astype(BF16), v_of(sl), preferred_element_type=F32)
        if has_prev:
            p_p = jnp.exp(s_p - m)
            den = den + jnp.sum(p_p, axis=1, keepdims=True)
            acc = acc + jnp.dot(p_p.astype(BF16), v_of(sp), preferred_element_type=F32)
        lse = m + jnp.log(den)
        start = n // per_class + d * (n % per_class) * ATTN_BLOCK
        rows = pl.ds(start, ATTN_BLOCK, stride=d) if d > 1 else sl
        o_rm[rows, :] = acc / den
        lse_ref[0, rows, :] = jnp.where(my_lane, lse, lse_ref[0, rows, :])
    o_ref[0] = o_rm[...].astype(o_ref.dtype)


def _attention(proj3, g):
    B, S, _ = proj3.shape
    Dh = HEAD_DIM

    def col(base):
        return pl.BlockSpec((1, S, Dh), lambda b, h: (b, 0, (base + g * ATTN_OUT) // Dh + h))

    return pl.pallas_call(
        functools.partial(_attn_kernel, dilation=GROUP_DILATION[g]),
        grid=(B, HEADS_PER_GROUP),
        in_specs=[col(COL_AQ), col(COL_AK), col(COL_AV)],
        out_specs=[pl.BlockSpec((1, S, Dh), lambda b, h: (b, 0, h)),
                   pl.BlockSpec((1, S, LANES), lambda b, h: (b, 0, 0))],
        out_shape=[jax.ShapeDtypeStruct((B, S, ATTN_OUT), BF16), jax.ShapeDtypeStruct((B, S, LANES), F32)],
        scratch_shapes=[pltpu.VMEM((S, Dh), F32), pltpu.VMEM((S, Dh), BF16), pltpu.VMEM((S, Dh), BF16),
                        pltpu.VMEM((S, Dh), BF16), pltpu.VMEM((S, Dh), F32)],
        compiler_params=_params(("parallel", "arbitrary")),
        name=f"attn{g}",
    )(proj3, proj3, proj3)


def _mlstm_kernel(bif_ref, mq_ref, mk_ref, mv_ref, mo_ref, ig_ref, fg_ref, cwq_ref, cwk_ref, gn_ref, o_ref,
                  pad_ref, qs_ref, ks_ref, c_ref, n_ref):
    S = mq_ref.shape[1]
    LC = MLSTM_CHUNK
    Dh = M_HEAD_DIM
    h = pl.program_id(1)
    b_i = bif_ref[h]
    b_f = bif_ref[M_HEADS + h]

    def conv_silu(src_ref, w_ref):
        pad_ref[0:SUBLANES, :] = jnp.zeros((SUBLANES, Dh), F32)
        pad_ref[SUBLANES:SUBLANES + S, :] = src_ref[0].astype(F32)
        acc = None
        for j in range(CONV_WIDTH):
            off = SUBLANES - (CONV_WIDTH - 1) + j
            term = w_ref[j:j + 1, :] * pad_ref[off:off + S, :]
            acc = term if acc is None else acc + term
        return acc * jax.nn.sigmoid(acc)

    qs_ref[...] = conv_silu(mq_ref, cwq_ref).astype(BF16)
    ks_ref[...] = conv_silu(mk_ref, cwk_ref) * (Dh ** -0.5)

    c_ref[...] = jnp.zeros_like(c_ref)
    n_ref[...] = jnp.zeros_like(n_ref)

    row = lax.broadcasted_iota(jnp.int32, (LC, LC), 0)
    col = lax.broadcasted_iota(jnp.int32, (LC, LC), 1)
    causal = col <= row
    eye = col == row
    lane8 = lax.broadcasted_iota(jnp.int32, (SUBLANES, LC), 1)
    gn = gn_ref[...]

    def chunk(c, m):
        r0 = pl.multiple_of(c * LC, LC)
        q = qs_ref[pl.ds(r0, LC), :]
        k = ks_ref[pl.ds(r0, LC), :]
        v = mv_ref[0, pl.ds(r0, LC), :]
        i_row = ig_ref[0, 0, pl.ds(c, 1), :] + b_i
        f_row = fg_ref[0, 0, pl.ds(c, 1), :] + b_f
        logf = jnp.minimum(f_row, 0.0) - jnp.log(1.0 + jnp.exp(-jnp.abs(f_row)))
        b8 = jnp.broadcast_to(logf, (SUBLANES, LC))
        s = 1
        while s < LC:
            b8 = b8 + jnp.where(lane8 >= s, pltpu.roll(b8, s, 1), 0.0)
            s *= 2
        b_row = b8[0:1, :]
        d_row = i_row - b_row
        b_col = jnp.sum(jnp.where(eye, b_row, 0.0), axis=1, keepdims=True)
        d_col = jnp.sum(jnp.where(eye, d_row, 0.0), axis=1, keepdims=True)
        dm = jnp.where(causal, b_col + d_row, NEG)
        inter = b_col + m
        m_t = jnp.maximum(inter, jnp.max(dm, axis=1, keepdims=True))
        w_intra = jnp.exp(dm - m_t)
        w_inter = jnp.exp(inter - m_t)
        kb = k.astype(BF16)
        a = lax.dot_general(q, kb, (((1,), (1,)), ((), ())), preferred_element_type=F32) * w_intra
        num = (jnp.dot(a.astype(BF16), v, preferred_element_type=F32)
               + w_inter * jnp.dot(q, c_ref[...].astype(BF16), preferred_element_type=F32))
        den = (jnp.sum(a, axis=1, keepdims=True)
               + w_inter * jnp.sum(q.astype(F32) * n_ref[...], axis=1, keepdims=True))
        hh = num / jnp.maximum(jnp.abs(den), jnp.exp(-m_t))
        mu = jnp.mean(hh, axis=1, keepdims=True)
        xc = hh - mu
        var = jnp.mean(xc * xc, axis=1, keepdims=True)
        hn = xc * lax.rsqrt(var + LN_EPS) * gn
        gate = jax.nn.sigmoid(mo_ref[0, pl.ds(r0, LC), :].astype(F32))
        o_ref[0, pl.ds(r0, LC), :] = (gate * hn).astype(o_ref.dtype)
        b_last = b_row[:, LC - 1:LC]
        m_new = jnp.maximum(b_last + m, jnp.max(b_last + d_row, axis=1, keepdims=True))
        decay = jnp.exp(b_last + m - m_new)
        kw = k * jnp.exp(b_last + d_col - m_new)
        c_ref[...] = decay * c_ref[...] + lax.dot_general(
            kw.astype(BF16), v, (((0,), (0,)), ((), ())), preferred_element_type=F32)
        n_ref[...] = decay * n_ref[...] + jnp.sum(kw, axis=0, keepdims=True)
        return m_new

    lax.fori_loop(0, S // LC, chunk, jnp.zeros((1, 1), F32))


def _mlstm(proj3, gates, conv_q, conv_k, b_if, g_norm):
    B, S, _ = proj3.shape
    Dh = M_HEAD_DIM
    nC = S // MLSTM_CHUNK

    def col(base):
        return pl.BlockSpec((1, S, Dh), lambda b, h, s: (b, 0, base // Dh + h))

    grid_spec = pltpu.PrefetchScalarGridSpec(
        num_scalar_prefetch=1,
        grid=(B, M_HEADS),
        in_specs=[
            col(COL_MQ), col(COL_MK), col(COL_MV), col(COL_MO),
            pl.BlockSpec((1, 1, nC, MLSTM_CHUNK), lambda b, h, s: (b, h, 0, 0)),
            pl.BlockSpec((1, 1, nC, MLSTM_CHUNK), lambda b, h, s: (b, M_HEADS + h, 0, 0)),
            pl.BlockSpec((CONV_WIDTH, Dh), lambda b, h, s: (0, h)),
            pl.BlockSpec((CONV_WIDTH, Dh), lambda b, h, s: (0, h)),
            pl.BlockSpec((1, Dh), lambda b, h, s: (0, h)),
        ],
        out_specs=pl.BlockSpec((1, S, Dh), lambda b, h, s: (b, 0, h)),
        scratch_shapes=[
            pltpu.VMEM((S + SUBLANES, Dh), F32),
            pltpu.VMEM((S, Dh), BF16),
            pltpu.VMEM((S, Dh), F32),
            pltpu.VMEM((Dh, Dh), F32),
            pltpu.VMEM((1, Dh), F32),
        ],
    )
    return pl.pallas_call(
        _mlstm_kernel,
        grid_spec=grid_spec,
        out_shape=jax.ShapeDtypeStruct((B, S, M_WIDTH), BF16),
        compiler_params=_params(("parallel", "parallel")),
        name="mlstm",
    )(b_if, proj3, proj3, proj3, proj3, gates, gates, conv_q, conv_k, g_norm)


def _post_kernel(x_ref, ao0_ref, ao1_ref, ao2_ref, lse0_ref, lse1_ref, lse2_ref, mem_ref, ga_ref, gm_ref,
                 wa_ref, wm_ref, wo_ref, gf_ref, wr_ref, br_ref, h1_ref, xn_ref, idx_ref, gate_ref):
    tm = x_ref.shape[0]
    ao_refs = (ao0_ref, ao1_ref, ao2_ref)
    lses = (lse0_ref[...], lse1_ref[...], lse2_ref[...])
    heads = []
    for h in range(HEADS_PER_GROUP):
        ls = [l[:, h:h + 1] for l in lses]
        mx = jnp.maximum(jnp.maximum(ls[0], ls[1]), ls[2])
        es = [jnp.exp(l - mx) for l in ls]
        tot = es[0] + es[1] + es[2]
        acc = None
        for g in range(N_GROUPS):
            term = (es[g] / tot) * ao_refs[g][:, h * HEAD_DIM:(h + 1) * HEAD_DIM].astype(F32)
            acc = term if acc is None else acc + term
        heads.append(acc.astype(BF16))
    attn = jnp.concatenate(heads, axis=1)
    ya = jnp.dot(attn, wa_ref[...], preferred_element_type=F32)
    ym = jnp.dot(mem_ref[...], wm_ref[...], preferred_element_type=F32)
    merged = (jax.nn.sigmoid(ga_ref[...].astype(F32)) * ya + jax.nn.sigmoid(gm_ref[...].astype(F32)) * ym)
    h1 = x_ref[...] + jnp.dot(merged.astype(BF16), wo_ref[...], preferred_element_type=F32)
    h1_ref[...] = h1
    ms = jnp.mean(h1 * h1, axis=-1, keepdims=True)
    xn = h1 * lax.rsqrt(ms + RMS_EPS) * gf_ref[...]
    for s in range(SUBLANES):
        xn_ref[pl.ds(s, tm, stride=SUBLANES), :] = xn[:, s * LANES:(s + 1) * LANES]
    logits = jnp.dot(xn, wr_ref[...], preferred_element_type=F32, precision=lax.Precision.HIGHEST) + br_ref[...]
    lane = lax.broadcasted_iota(jnp.int32, (tm, LANES), 1).astype(F32)
    vals = logits
    tops, idxs = [], []
    for _ in range(TOP_K):
        mx = jnp.max(vals, axis=1, keepdims=True)
        ix = jnp.min(jnp.where(vals == mx, lane, float(LANES)), axis=1, keepdims=True)
        tops.append(mx)
        idxs.append(ix)
        vals = jnp.where(lane == ix, NEG, vals)
    es = [jnp.exp(t - tops[0]) for t in tops]
    tot = es[0] + es[1] + es[2] + es[3]
    idx_out = jnp.zeros((tm, LANES), F32)
    gate_out = jnp.zeros((tm, LANES), F32)
    for kk in range(TOP_K):
        idx_out = jnp.where(lane == float(kk), idxs[kk], idx_out)
        gate_out = jnp.where(lane == float(kk), es[kk] / tot, gate_out)
    idx_ref[...] = idx_out.astype(jnp.int32)
    gate_ref[...] = gate_out


def _post(x2, aos, lses, mem, proj, w_a, w_m, w_o, g_ffn, w_r, b_r):
    T, D = x2.shape
    tm = POST_TM
    row = lambda w: pl.BlockSpec((tm, w), lambda i: (i, 0))
    full = lambda a: pl.BlockSpec(a.shape, lambda i: (0,) * a.ndim)
    return pl.pallas_call(
        _post_kernel,
        grid=(T // tm,),
        in_specs=[
            row(D), row(ATTN_OUT), row(ATTN_OUT), row(ATTN_OUT), row(LANES), row(LANES), row(LANES), row(M_WIDTH),
            pl.BlockSpec((tm, D), lambda i: (i, COL_GA // D)),
            pl.BlockSpec((tm, D), lambda i: (i, COL_GM // D)),
            full(w_a), full(w_m), full(w_o), full(g_ffn), full(w_r), full(b_r),
        ],
        out_specs=[row(D), pl.BlockSpec((tm * SUBLANES, LANES), lambda i: (i, 0)), row(LANES), row(LANES)],
        out_shape=[jax.ShapeDtypeStruct((T, D), F32), jax.ShapeDtypeStruct((T * SUBLANES, LANES), F32),
                   jax.ShapeDtypeStruct((T, LANES), jnp.int32), jax.ShapeDtypeStruct((T, LANES), F32)],
        compiler_params=_params(("parallel",)),
        name="post",
    )(x2, *aos, *lses, mem, proj, proj, w_a, w_m, w_o, g_ffn, w_r, b_r)


def _expert_kernel(be_ref, nv_ref, nu_ref, idx_hbm, xn_hbm, wgu_ref, wd_ref, bgu_ref, bd_ref, y_hbm,
                   idx_smem, xbuf, ybuf, xb_ref, wgu_bf, wdx_bf, wd_tmp, isem, gsem, ssem):
    i = pl.program_id(0)
    n_idx = idx_hbm.shape[0]
    n_used = nu_ref[0]
    BLK = MOE_BLOCK
    n_islots = idx_smem.shape[0]
    n_bufs = xbuf.shape[0]

    def tile(t):
        return pl.ds(pl.multiple_of(t * SUBLANES, SUBLANES), SUBLANES)

    def tiles(n):
        return pl.ds(0, pl.multiple_of(n * SUBLANES, SUBLANES))

    def idx_copy(blk):
        slot = blk % n_islots
        return pltpu.make_async_copy(idx_hbm.at[blk], idx_smem.at[slot], isem.at[slot])

    def gather_start(blk):
        islot, bslot = blk % n_islots, blk % n_bufs
        for r in range(BLK):
            tok = idx_smem[islot, r]
            pltpu.make_async_copy(xn_hbm.at[tile(tok)], xbuf.at[bslot, pl.ds(r * SUBLANES, SUBLANES)],
                                  gsem.at[bslot]).start()

    def gather_wait(blk):
        bslot = blk % n_bufs
        pltpu.make_async_copy(xn_hbm.at[tiles(BLK)], xbuf.at[bslot], gsem.at[bslot]).wait()

    def scatter_start(blk, n):
        islot, bslot = blk % n_islots, blk % n_bufs
        for r in range(BLK):
            @pl.when(r < n)
            def _():
                dst = idx_smem[islot, BLK + r]
                pltpu.make_async_copy(ybuf.at[bslot, pl.ds(r * SUBLANES, SUBLANES)], y_hbm.at[tile(dst)],
                                      ssem.at[bslot]).start()

    def scatter_wait(blk, n):
        bslot = blk % n_bufs

        @pl.when(n > 0)
        def _():
            pltpu.make_async_copy(ybuf.at[bslot, tiles(n)], y_hbm.at[tiles(n)], ssem.at[bslot]).wait()

    def n_rows(blk):
        return jnp.where(blk >= 0, nv_ref[jnp.maximum(blk, 0)], 0)

    @pl.when(i == 0)
    def _():
        idx_copy(0).start()
        idx_copy(1).start()
        idx_copy(2).start()
        idx_copy(0).wait()
        idx_copy(1).wait()
        gather_start(0)
        gather_start(1)
        wd_tmp[...] = jnp.zeros_like(wd_tmp)

    @pl.when(i + 3 < n_idx)
    def _():
        idx_copy(i + 3).start()

    @pl.when(i + 2 < n_idx)
    def _():
        idx_copy(i + 2).wait()

    @pl.when(jnp.logical_and(i < n_used, jnp.logical_or(i == 0, be_ref[i] != be_ref[jnp.maximum(i - 1, 0)])))
    def _():
        wgu_bf[...] = wgu_ref[0].astype(BF16)
        rows = wd_tmp.shape[1] // 2
        for c in range(wd_ref.shape[1] // rows):
            for s in range(wd_ref.shape[2] // LANES):
                wd_tmp.at[s][pl.ds(0, rows, stride=2), :] = wd_ref[0, c * rows:(c + 1) * rows,
                                                                   s * LANES:(s + 1) * LANES]
            for s in range(wd_ref.shape[2] // LANES):
                wdx_bf[2 * c * rows:2 * (c + 1) * rows, s * LANES:(s + 1) * LANES] = wd_tmp[s].astype(BF16)

    @pl.when(i < n_used)
    def _():
        gather_wait(i)
        scatter_wait(i - 3, n_rows(i - 3))

    @pl.when(i < n_used)
    def _():
        bslot = i % n_bufs
        for s in range(SUBLANES):
            xb_ref[:, s * LANES:(s + 1) * LANES] = xbuf[bslot, pl.ds(s, BLK, stride=SUBLANES), :].astype(BF16)
        gather_start(i + 2)
        scatter_start(i - 1, n_rows(i - 1))
        hg = jnp.dot(xb_ref[...], wgu_bf[...], preferred_element_type=F32) + bgu_ref[0]
        nxt = pltpu.roll(hg, hg.shape[1] - 1, 1)
        gate = jnp.minimum(hg, SWIGLU_LIMIT)
        up = jnp.clip(nxt, -SWIGLU_LIMIT, SWIGLU_LIMIT)
        act = (up + 1.0) * (gate * jax.nn.sigmoid(SWIGLU_ALPHA * gate))
        y = jnp.dot(act.astype(BF16), wdx_bf[...], preferred_element_type=F32) + bd_ref[0]
        for s in range(SUBLANES):
            ybuf[bslot, pl.ds(s, BLK, stride=SUBLANES), :] = y[:, s * LANES:(s + 1) * LANES]

    @pl.when(i == n_used)
    def _():
        gather_wait(i)
        gather_wait(i + 1)
        scatter_start(i - 1, n_rows(i - 1))
        scatter_wait(i - 3, n_rows(i - 3))
        scatter_wait(i - 2, n_rows(i - 2))
        scatter_wait(i - 1, n_rows(i - 1))


def _experts(block_e, n_valid, n_used, idx, xn2, wgu, wd, bgu, bd, n_out_rows):
    nb = block_e.shape[0]
    D = wd.shape[2]
    F2 = wgu.shape[2]
    emap = lambda i, be, nv, nu: (be[i], 0, 0)
    grid_spec = pltpu.PrefetchScalarGridSpec(
        num_scalar_prefetch=3,
        grid=(nb,),
        in_specs=[
            pl.BlockSpec(memory_space=pl.ANY),
            pl.BlockSpec(memory_space=pl.ANY),
            pl.BlockSpec((1, D, F2), emap),
            pl.BlockSpec((1, F2 // 2, D), emap),
            pl.BlockSpec((1, 1, F2), emap),
            pl.BlockSpec((1, 1, D), emap),
        ],
        out_specs=pl.BlockSpec(memory_space=pl.ANY),
        scratch_shapes=[
            pltpu.SMEM((8, 2 * MOE_BLOCK), jnp.int32),
            pltpu.VMEM((3, MOE_BLOCK * SUBLANES, LANES), F32),
            pltpu.VMEM((3, MOE_BLOCK * SUBLANES, LANES), F32),
            pltpu.VMEM((MOE_BLOCK, D), BF16),
            pltpu.VMEM((D, F2), BF16),
            pltpu.VMEM((F2, D), BF16),
            pltpu.VMEM((D // LANES, F2 // 4, LANES), F32),
            pltpu.SemaphoreType.DMA((8,)),
            pltpu.SemaphoreType.DMA((3,)),
            pltpu.SemaphoreType.DMA((3,)),
        ],
    )
    return pl.pallas_call(
        _expert_kernel,
        grid_spec=grid_spec,
        out_shape=jax.ShapeDtypeStruct((n_out_rows * SUBLANES, LANES), F32),
        compiler_params=_params(("arbitrary",)),
        name="experts",
    )(block_e, n_valid, n_used, idx, xn2, wgu, wd, bgu, bd)


def _final_kernel(h1_ref, y0_ref, y1_ref, y2_ref, y3_ref, gate_ref, g_ref, o_ref):
    tm = h1_ref.shape[0]
    gts = gate_ref[...]
    g4 = [gts[:, kk:kk + 1] for kk in range(TOP_K)]
    pieces = []
    ss = None
    for s in range(SUBLANES):
        h = h1_ref[:, s * LANES:(s + 1) * LANES]
        for kk, y_ref in enumerate((y0_ref, y1_ref, y2_ref, y3_ref)):
            h = h + g4[kk] * y_ref[pl.ds(s, tm, stride=SUBLANES), :]
        pieces.append(h)
        sq = jnp.sum(h * h, axis=-1, keepdims=True)
        ss = sq if ss is None else ss + sq
    inv = lax.rsqrt(ss / h1_ref.shape[1] + RMS_EPS)
    for s in range(SUBLANES):
        o_ref[:, s * LANES:(s + 1) * LANES] = pieces[s] * inv * g_ref[:, s * LANES:(s + 1) * LANES]


def _final(h1, y4, gates, g_final):
    T, D = h1.shape
    tm = FINAL_TM
    nt = T // tm
    yspec = lambda kk: pl.BlockSpec((tm * SUBLANES, LANES), lambda i: (kk * nt + i, 0))
    return pl.pallas_call(
        _final_kernel,
        grid=(nt,),
        in_specs=[pl.BlockSpec((tm, D), lambda i: (i, 0)), yspec(0), yspec(1), yspec(2), yspec(3),
                  pl.BlockSpec((tm, LANES), lambda i: (i, 0)), pl.BlockSpec((1, D), lambda i: (0, 0))],
        out_specs=pl.BlockSpec((tm, D), lambda i: (i, 0)),
        out_shape=jax.ShapeDtypeStruct((T, D), F32),
        compiler_params=_params(("parallel",)),
        name="final",
    )(h1, y4, y4, y4, y4, gates, g_final)


def _rope_tables(positions):
    half = ROPE_DIM // 2
    inv_freq = ROPE_THETA ** (-jnp.arange(half, dtype=F32) / half)
    ang = positions.astype(F32).reshape(-1)[:, None] * inv_freq
    cos, sin = jnp.cos(ang), jnp.sin(ang)
    T = ang.shape[0]
    cos_t = jnp.concatenate([cos, cos, jnp.ones((T, HEAD_DIM - ROPE_DIM), F32)], axis=1)
    sa_t = jnp.concatenate([-sin, jnp.zeros((T, HEAD_DIM - half), F32)], axis=1)
    sb_t = jnp.concatenate([jnp.zeros((T, half), F32), sin, jnp.zeros((T, HEAD_DIM - ROPE_DIM), F32)], axis=1)
    return cos_t, sa_t, sb_t


def _routing(top_idx, T):
    A = T * TOP_K
    BLK = MOE_BLOCK
    n_blocks = A // BLK + N_EXPERTS
    n_rows = n_blocks * BLK
    flat_e = top_idx.reshape(A)
    onehot = (flat_e[:, None] == jnp.arange(N_EXPERTS, dtype=jnp.int32)[None, :]).astype(jnp.int32)
    csum = jnp.cumsum(onehot, axis=0)
    rank = jnp.take_along_axis(csum, flat_e[:, None], axis=1)[:, 0] - 1
    counts = csum[-1]
    padded = (counts + BLK - 1) // BLK * BLK
    padded_ends = jnp.cumsum(padded)
    padded_starts = padded_ends - padded
    dest = padded_starts[flat_e] + rank
    row_a = jnp.zeros((n_rows,), jnp.int32).at[dest].set(jnp.arange(A, dtype=jnp.int32), unique_indices=True)
    row_tok = row_a // TOP_K
    row_slot = (row_a % TOP_K) * T + row_tok
    block_start = jnp.arange(n_blocks, dtype=jnp.int32) * BLK
    block_e = jnp.minimum(jnp.sum((padded_ends[None, :] <= block_start[:, None]).astype(jnp.int32), axis=1),
                          N_EXPERTS - 1)
    n_valid = jnp.clip(padded_starts[block_e] + counts[block_e] - block_start, 0, BLK).astype(jnp.int32)
    n_used = (padded_ends[-1] // BLK).astype(jnp.int32).reshape(1)
    n_valid = jnp.where(block_start < padded_ends[-1], n_valid, 0)
    idx = jnp.concatenate([row_tok.reshape(n_blocks, BLK), row_slot.reshape(n_blocks, BLK)], axis=1)
    block_e = jnp.concatenate([block_e, block_e[-1:]])
    n_valid = jnp.concatenate([n_valid, jnp.zeros((1,), jnp.int32)])
    idx = jnp.concatenate([idx, jnp.zeros((2, 2 * BLK), jnp.int32)], axis=0)
    return block_e, n_valid, n_used, idx, A


def kernel(x, positions, g_mix, w_in, conv_qk, b_if, g_mlstm_norm, w_attn_out, w_mlstm_out, w_mix_out, g_ffn,
           w_router, b_router, w_gate_up, b_gate_up, w_down, b_down, g_final):
    B, S, D = x.shape
    T = B * S
    l = 0
    x2 = x.reshape(T, D)

    w = w_in[l]
    o = np.cumsum((0, ATTN_WIDTH, ATTN_WIDTH, ATTN_WIDTH, M_WIDTH, M_WIDTH, M_WIDTH, M_WIDTH, M_HEADS, M_HEADS,
                   D_MODEL, D_MODEL))
    seg = lambda a: w[:, int(o[a]):int(o[a + 1])]
    w_main = jnp.concatenate([seg(0), seg(1), seg(9), seg(10), seg(2), seg(3), seg(4), seg(5), seg(6)],
                             axis=1).astype(BF16)
    w_if = jnp.pad(w[:, int(o[7]):int(o[9])], ((0, 0), (0, LANES - 2 * M_HEADS))).astype(BF16)
    cos_t, sa_t, sb_t = _rope_tables(positions)

    proj, ifo = _proj(x2, g_mix[l][None, :], w_main, w_if, cos_t, sa_t, sb_t)

    proj3 = proj.reshape(B, S, N_MAIN)
    attn_out = [_attention(proj3, g) for g in range(N_GROUPS)]
    aos = [o.reshape(T, ATTN_OUT) for o, _ in attn_out]
    lses = [l.reshape(T, LANES) for _, l in attn_out]

    nC = S // MLSTM_CHUNK
    gates = ifo[:, :2 * M_HEADS].reshape(B, S, 2 * M_HEADS).transpose(0, 2, 1).reshape(B, 2 * M_HEADS, nC, MLSTM_CHUNK)
    mem = _mlstm(proj3, gates, conv_qk[l][:, :M_WIDTH], conv_qk[l][:, M_WIDTH:], b_if[l],
                 g_mlstm_norm[l][None, :])

    w_r = jnp.pad(w_router[l], ((0, 0), (0, LANES - N_EXPERTS)))
    b_r = jnp.concatenate([b_router[l], jnp.full((LANES - N_EXPERTS,), NEG, F32)])[None, :]
    h1, xn2, top_idx, gates4 = _post(x2, aos, lses, mem.reshape(T, M_WIDTH), proj, w_attn_out[l].astype(BF16),
                                     w_mlstm_out[l].astype(BF16), w_mix_out[l].astype(BF16), g_ffn[l][None, :],
                                     w_r, b_r)

    block_e, n_valid, n_used, idx, n_out_rows = _routing(top_idx[:, :TOP_K], T)
    y4 = _experts(block_e, n_valid, n_used, idx, xn2, w_gate_up[l], w_down[l], b_gate_up[l][:, None, :],
                  b_down[l][:, None, :], n_out_rows)

    out = _final(h1, y4, gates4, g_final[None, :])
    return out.reshape(B, S, D)
```

```python
import functools

import jax
import jax.numpy as jnp
import numpy as np
from jax import lax
from jax.experimental import pallas as pl
from jax.experimental.pallas import tpu as pltpu

F32 = jnp.float32
BF16 = jnp.bfloat16

D_MODEL = 1024
N_GROUPS = 3
GROUP_DILATION = (1, 4, 16)
HEADS_PER_GROUP = 4
HEAD_DIM = 128
ATTN_WIDTH = N_GROUPS * HEADS_PER_GROUP * HEAD_DIM
ATTN_OUT = HEADS_PER_GROUP * HEAD_DIM
ATTN_BLOCK = 128
ROPE_THETA = 500000.0
ROPE_DIM = HEAD_DIM // 4
M_HEADS = 4
M_WIDTH = D_MODEL
M_HEAD_DIM = M_WIDTH // M_HEADS
CONV_WIDTH = 4
N_EXPERTS = 32
TOP_K = 4
D_FF = D_MODEL
SWIGLU_LIMIT = 7.0
SWIGLU_ALPHA = 1.702
RMS_EPS = 1e-5
LN_EPS = 1e-5

LANES = 128
SUBLANES = 8
VMEM_LIMIT = 56 * 1024 * 1024

PROJ_TM = 1024
PROJ_TN = 1536
MLSTM_CHUNK = 128
MLSTM_SEGMENT = 1024
POST_TM = 512
MOE_BLOCK = 256
FINAL_TM = 512
NEG = -1e30

COL_AQ = 0
COL_AK = ATTN_WIDTH
COL_GA = 2 * ATTN_WIDTH
COL_GM = COL_GA + D_MODEL
COL_MQ = COL_GM + D_MODEL
COL_MK = COL_MQ + M_WIDTH
COL_MV = COL_MK + M_WIDTH
COL_MO = COL_MV + M_WIDTH
COL_AV = COL_MO + M_WIDTH
N_MAIN = COL_AV + ATTN_WIDTH


def _params(sem):
    return pltpu.CompilerParams(dimension_semantics=sem, vmem_limit_bytes=VMEM_LIMIT)


def _proj_kernel(x_ref, g_ref, w_ref, wif_ref, cos_ref, sa_ref, sb_ref, o_ref, if_ref, xn_ref, *, rope_tiles):
    j = pl.program_id(1)

    @pl.when(j == 0)
    def _():
        x = x_ref[...]
        ms = jnp.mean(x * x, axis=-1, keepdims=True)
        xn = (x * lax.rsqrt(ms + RMS_EPS) * g_ref[...]).astype(BF16)
        xn_ref[...] = xn
        if_ref[...] = jnp.dot(xn, wif_ref[...], preferred_element_type=F32)

    acc = jnp.dot(xn_ref[...], w_ref[...], preferred_element_type=F32)

    @pl.when(j < rope_tiles)
    def _():
        c = cos_ref[...]
        sa = sa_ref[...]
        sb = sb_ref[...]
        for h in range(acc.shape[1] // HEAD_DIM):
            a = acc[:, h * HEAD_DIM:(h + 1) * HEAD_DIM]
            y = a * c + pltpu.roll(a, HEAD_DIM - ROPE_DIM // 2, 1) * sa + pltpu.roll(a, ROPE_DIM // 2, 1) * sb
            o_ref[:, h * HEAD_DIM:(h + 1) * HEAD_DIM] = y.astype(o_ref.dtype)

    @pl.when(j >= rope_tiles)
    def _():
        o_ref[...] = acc.astype(o_ref.dtype)


def _proj(x2, g_mix, w_main, w_if, cos_t, sa_t, sb_t):
    T, D = x2.shape
    N = w_main.shape[1]
    tm, tn = PROJ_TM, PROJ_TN
    grid = (T // tm, N // tn)
    return pl.pallas_call(
        functools.partial(_proj_kernel, rope_tiles=(2 * ATTN_WIDTH) // tn),
        grid=grid,
        in_specs=[
            pl.BlockSpec((tm, D), lambda i, j: (i, 0)),
            pl.BlockSpec((1, D), lambda i, j: (0, 0)),
            pl.BlockSpec((D, tn), lambda i, j: (0, j)),
            pl.BlockSpec((D, LANES), lambda i, j: (0, 0)),
            pl.BlockSpec((tm, LANES), lambda i, j: (i, 0)),
            pl.BlockSpec((tm, LANES), lambda i, j: (i, 0)),
            pl.BlockSpec((tm, LANES), lambda i, j: (i, 0)),
        ],
        out_specs=[
            pl.BlockSpec((tm, tn), lambda i, j: (i, j)),
            pl.BlockSpec((tm, LANES), lambda i, j: (i, 0)),
        ],
        out_shape=[jax.ShapeDtypeStruct((T, N), BF16), jax.ShapeDtypeStruct((T, LANES), F32)],
        scratch_shapes=[pltpu.VMEM((tm, D), BF16)],
        compiler_params=_params(("parallel", "arbitrary")),
        name="proj",
    )(x2, g_mix, w_main, w_if, cos_t, sa_t, sb_t)


def _attn_kernel(q_ref, k_ref, v_ref, o_ref, lse_ref, stage_ref, q_rm, k_rm, v_rm, o_rm, *, dilation):
    S = q_ref.shape[1]
    d = dilation
    L = S // d
    per_class = L // ATTN_BLOCK
    nblk = S // ATTN_BLOCK
    h = pl.program_id(1)

    if d == 1:
        q_of = lambda sl: q_ref[0, sl, :]
        k_of = lambda sl: k_ref[0, sl, :]
        v_of = lambda sl: v_ref[0, sl, :]
    else:
        for src, dst in ((q_ref, q_rm), (k_ref, k_rm), (v_ref, v_rm)):
            stage_ref[...] = src[0].astype(F32)
            if d <= SUBLANES:
                for r in range(d):
                    dst[r * L:(r + 1) * L, :] = stage_ref[pl.ds(r, L, stride=d), :].astype(BF16)
            else:
                d0 = int(round(d ** 0.5))
                assert d0 * d0 == d and d0 <= SUBLANES
                for a in range(d0):
                    o_rm[a * (S // d0):(a + 1) * (S // d0), :] = stage_ref[pl.ds(a, S // d0, stride=d0), :]
                for a in range(d0):
                    for b in range(d0):
                        r = a + d0 * b
                        dst[r * L:(r + 1) * L, :] = o_rm[pl.ds(a * (S // d0) + b, L, stride=d0), :].astype(BF16)
        q_of = lambda sl: q_rm[sl, :]
        k_of = lambda sl: k_rm[sl, :]
        v_of = lambda sl: v_rm[sl, :]

    @pl.when(h == 0)
    def _():
        lse_ref[...] = jnp.zeros_like(lse_ref)

    row = lax.broadcasted_iota(jnp.int32, (ATTN_BLOCK, ATTN_BLOCK), 0)
    col = lax.broadcasted_iota(jnp.int32, (ATTN_BLOCK, ATTN_BLOCK), 1)
    bias_cur = jnp.where(col <= row, 0.0, NEG).astype(F32)
    bias_prev = jnp.where(col >= row, 0.0, NEG).astype(F32)
    my_lane = col == h
    scale = HEAD_DIM ** -0.5
    dn = (((1,), (1,)), ((), ()))
    for n in range(nblk):
        sl = slice(n * ATTN_BLOCK, (n + 1) * ATTN_BLOCK)
        has_prev = n % per_class != 0
        q = q_of(sl)
        s_c = lax.dot_general(q, k_of(sl), dn, preferred_element_type=F32) * scale + bias_cur
        m = jnp.max(s_c, axis=1, keepdims=True)
        if has_prev:
            sp = slice((n - 1) * ATTN_BLOCK, n * ATTN_BLOCK)
            s_p = lax.dot_general(q, k_of(sp), dn, preferred_element_type=F32) * scale + bias_prev
            m = jnp.maximum(m, jnp.max(s_p, axis=1, keepdims=True))
        p_c = jnp.exp(s_c - m)
        den = jnp.sum(p_c, axis=1, keepdims=True)
        acc = jnp.dot(p_c.astype(BF16), v_of(sl), preferred_element_type=F32)
        if has_prev:
            p_p = jnp.exp(s_p - m)
            den = den + jnp.sum(p_p, axis=1, keepdims=True)
            acc = acc + jnp.dot(p_p.astype(BF16), v_of(sp), preferred_element_type=F32)
        lse = m + jnp.log(den)
        start = n // per_class + d * (n % per_class) * ATTN_BLOCK
        rows = pl.ds(start, ATTN_BLOCK, stride=d) if d > 1 else sl
        o_rm[rows, :] = acc / den
        lse_ref[0, rows, :] = jnp.where(my_lane, lse, lse_ref[0, rows, :])
    o_ref[0] = o_rm[...].astype(o_ref.dtype)


def _attention(proj3, g):
    B, S, _ = proj3.shape
    Dh = HEAD_DIM

    def col(base):
        return pl.BlockSpec((1, S, Dh), lambda b, h: (b, 0, (base + g * ATTN_OUT) // Dh + h))

    return pl.pallas_call(
        functools.partial(_attn_kernel, dilation=GROUP_DILATION[g]),
        grid=(B, HEADS_PER_GROUP),
        in_specs=[col(COL_AQ), col(COL_AK), col(COL_AV)],
        out_specs=[pl.BlockSpec((1, S, Dh), lambda b, h: (b, 0, h)),
                   pl.BlockSpec((1, S, LANES), lambda b, h: (b, 0, 0))],
        out_shape=[jax.ShapeDtypeStruct((B, S, ATTN_OUT), BF16), jax.ShapeDtypeStruct((B, S, LANES), F32)],
        scratch_shapes=[pltpu.VMEM((S, Dh), F32), pltpu.VMEM((S, Dh), BF16), pltpu.VMEM((S, Dh), BF16),
                        pltpu.VMEM((S, Dh), BF16), pltpu.VMEM((S, Dh), F32)],
        compiler_params=_params(("parallel", "arbitrary")),
        name=f"attn{g}",
    )(proj3, proj3, proj3)


def _mlstm_kernel(bif_ref, mq_ref, mk_ref, mv_ref, mo_ref, g_ref, cwq_ref, cwk_ref, gn_ref, o_ref,
                  pad_ref, qs_ref, ks_ref, c_ref, n_ref, m_ref, hq_ref, hk_ref):
    SEG = mq_ref.shape[1]
    LC = MLSTM_CHUNK
    Dh = M_HEAD_DIM
    H = M_HEADS
    seg = pl.program_id(1)

    @pl.when(seg == 0)
    def _():
        c_ref[...] = jnp.zeros_like(c_ref)
        n_ref[...] = jnp.zeros_like(n_ref)
        m_ref[...] = jnp.zeros_like(m_ref)
        hq_ref[...] = jnp.zeros_like(hq_ref)
        hk_ref[...] = jnp.zeros_like(hk_ref)

    def conv_silu(src_ref, hist_ref, w_ref, cols, dst_ref, scale):
        pad_ref[0:SUBLANES, :] = hist_ref[:, cols]
        pad_ref[SUBLANES:SUBLANES + SEG, :] = src_ref[0, :, cols].astype(F32)
        hist_ref[:, cols] = pad_ref[SEG:SEG + SUBLANES, :]
        rb = 2 * LC
        for r0 in range(0, SEG, rb):
            acc = None
            for j in range(CONV_WIDTH):
                off = r0 + SUBLANES - (CONV_WIDTH - 1) + j
                term = w_ref[j:j + 1, cols] * pad_ref[off:off + rb, :]
                acc = term if acc is None else acc + term
            dst_ref[r0:r0 + rb, cols] = (acc * jax.nn.sigmoid(acc) * scale).astype(dst_ref.dtype)

    for hd in range(H):
        cols = slice(hd * Dh, (hd + 1) * Dh)
        conv_silu(mq_ref, hq_ref, cwq_ref, cols, qs_ref, 1.0)
        conv_silu(mk_ref, hk_ref, cwk_ref, cols, ks_ref, Dh ** -0.5)

    row = lax.broadcasted_iota(jnp.int32, (LC, LC), 0)
    col = lax.broadcasted_iota(jnp.int32, (LC, LC), 1)
    causal = col <= row
    eye = col == row
    lane8 = lax.broadcasted_iota(jnp.int32, (SUBLANES, LC), 1)

    def head_chunk(hd, c, r0, m):
        cols = slice(hd * Dh, (hd + 1) * Dh)
        c_ref_h = c_ref.at[hd]
        gn = gn_ref[:, cols]
        q = qs_ref[pl.ds(r0, LC), cols]
        k = ks_ref[pl.ds(r0, LC), cols]
        v = mv_ref[0, pl.ds(r0, LC), cols]
        i_row = g_ref[0, hd, pl.ds(c, 1), :] + bif_ref[hd]
        f_row = g_ref[0, H + hd, pl.ds(c, 1), :] + bif_ref[H + hd]
        logf = jnp.minimum(f_row, 0.0) - jnp.log(1.0 + jnp.exp(-jnp.abs(f_row)))
        b8 = jnp.broadcast_to(logf, (SUBLANES, LC))
        s = 1
        while s < LC:
            b8 = b8 + jnp.where(lane8 >= s, pltpu.roll(b8, s, 1), 0.0)
            s *= 2
        b_row = b8[0:1, :]
        d_row = i_row - b_row
        b_col = jnp.sum(jnp.where(eye, b_row, 0.0), axis=1, keepdims=True)
        d_col = jnp.sum(jnp.where(eye, d_row, 0.0), axis=1, keepdims=True)
        dm = jnp.where(causal, b_col + d_row, NEG)
        inter = b_col + m
        m_t = jnp.maximum(inter, jnp.max(dm, axis=1, keepdims=True))
        w_intra = jnp.exp(dm - m_t)
        w_inter = jnp.exp(inter - m_t)
        kb = k.astype(BF16)
        a = lax.dot_general(q, kb, (((1,), (1,)), ((), ())), preferred_element_type=F32) * w_intra
        num = (jnp.dot(a.astype(BF16), v, preferred_element_type=F32)
               + w_inter * jnp.dot(q, c_ref_h[...].astype(BF16), preferred_element_type=F32))
        den = (jnp.sum(a, axis=1, keepdims=True)
               + w_inter * jnp.sum(q.astype(F32) * n_ref[:, cols], axis=1, keepdims=True))
        hh = num / jnp.maximum(jnp.abs(den), jnp.exp(-m_t))
        mu = jnp.mean(hh, axis=1, keepdims=True)
        xc = hh - mu
        var = jnp.mean(xc * xc, axis=1, keepdims=True)
        hn = xc * lax.rsqrt(var + LN_EPS) * gn
        gate = jax.nn.sigmoid(mo_ref[0, pl.ds(r0, LC), cols].astype(F32))
        o_ref[0, pl.ds(r0, LC), cols] = (gate * hn).astype(o_ref.dtype)
        b_last = b_row[:, LC - 1:LC]
        m_new = jnp.maximum(b_last + m, jnp.max(b_last + d_row, axis=1, keepdims=True))
        decay = jnp.exp(b_last + m - m_new)
        kw = k * jnp.exp(b_last + d_col - m_new)
        c_ref_h[...] = decay * c_ref_h[...] + lax.dot_general(
            kw.astype(BF16), v, (((0,), (0,)), ((), ())), preferred_element_type=F32)
        n_ref[:, cols] = decay * n_ref[:, cols] + jnp.sum(kw, axis=0, keepdims=True)
        return m_new

    def chunk(c, ms):
        r0 = pl.multiple_of(c * LC, LC)
        return tuple(head_chunk(hd, c, r0, ms[hd]) for hd in range(H))

    ms = lax.fori_loop(0, SEG // LC, chunk, tuple(m_ref[:, hd:hd + 1] for hd in range(H)))
    for hd in range(H):
        m_ref[:, hd:hd + 1] = ms[hd]


def _mlstm(proj3, gates, conv_q, conv_k, b_if, g_norm):
    B, S, _ = proj3.shape
    W = M_WIDTH
    SEG = MLSTM_SEGMENT
    nseg = S // SEG
    nC = SEG // MLSTM_CHUNK

    def col(base):
        return pl.BlockSpec((1, SEG, W), lambda b, s, pre: (b, s, base // W))

    full = lambda a: pl.BlockSpec(a.shape, lambda b, s, pre: (0,) * a.ndim)
    grid_spec = pltpu.PrefetchScalarGridSpec(
        num_scalar_prefetch=1,
        grid=(B, nseg),
        in_specs=[
            col(COL_MQ), col(COL_MK), col(COL_MV), col(COL_MO),
            pl.BlockSpec((1, 2 * M_HEADS, nC, MLSTM_CHUNK), lambda b, s, pre: (b, 0, s, 0)),
            full(conv_q), full(conv_k), full(g_norm),
        ],
        out_specs=pl.BlockSpec((1, SEG, W), lambda b, s, pre: (b, s, 0)),
        scratch_shapes=[
            pltpu.VMEM((SEG + SUBLANES, M_HEAD_DIM), F32),
            pltpu.VMEM((SEG, W), BF16),
            pltpu.VMEM((SEG, W), F32),
            pltpu.VMEM((M_HEADS, M_HEAD_DIM, M_HEAD_DIM), F32),
            pltpu.VMEM((1, W), F32),
            pltpu.VMEM((1, LANES), F32),
            pltpu.VMEM((SUBLANES, W), F32),
            pltpu.VMEM((SUBLANES, W), F32),
        ],
    )
    return pl.pallas_call(
        _mlstm_kernel,
        grid_spec=grid_spec,
        out_shape=jax.ShapeDtypeStruct((B, S, W), BF16),
        compiler_params=_params(("parallel", "arbitrary")),
        name="mlstm",
    )(b_if, proj3, proj3, proj3, proj3, gates, conv_q, conv_k, g_norm)


def _post_kernel(x_ref, ao0_ref, ao1_ref, ao2_ref, lse0_ref, lse1_ref, lse2_ref, mem_ref, ga_ref, gm_ref,
                 wa_ref, wm_ref, wo_ref, gf_ref, wr_ref, br_ref, h1_ref, xn_ref, idx_ref, gate_ref):
    tm = x_ref.shape[0]
    ao_refs = (ao0_ref, ao1_ref, ao2_ref)
    lses = (lse0_ref[...], lse1_ref[...], lse2_ref[...])
    heads = []
    for h in range(HEADS_PER_GROUP):
        ls = [l[:, h:h + 1] for l in lses]
        mx = jnp.maximum(jnp.maximum(ls[0], ls[1]), ls[2])
        es = [jnp.exp(l - mx) for l in ls]
        tot = es[0] + es[1] + es[2]
        acc = None
        for g in range(N_GROUPS):
            term = (es[g] / tot) * ao_refs[g][:, h * HEAD_DIM:(h + 1) * HEAD_DIM].astype(F32)
            acc = term if acc is None else acc + term
        heads.append(acc.astype(BF16))
    attn = jnp.concatenate(heads, axis=1)
    ya = jnp.dot(attn, wa_ref[...], preferred_element_type=F32)
    ym = jnp.dot(mem_ref[...], wm_ref[...], preferred_element_type=F32)
    merged = (jax.nn.sigmoid(ga_ref[...].astype(F32)) * ya + jax.nn.sigmoid(gm_ref[...].astype(F32)) * ym)
    h1 = x_ref[...] + jnp.dot(merged.astype(BF16), wo_ref[...], preferred_element_type=F32)
    h1_ref[...] = h1
    ms = jnp.mean(h1 * h1, axis=-1, keepdims=True)
    xn = h1 * lax.rsqrt(ms + RMS_EPS) * gf_ref[...]
    for s in range(SUBLANES):
        xn_ref[pl.ds(s, tm, stride=SUBLANES), :] = xn[:, s * LANES:(s + 1) * LANES]
    logits = jnp.dot(xn, wr_ref[...], preferred_element_type=F32, precision=lax.Precision.HIGHEST) + br_ref[...]
    lane = lax.broadcasted_iota(jnp.int32, (tm, LANES), 1).astype(F32)
    vals = logits
    tops, idxs = [], []
    for _ in range(TOP_K):
        mx = jnp.max(vals, axis=1, keepdims=True)
        ix = jnp.min(jnp.where(vals == mx, lane, float(LANES)), axis=1, keepdims=True)
        tops.append(mx)
        idxs.append(ix)
        vals = jnp.where(lane == ix, NEG, vals)
    es = [jnp.exp(t - tops[0]) for t in tops]
    tot = es[0] + es[1] + es[2] + es[3]
    idx_out = jnp.zeros((tm, LANES), F32)
    gate_out = jnp.zeros((tm, LANES), F32)
    for kk in range(TOP_K):
        idx_out = jnp.where(lane == float(kk), idxs[kk], idx_out)
        gate_out = jnp.where(lane == float(kk), es[kk] / tot, gate_out)
    idx_ref[...] = idx_out.astype(jnp.int32)
    gate_ref[...] = gate_out


def _post(x2, aos, lses, mem, proj, w_a, w_m, w_o, g_ffn, w_r, b_r):
    T, D = x2.shape
    tm = POST_TM
    row = lambda w: pl.BlockSpec((tm, w), lambda i: (i, 0))
    full = lambda a: pl.BlockSpec(a.shape, lambda i: (0,) * a.ndim)
    return pl.pallas_call(
        _post_kernel,
        grid=(T // tm,),
        in_specs=[
            row(D), row(ATTN_OUT), row(ATTN_OUT), row(ATTN_OUT), row(LANES), row(LANES), row(LANES), row(M_WIDTH),
            pl.BlockSpec((tm, D), lambda i: (i, COL_GA // D)),
            pl.BlockSpec((tm, D), lambda i: (i, COL_GM // D)),
            full(w_a), full(w_m), full(w_o), full(g_ffn), full(w_r), full(b_r),
        ],
        out_specs=[row(D), pl.BlockSpec((tm * SUBLANES, LANES), lambda i: (i, 0)), row(LANES), row(LANES)],
        out_shape=[jax.ShapeDtypeStruct((T, D), F32), jax.ShapeDtypeStruct((T * SUBLANES, LANES), F32),
                   jax.ShapeDtypeStruct((T, LANES), jnp.int32), jax.ShapeDtypeStruct((T, LANES), F32)],
        compiler_params=_params(("parallel",)),
        name="post",
    )(x2, *aos, *lses, mem, proj, proj, w_a, w_m, w_o, g_ffn, w_r, b_r)


def _expert_kernel(be_ref, nv_ref, nu_ref, idx_hbm, xn_hbm, wgu_ref, wd_ref, bgu_ref, bd_ref, y_hbm,
                   idx_smem, xbuf, ybuf, xb_ref, wgu_bf, wdx_bf, wd_tmp, isem, gsem, ssem):
    i = pl.program_id(0)
    n_idx = idx_hbm.shape[0]
    n_used = nu_ref[0]
    BLK = MOE_BLOCK
    n_islots = idx_smem.shape[0]
    n_bufs = xbuf.shape[0]

    def tile(t):
        return pl.ds(pl.multiple_of(t * SUBLANES, SUBLANES), SUBLANES)

    def tiles(n):
        return pl.ds(0, pl.multiple_of(n * SUBLANES, SUBLANES))

    def idx_copy(blk):
        slot = blk % n_islots
        return pltpu.make_async_copy(idx_hbm.at[blk], idx_smem.at[slot], isem.at[slot])

    def gather_start(blk):
        islot, bslot = blk % n_islots, blk % n_bufs
        for r in range(BLK):
            tok = idx_smem[islot, r]
            pltpu.make_async_copy(xn_hbm.at[tile(tok)], xbuf.at[bslot, pl.ds(r * SUBLANES, SUBLANES)],
                                  gsem.at[bslot]).start()

    def gather_wait(blk):
        bslot = blk % n_bufs
        pltpu.make_async_copy(xn_hbm.at[tiles(BLK)], xbuf.at[bslot], gsem.at[bslot]).wait()

    def scatter_start(blk, n):
        islot, bslot = blk % n_islots, blk % n_bufs
        for r in range(BLK):
            @pl.when(r < n)
            def _():
                dst = idx_smem[islot, BLK + r]
                pltpu.make_async_copy(ybuf.at[bslot, pl.ds(r * SUBLANES, SUBLANES)], y_hbm.at[tile(dst)],
                                      ssem.at[bslot]).start()

    def scatter_wait(blk, n):
        bslot = blk % n_bufs

        @pl.when(n > 0)
        def _():
            pltpu.make_async_copy(ybuf.at[bslot, tiles(n)], y_hbm.at[tiles(n)], ssem.at[bslot]).wait()

    def n_rows(blk):
        return jnp.where(blk >= 0, nv_ref[jnp.maximum(blk, 0)], 0)

    @pl.when(i == 0)
    def _():
        idx_copy(0).start()
        idx_copy(1).start()
        idx_copy(2).start()
        idx_copy(0).wait()
        idx_copy(1).wait()
        gather_start(0)
        gather_start(1)
        wd_tmp[...] = jnp.zeros_like(wd_tmp)

    @pl.when(i + 3 < n_idx)
    def _():
        idx_copy(i + 3).start()

    @pl.when(i + 2 < n_idx)
    def _():
        idx_copy(i + 2).wait()

    @pl.when(jnp.logical_and(i < n_used, jnp.logical_or(i == 0, be_ref[i] != be_ref[jnp.maximum(i - 1, 0)])))
    def _():
        wgu_bf[...] = wgu_ref[0].astype(BF16)
        rows = wd_tmp.shape[1] // 2
        for c in range(wd_ref.shape[1] // rows):
            for s in range(wd_ref.shape[2] // LANES):
                wd_tmp.at[s][pl.ds(0, rows, stride=2), :] = wd_ref[0, c * rows:(c + 1) * rows,
                                                                   s * LANES:(s + 1) * LANES]
            for s in range(wd_ref.shape[2] // LANES):
                wdx_bf[2 * c * rows:2 * (c + 1) * rows, s * LANES:(s + 1) * LANES] = wd_tmp[s].astype(BF16)

    @pl.when(i < n_used)
    def _():
        gather_wait(i)
        scatter_wait(i - 3, n_rows(i - 3))

    @pl.when(i < n_used)
    def _():
        bslot = i % n_bufs
        for s in range(SUBLANES):
            xb_ref[:, s * LANES:(s + 1) * LANES] = xbuf[bslot, pl.ds(s, BLK, stride=SUBLANES), :].astype(BF16)
        gather_start(i + 2)
        scatter_start(i - 1, n_rows(i - 1))
        hg = jnp.dot(xb_ref[...], wgu_bf[...], preferred_element_type=F32) + bgu_ref[0]
        nxt = pltpu.roll(hg, hg.shape[1] - 1, 1)
        gate = jnp.minimum(hg, SWIGLU_LIMIT)
        up = jnp.clip(nxt, -SWIGLU_LIMIT, SWIGLU_LIMIT)
        act = (up + 1.0) * (gate * jax.nn.sigmoid(SWIGLU_ALPHA * gate))
        y = jnp.dot(act.astype(BF16), wdx_bf[...], preferred_element_type=F32) + bd_ref[0]
        for s in range(SUBLANES):
            ybuf[bslot, pl.ds(s, BLK, stride=SUBLANES), :] = y[:, s * LANES:(s + 1) * LANES]

    @pl.when(i == n_used)
    def _():
        gather_wait(i)
        gather_wait(i + 1)
        scatter_start(i - 1, n_rows(i - 1))
        scatter_wait(i - 3, n_rows(i - 3))
        scatter_wait(i - 2, n_rows(i - 2))
        scatter_wait(i - 1, n_rows(i - 1))


def _experts(block_e, n_valid, n_used, idx, xn2, wgu, wd, bgu, bd, n_out_rows):
    nb = block_e.shape[0]
    D = wd.shape[2]
    F2 = wgu.shape[2]
    emap = lambda i, be, nv, nu: (be[i], 0, 0)
    grid_spec = pltpu.PrefetchScalarGridSpec(
        num_scalar_prefetch=3,
        grid=(nb,),
        in_specs=[
            pl.BlockSpec(memory_space=pl.ANY),
            pl.BlockSpec(memory_space=pl.ANY),
            pl.BlockSpec((1, D, F2), emap),
            pl.BlockSpec((1, F2 // 2, D), emap),
            pl.BlockSpec((1, 1, F2), emap),
            pl.BlockSpec((1, 1, D), emap),
        ],
        out_specs=pl.BlockSpec(memory_space=pl.ANY),
        scratch_shapes=[
            pltpu.SMEM((8, 2 * MOE_BLOCK), jnp.int32),
            pltpu.VMEM((3, MOE_BLOCK * SUBLANES, LANES), F32),
            pltpu.VMEM((3, MOE_BLOCK * SUBLANES, LANES), F32),
            pltpu.VMEM((MOE_BLOCK, D), BF16),
            pltpu.VMEM((D, F2), BF16),
            pltpu.VMEM((F2, D), BF16),
            pltpu.VMEM((D // LANES, F2 // 4, LANES), F32),
            pltpu.SemaphoreType.DMA((8,)),
            pltpu.SemaphoreType.DMA((3,)),
            pltpu.SemaphoreType.DMA((3,)),
        ],
    )
    return pl.pallas_call(
        _expert_kernel,
        grid_spec=grid_spec,
        out_shape=jax.ShapeDtypeStruct((n_out_rows * SUBLANES, LANES), F32),
        compiler_params=_params(("arbitrary",)),
        name="experts",
    )(block_e, n_valid, n_used, idx, xn2, wgu, wd, bgu, bd)


def _final_kernel(h1_ref, y0_ref, y1_ref, y2_ref, y3_ref, gate_ref, g_ref, o_ref):
    tm = h1_ref.shape[0]
    gts = gate_ref[...]
    g4 = [gts[:, kk:kk + 1] for kk in range(TOP_K)]
    pieces = []
    ss = None
    for s in range(SUBLANES):
        h = h1_ref[:, s * LANES:(s + 1) * LANES]
        for kk, y_ref in enumerate((y0_ref, y1_ref, y2_ref, y3_ref)):
            h = h + g4[kk] * y_ref[pl.ds(s, tm, stride=SUBLANES), :]
        pieces.append(h)
        sq = jnp.sum(h * h, axis=-1, keepdims=True)
        ss = sq if ss is None else ss + sq
    inv = lax.rsqrt(ss / h1_ref.shape[1] + RMS_EPS)
    for s in range(SUBLANES):
        o_ref[:, s * LANES:(s + 1) * LANES] = pieces[s] * inv * g_ref[:, s * LANES:(s + 1) * LANES]


def _final(h1, y4, gates, g_final):
    T, D = h1.shape
    tm = FINAL_TM
    nt = T // tm
    yspec = lambda kk: pl.BlockSpec((tm * SUBLANES, LANES), lambda i: (kk * nt + i, 0))
    return pl.pallas_call(
        _final_kernel,
        grid=(nt,),
        in_specs=[pl.BlockSpec((tm, D), lambda i: (i, 0)), yspec(0), yspec(1), yspec(2), yspec(3),
                  pl.BlockSpec((tm, LANES), lambda i: (i, 0)), pl.BlockSpec((1, D), lambda i: (0, 0))],
        out_specs=pl.BlockSpec((tm, D), lambda i: (i, 0)),
        out_shape=jax.ShapeDtypeStruct((T, D), F32),
        compiler_params=_params(("parallel",)),
        name="final",
    )(h1, y4, y4, y4, y4, gates, g_final)


def _rope_tables(positions):
    half = ROPE_DIM // 2
    inv_freq = ROPE_THETA ** (-jnp.arange(half, dtype=F32) / half)
    ang = positions.astype(F32).reshape(-1)[:, None] * inv_freq
    cos, sin = jnp.cos(ang), jnp.sin(ang)
    T = ang.shape[0]
    cos_t = jnp.concatenate([cos, cos, jnp.ones((T, HEAD_DIM - ROPE_DIM), F32)], axis=1)
    sa_t = jnp.concatenate([-sin, jnp.zeros((T, HEAD_DIM - half), F32)], axis=1)
    sb_t = jnp.concatenate([jnp.zeros((T, half), F32), sin, jnp.zeros((T, HEAD_DIM - ROPE_DIM), F32)], axis=1)
    return cos_t, sa_t, sb_t


def _routing(top_idx, T):
    A = T * TOP_K
    BLK = MOE_BLOCK
    n_blocks = A // BLK + N_EXPERTS
    flat_e = top_idx.reshape(A)
    order = jnp.sort(flat_e * A + jnp.arange(A, dtype=jnp.int32)) % A
    experts = jnp.arange(N_EXPERTS, dtype=jnp.int32)
    counts = jnp.sum((flat_e[None, :] == experts[:, None]).astype(jnp.int32), axis=1)
    starts = jnp.cumsum(counts) - counts
    padded = (counts + BLK - 1) // BLK * BLK
    padded_ends = jnp.cumsum(padded)
    padded_starts = padded_ends - padded
    block_start = jnp.arange(n_blocks, dtype=jnp.int32) * BLK
    block_e = jnp.minimum(jnp.sum((padded_ends[None, :] <= block_start[:, None]).astype(jnp.int32), axis=1),
                          N_EXPERTS - 1)
    n_valid = jnp.clip(padded_starts[block_e] + counts[block_e] - block_start, 0, BLK).astype(jnp.int32)
    n_used = (padded_ends[-1] // BLK).astype(jnp.int32).reshape(1)
    n_valid = jnp.where(block_start < padded_ends[-1], n_valid, 0)
    first = jnp.clip(starts[block_e] + block_start - padded_starts[block_e], 0, A)
    order_ext = jnp.concatenate([order, jnp.zeros((BLK,), jnp.int32)])
    row_a = jax.vmap(lambda s: lax.dynamic_slice(order_ext, (s,), (BLK,)))(first)
    row_tok = row_a // TOP_K
    row_slot = (row_a % TOP_K) * T + row_tok
    idx = jnp.concatenate([row_tok, row_slot], axis=1)
    block_e = jnp.concatenate([block_e, block_e[-1:]])
    n_valid = jnp.concatenate([n_valid, jnp.zeros((1,), jnp.int32)])
    idx = jnp.concatenate([idx, jnp.zeros((2, 2 * BLK), jnp.int32)], axis=0)
    return block_e, n_valid, n_used, idx, A


def kernel(x, positions, g_mix, w_in, conv_qk, b_if, g_mlstm_norm, w_attn_out, w_mlstm_out, w_mix_out, g_ffn,
           w_router, b_router, w_gate_up, b_gate_up, w_down, b_down, g_final):
    B, S, D = x.shape
    T = B * S
    l = 0
    x2 = x.reshape(T, D)

    w = w_in[l]
    o = np.cumsum((0, ATTN_WIDTH, ATTN_WIDTH, ATTN_WIDTH, M_WIDTH, M_WIDTH, M_WIDTH, M_WIDTH, M_HEADS, M_HEADS,
                   D_MODEL, D_MODEL))
    seg = lambda a: w[:, int(o[a]):int(o[a + 1])]
    w_main = jnp.concatenate([seg(0), seg(1), seg(9), seg(10), seg(3), seg(4), seg(5), seg(6), seg(2)],
                             axis=1).astype(BF16)
    w_if = jnp.pad(w[:, int(o[7]):int(o[9])], ((0, 0), (0, LANES - 2 * M_HEADS))).astype(BF16)
    cos_t, sa_t, sb_t = _rope_tables(positions)

    proj, ifo = _proj(x2, g_mix[l][None, :], w_main, w_if, cos_t, sa_t, sb_t)

    proj3 = proj.reshape(B, S, N_MAIN)
    attn_out = [_attention(proj3, g) for g in range(N_GROUPS)]
    aos = [o.reshape(T, ATTN_OUT) for o, _ in attn_out]
    lses = [l.reshape(T, LANES) for _, l in attn_out]

    nC = S // MLSTM_CHUNK
    gates = ifo[:, :2 * M_HEADS].reshape(B, S, 2 * M_HEADS).transpose(0, 2, 1).reshape(B, 2 * M_HEADS, nC, MLSTM_CHUNK)
    mem = _mlstm(proj3, gates, conv_qk[l][:, :M_WIDTH], conv_qk[l][:, M_WIDTH:], b_if[l],
                 g_mlstm_norm[l][None, :])

    w_r = jnp.pad(w_router[l], ((0, 0), (0, LANES - N_EXPERTS)))
    b_r = jnp.concatenate([b_router[l], jnp.full((LANES - N_EXPERTS,), NEG, F32)])[None, :]
    h1, xn2, top_idx, gates4 = _post(x2, aos, lses, mem.reshape(T, M_WIDTH), proj, w_attn_out[l].astype(BF16),
                                     w_mlstm_out[l].astype(BF16), w_mix_out[l].astype(BF16), g_ffn[l][None, :],
                                     w_r, b_r)

    block_e, n_valid, n_used, idx, n_out_rows = _routing(top_idx[:, :TOP_K], T)
    y4 = _experts(block_e, n_valid, n_used, idx, xn2, w_gate_up[l], w_down[l], b_gate_up[l][:, None, :],
                  b_down[l][:, None, :], n_out_rows)

    out = _final(h1, y4, gates4, g_final[None, :])
    return out.reshape(B, S, D)
```

```python
import functools

import jax
import jax.numpy as jnp
import numpy as np
from jax import lax
from jax.experimental import pallas as pl
from jax.experimental.pallas import tpu as pltpu

F32 = jnp.float32
BF16 = jnp.bfloat16

D_MODEL = 1024
N_GROUPS = 3
GROUP_DILATION = (1, 4, 16)
HEADS_PER_GROUP = 4
HEAD_DIM = 128
ATTN_WIDTH = N_GROUPS * HEADS_PER_GROUP * HEAD_DIM
ATTN_OUT = HEADS_PER_GROUP * HEAD_DIM
ATTN_BLOCK = 128
ROPE_THETA = 500000.0
ROPE_DIM = HEAD_DIM // 4
M_HEADS = 4
M_WIDTH = D_MODEL
M_HEAD_DIM = M_WIDTH // M_HEADS
CONV_WIDTH = 4
N_EXPERTS = 32
TOP_K = 4
D_FF = D_MODEL
SWIGLU_LIMIT = 7.0
SWIGLU_ALPHA = 1.702
RMS_EPS = 1e-5
LN_EPS = 1e-5

LANES = 128
SUBLANES = 8
VMEM_LIMIT = 56 * 1024 * 1024

PROJ_TM = 1024
PROJ_TN = 1536
MLSTM_CHUNK = 128
MLSTM_SEGMENT = 1024
POST_TM = 512
MOE_BLOCK = 256
FINAL_TM = 512
NEG = -1e30

COL_AQ = 0
COL_AK = ATTN_WIDTH
COL_GA = 2 * ATTN_WIDTH
COL_GM = COL_GA + D_MODEL
COL_MQ = COL_GM + D_MODEL
COL_MK = COL_MQ + M_WIDTH
COL_MV = COL_MK + M_WIDTH
COL_MO = COL_MV + M_WIDTH
COL_AV = COL_MO + M_WIDTH
N_MAIN = COL_AV + ATTN_WIDTH


def _sigmoid(x):
    return 0.5 * jnp.tanh(0.5 * x) + 0.5


def _params(sem):
    return pltpu.CompilerParams(dimension_semantics=sem, vmem_limit_bytes=VMEM_LIMIT)


def _proj_kernel(x_ref, g_ref, w_ref, wif_ref, cos_ref, sa_ref, sb_ref, o_ref, if_ref, xn_ref, *, rope_tiles):
    j = pl.program_id(1)

    @pl.when(j == 0)
    def _():
        x = x_ref[...]
        ms = jnp.mean(x * x, axis=-1, keepdims=True)
        xn = (x * lax.rsqrt(ms + RMS_EPS) * g_ref[...]).astype(BF16)
        xn_ref[...] = xn
        if_ref[...] = jnp.dot(xn, wif_ref[...], preferred_element_type=F32)

    acc = jnp.dot(xn_ref[...], w_ref[...], preferred_element_type=F32)

    @pl.when(j < rope_tiles)
    def _():
        c = cos_ref[...]
        sa = sa_ref[...]
        sb = sb_ref[...]
        for h in range(acc.shape[1] // HEAD_DIM):
            a = acc[:, h * HEAD_DIM:(h + 1) * HEAD_DIM]
            y = a * c + pltpu.roll(a, HEAD_DIM - ROPE_DIM // 2, 1) * sa + pltpu.roll(a, ROPE_DIM // 2, 1) * sb
            o_ref[:, h * HEAD_DIM:(h + 1) * HEAD_DIM] = y.astype(o_ref.dtype)

    @pl.when(j >= rope_tiles)
    def _():
        o_ref[...] = acc.astype(o_ref.dtype)


def _proj(x2, g_mix, w_main, w_if, cos_t, sa_t, sb_t):
    T, D = x2.shape
    N = w_main.shape[1]
    tm, tn = PROJ_TM, PROJ_TN
    grid = (T // tm, N // tn)
    return pl.pallas_call(
        functools.partial(_proj_kernel, rope_tiles=(2 * ATTN_WIDTH) // tn),
        grid=grid,
        in_specs=[
            pl.BlockSpec((tm, D), lambda i, j: (i, 0)),
            pl.BlockSpec((1, D), lambda i, j: (0, 0)),
            pl.BlockSpec((D, tn), lambda i, j: (0, j)),
            pl.BlockSpec((D, LANES), lambda i, j: (0, 0)),
            pl.BlockSpec((tm, LANES), lambda i, j: (i, 0)),
            pl.BlockSpec((tm, LANES), lambda i, j: (i, 0)),
            pl.BlockSpec((tm, LANES), lambda i, j: (i, 0)),
        ],
        out_specs=[
            pl.BlockSpec((tm, tn), lambda i, j: (i, j)),
            pl.BlockSpec((tm, LANES), lambda i, j: (i, 0)),
        ],
        out_shape=[jax.ShapeDtypeStruct((T, N), BF16), jax.ShapeDtypeStruct((T, LANES), F32)],
        scratch_shapes=[pltpu.VMEM((tm, D), BF16)],
        compiler_params=_params(("parallel", "arbitrary")),
        name="proj",
    )(x2, g_mix, w_main, w_if, cos_t, sa_t, sb_t)


def _attn_kernel(q_ref, k_ref, v_ref, o_ref, lse_ref, stage_ref, q_rm, k_rm, v_rm, o_rm, *, dilation):
    S = q_ref.shape[1]
    d = dilation
    L = S // d
    per_class = L // ATTN_BLOCK
    nblk = S // ATTN_BLOCK
    h = pl.program_id(1)

    if d == 1:
        q_of = lambda sl: q_ref[0, sl, :]
        k_of = lambda sl: k_ref[0, sl, :]
        v_of = lambda sl: v_ref[0, sl, :]
    else:
        for src, dst in ((q_ref, q_rm), (k_ref, k_rm), (v_ref, v_rm)):
            stage_ref[...] = src[0].astype(F32)
            if d <= SUBLANES:
                for r in range(d):
                    dst[r * L:(r + 1) * L, :] = stage_ref[pl.ds(r, L, stride=d), :].astype(BF16)
            else:
                d0 = int(round(d ** 0.5))
                assert d0 * d0 == d and d0 <= SUBLANES
                for a in range(d0):
                    o_rm[a * (S // d0):(a + 1) * (S // d0), :] = stage_ref[pl.ds(a, S // d0, stride=d0), :]
                for a in range(d0):
                    for b in range(d0):
                        r = a + d0 * b
                        dst[r * L:(r + 1) * L, :] = o_rm[pl.ds(a * (S // d0) + b, L, stride=d0), :].astype(BF16)
        q_of = lambda sl: q_rm[sl, :]
        k_of = lambda sl: k_rm[sl, :]
        v_of = lambda sl: v_rm[sl, :]

    @pl.when(h == 0)
    def _():
        lse_ref[...] = jnp.zeros_like(lse_ref)

    row = lax.broadcasted_iota(jnp.int32, (ATTN_BLOCK, ATTN_BLOCK), 0)
    col = lax.broadcasted_iota(jnp.int32, (ATTN_BLOCK, ATTN_BLOCK), 1)
    bias_cur = jnp.where(col <= row, 0.0, NEG).astype(F32)
    bias_prev = jnp.where(col >= row, 0.0, NEG).astype(F32)
    my_lane = col == h
    scale = HEAD_DIM ** -0.5
    dn = (((1,), (1,)), ((), ()))
    for n in range(nblk):
        sl = slice(n * ATTN_BLOCK, (n + 1) * ATTN_BLOCK)
        has_prev = n % per_class != 0
        q = q_of(sl)
        s_c = lax.dot_general(q, k_of(sl), dn, preferred_element_type=F32) * scale + bias_cur
        m = jnp.max(s_c, axis=1, keepdims=True)
        if has_prev:
            sp = slice((n - 1) * ATTN_BLOCK, n * ATTN_BLOCK)
            s_p = lax.dot_general(q, k_of(sp), dn, preferred_element_type=F32) * scale + bias_prev
            m = jnp.maximum(m, jnp.max(s_p, axis=1, keepdims=True))
        p_c = jnp.exp(s_c - m)
        den = jnp.sum(p_c, axis=1, keepdims=True)
        acc = jnp.dot(p_c.astype(BF16), v_of(sl), preferred_element_type=F32)
        if has_prev:
            p_p = jnp.exp(s_p - m)
            den = den + jnp.sum(p_p, axis=1, keepdims=True)
            acc = acc + jnp.dot(p_p.astype(BF16), v_of(sp), preferred_element_type=F32)
        lse = m + jnp.log(den)
        start = n // per_class + d * (n % per_class) * ATTN_BLOCK
        rows = pl.ds(start, ATTN_BLOCK, stride=d) if d > 1 else sl
        o_rm[rows, :] = acc / den
        lse_ref[0, rows, :] = jnp.where(my_lane, lse, lse_ref[0, rows, :])
    o_ref[0] = o_rm[...].astype(o_ref.dtype)


def _attention(proj3, g):
    B, S, _ = proj3.shape
    Dh = HEAD_DIM

    def col(base):
        return pl.BlockSpec((1, S, Dh), lambda b, h: (b, 0, (base + g * ATTN_OUT) // Dh + h))

    return pl.pallas_call(
        functools.partial(_attn_kernel, dilation=GROUP_DILATION[g]),
        grid=(B, HEADS_PER_GROUP),
        in_specs=[col(COL_AQ), col(COL_AK), col(COL_AV)],
        out_specs=[pl.BlockSpec((1, S, Dh), lambda b, h: (b, 0, h)),
                   pl.BlockSpec((1, S, LANES), lambda b, h: (b, 0, 0))],
        out_shape=[jax.ShapeDtypeStruct((B, S, ATTN_OUT), BF16), jax.ShapeDtypeStruct((B, S, LANES), F32)],
        scratch_shapes=[pltpu.VMEM((S, Dh), F32), pltpu.VMEM((S, Dh), BF16), pltpu.VMEM((S, Dh), BF16),
                        pltpu.VMEM((S, Dh), BF16), pltpu.VMEM((S, Dh), F32)],
        compiler_params=_params(("parallel", "arbitrary")),
        name=f"attn{g}",
    )(proj3, proj3, proj3)


def _mlstm_kernel(bif_ref, mq_ref, mk_ref, mv_ref, mo_ref, g_ref, cwq_ref, cwk_ref, gn_ref, o_ref,
                  pad_ref, qs_ref, ks_ref, c_ref, n_ref, m_ref, hq_ref, hk_ref):
    SEG = mq_ref.shape[1]
    LC = MLSTM_CHUNK
    Dh = M_HEAD_DIM
    H = M_HEADS
    seg = pl.program_id(1)

    @pl.when(seg == 0)
    def _():
        c_ref[...] = jnp.zeros_like(c_ref)
        n_ref[...] = jnp.zeros_like(n_ref)
        m_ref[...] = jnp.zeros_like(m_ref)
        hq_ref[...] = jnp.zeros_like(hq_ref)
        hk_ref[...] = jnp.zeros_like(hk_ref)

    def conv_silu(src_ref, hist_ref, w_ref, cols, dst_ref, scale):
        pad_ref[0:SUBLANES, :] = hist_ref[:, cols]
        pad_ref[SUBLANES:SUBLANES + SEG, :] = src_ref[0, :, cols].astype(F32)
        hist_ref[:, cols] = pad_ref[SEG:SEG + SUBLANES, :]
        rb = 2 * LC
        for r0 in range(0, SEG, rb):
            acc = None
            for j in range(CONV_WIDTH):
                off = r0 + SUBLANES - (CONV_WIDTH - 1) + j
                term = w_ref[j:j + 1, cols] * pad_ref[off:off + rb, :]
                acc = term if acc is None else acc + term
            dst_ref[r0:r0 + rb, cols] = (acc * _sigmoid(acc) * scale).astype(dst_ref.dtype)

    for hd in range(H):
        cols = slice(hd * Dh, (hd + 1) * Dh)
        conv_silu(mq_ref, hq_ref, cwq_ref, cols, qs_ref, 1.0)
        conv_silu(mk_ref, hk_ref, cwk_ref, cols, ks_ref, Dh ** -0.5)

    row = lax.broadcasted_iota(jnp.int32, (LC, LC), 0)
    col = lax.broadcasted_iota(jnp.int32, (LC, LC), 1)
    causal = col <= row
    eye = col == row
    lane8 = lax.broadcasted_iota(jnp.int32, (SUBLANES, LC), 1)

    def head_chunk(hd, c, r0, m):
        cols = slice(hd * Dh, (hd + 1) * Dh)
        c_ref_h = c_ref.at[hd]
        gn = gn_ref[:, cols]
        q = qs_ref[pl.ds(r0, LC), cols]
        k = ks_ref[pl.ds(r0, LC), cols]
        v = mv_ref[0, pl.ds(r0, LC), cols]
        i_row = g_ref[0, hd, pl.ds(c, 1), :] + bif_ref[hd]
        f_row = g_ref[0, H + hd, pl.ds(c, 1), :] + bif_ref[H + hd]
        logf = jnp.minimum(f_row, 0.0) - jnp.log(1.0 + jnp.exp(-jnp.abs(f_row)))
        b8 = jnp.broadcast_to(logf, (SUBLANES, LC))
        s = 1
        while s < LC:
            b8 = b8 + jnp.where(lane8 >= s, pltpu.roll(b8, s, 1), 0.0)
            s *= 2
        b_row = b8[0:1, :]
        d_row = i_row - b_row
        b_col = jnp.sum(jnp.where(eye, b_row, 0.0), axis=1, keepdims=True)
        d_col = jnp.sum(jnp.where(eye, d_row, 0.0), axis=1, keepdims=True)
        dm = jnp.where(causal, b_col + d_row, NEG)
        inter = b_col + m
        m_t = jnp.maximum(inter, jnp.max(dm, axis=1, keepdims=True))
        w_intra = jnp.exp(dm - m_t)
        w_inter = jnp.exp(inter - m_t)
        kb = k.astype(BF16)
        a = lax.dot_general(q, kb, (((1,), (1,)), ((), ())), preferred_element_type=F32) * w_intra
        num = (jnp.dot(a.astype(BF16), v, preferred_element_type=F32)
               + w_inter * jnp.dot(q, c_ref_h[...].astype(BF16), preferred_element_type=F32))
        den = (jnp.sum(a, axis=1, keepdims=True)
               + w_inter * jnp.sum(q.astype(F32) * n_ref[:, cols], axis=1, keepdims=True))
        hh = num / jnp.maximum(jnp.abs(den), jnp.exp(-m_t))
        mu = jnp.mean(hh, axis=1, keepdims=True)
        xc = hh - mu
        var = jnp.mean(xc * xc, axis=1, keepdims=True)
        hn = xc * lax.rsqrt(var + LN_EPS) * gn
        gate = _sigmoid(mo_ref[0, pl.ds(r0, LC), cols].astype(F32))
        o_ref[0, pl.ds(r0, LC), cols] = (gate * hn).astype(o_ref.dtype)
        b_last = b_row[:, LC - 1:LC]
        m_new = jnp.maximum(b_last + m, jnp.max(b_last + d_row, axis=1, keepdims=True))
        decay = jnp.exp(b_last + m - m_new)
        kw = k * jnp.exp(b_last + d_col - m_new)
        c_ref_h[...] = decay * c_ref_h[...] + lax.dot_general(
            kw.astype(BF16), v, (((0,), (0,)), ((), ())), preferred_element_type=F32)
        n_ref[:, cols] = decay * n_ref[:, cols] + jnp.sum(kw, axis=0, keepdims=True)
        return m_new

    def chunk(c, ms):
        r0 = pl.multiple_of(c * LC, LC)
        return tuple(head_chunk(hd, c, r0, ms[hd]) for hd in range(H))

    ms = lax.fori_loop(0, SEG // LC, chunk, tuple(m_ref[:, hd:hd + 1] for hd in range(H)))
    for hd in range(H):
        m_ref[:, hd:hd + 1] = ms[hd]


def _mlstm(proj3, gates, conv_q, conv_k, b_if, g_norm):
    B, S, _ = proj3.shape
    W = M_WIDTH
    SEG = MLSTM_SEGMENT
    nseg = S // SEG
    nC = SEG // MLSTM_CHUNK

    def col(base):
        return pl.BlockSpec((1, SEG, W), lambda b, s, pre: (b, s, base // W))

    full = lambda a: pl.BlockSpec(a.shape, lambda b, s, pre: (0,) * a.ndim)
    grid_spec = pltpu.PrefetchScalarGridSpec(
        num_scalar_prefetch=1,
        grid=(B, nseg),
        in_specs=[
            col(COL_MQ), col(COL_MK), col(COL_MV), col(COL_MO),
            pl.BlockSpec((1, 2 * M_HEADS, nC, MLSTM_CHUNK), lambda b, s, pre: (b, 0, s, 0)),
            full(conv_q), full(conv_k), full(g_norm),
        ],
        out_specs=pl.BlockSpec((1, SEG, W), lambda b, s, pre: (b, s, 0)),
        scratch_shapes=[
            pltpu.VMEM((SEG + SUBLANES, M_HEAD_DIM), F32),
            pltpu.VMEM((SEG, W), BF16),
            pltpu.VMEM((SEG, W), F32),
            pltpu.VMEM((M_HEADS, M_HEAD_DIM, M_HEAD_DIM), F32),
            pltpu.VMEM((1, W), F32),
            pltpu.VMEM((1, LANES), F32),
            pltpu.VMEM((SUBLANES, W), F32),
            pltpu.VMEM((SUBLANES, W), F32),
        ],
    )
    return pl.pallas_call(
        _mlstm_kernel,
        grid_spec=grid_spec,
        out_shape=jax.ShapeDtypeStruct((B, S, W), BF16),
        compiler_params=_params(("parallel", "arbitrary")),
        name="mlstm",
    )(b_if, proj3, proj3, proj3, proj3, gates, conv_q, conv_k, g_norm)


def _post_kernel(x_ref, ao0_ref, ao1_ref, ao2_ref, lse0_ref, lse1_ref, lse2_ref, mem_ref, ga_ref, gm_ref,
                 wa_ref, wm_ref, wo_ref, gf_ref, wr_ref, wrl_ref, br_ref, h1_ref, xn_ref, idx_ref, gate_ref):
    tm = x_ref.shape[0]
    ao_refs = (ao0_ref, ao1_ref, ao2_ref)
    lses = (lse0_ref[...], lse1_ref[...], lse2_ref[...])
    heads = []
    for h in range(HEADS_PER_GROUP):
        ls = [l[:, h:h + 1] for l in lses]
        mx = jnp.maximum(jnp.maximum(ls[0], ls[1]), ls[2])
        es = [jnp.exp(l - mx) for l in ls]
        tot = es[0] + es[1] + es[2]
        acc = None
        for g in range(N_GROUPS):
            term = (es[g] / tot) * ao_refs[g][:, h * HEAD_DIM:(h + 1) * HEAD_DIM].astype(F32)
            acc = term if acc is None else acc + term
        heads.append(acc.astype(BF16))
    attn = jnp.concatenate(heads, axis=1)
    ya = jnp.dot(attn, wa_ref[...], preferred_element_type=F32)
    ym = jnp.dot(mem_ref[...], wm_ref[...], preferred_element_type=F32)
    merged = _sigmoid(ga_ref[...].astype(F32)) * ya + _sigmoid(gm_ref[...].astype(F32)) * ym
    h1 = x_ref[...] + jnp.dot(merged.astype(BF16), wo_ref[...], preferred_element_type=F32)
    h1_ref[...] = h1
    ms = jnp.mean(h1 * h1, axis=-1, keepdims=True)
    xn = h1 * lax.rsqrt(ms + RMS_EPS) * gf_ref[...]
    for s in range(SUBLANES):
        xn_ref[pl.ds(s, tm, stride=SUBLANES), :] = xn[:, s * LANES:(s + 1) * LANES]
    xn_hi = xn.astype(BF16)
    xn_lo = (xn - xn_hi.astype(F32)).astype(BF16)
    logits = (jnp.dot(xn_hi, wr_ref[...], preferred_element_type=F32)
              + jnp.dot(xn_hi, wrl_ref[...], preferred_element_type=F32)
              + jnp.dot(xn_lo, wr_ref[...], preferred_element_type=F32)) + br_ref[...]
    lane = lax.broadcasted_iota(jnp.int32, (tm, LANES), 1).astype(F32)
    vals = logits
    tops, idxs = [], []
    for _ in range(TOP_K):
        mx = jnp.max(vals, axis=1, keepdims=True)
        ix = jnp.min(jnp.where(vals == mx, lane, float(LANES)), axis=1, keepdims=True)
        tops.append(mx)
        idxs.append(ix)
        vals = jnp.where(lane == ix, NEG, vals)
    es = [jnp.exp(t - tops[0]) for t in tops]
    tot = es[0] + es[1] + es[2] + es[3]
    idx_out = jnp.zeros((tm, LANES), F32)
    gate_out = jnp.zeros((tm, LANES), F32)
    for kk in range(TOP_K):
        idx_out = jnp.where(lane == float(kk), idxs[kk], idx_out)
        gate_out = jnp.where(lane == float(kk), es[kk] / tot, gate_out)
    idx_ref[...] = idx_out.astype(jnp.int32)
    gate_ref[...] = gate_out


def _post(x2, aos, lses, mem, proj, w_a, w_m, w_o, g_ffn, w_r, w_r_lo, b_r):
    T, D = x2.shape
    tm = POST_TM
    row = lambda w: pl.BlockSpec((tm, w), lambda i: (i, 0))
    full = lambda a: pl.BlockSpec(a.shape, lambda i: (0,) * a.ndim)
    return pl.pallas_call(
        _post_kernel,
        grid=(T // tm,),
        in_specs=[
            row(D), row(ATTN_OUT), row(ATTN_OUT), row(ATTN_OUT), row(LANES), row(LANES), row(LANES), row(M_WIDTH),
            pl.BlockSpec((tm, D), lambda i: (i, COL_GA // D)),
            pl.BlockSpec((tm, D), lambda i: (i, COL_GM // D)),
            full(w_a), full(w_m), full(w_o), full(g_ffn), full(w_r), full(w_r_lo), full(b_r),
        ],
        out_specs=[row(D), pl.BlockSpec((tm * SUBLANES, LANES), lambda i: (i, 0)), row(LANES), row(LANES)],
        out_shape=[jax.ShapeDtypeStruct((T, D), F32), jax.ShapeDtypeStruct((T * SUBLANES, LANES), F32),
                   jax.ShapeDtypeStruct((T, LANES), jnp.int32), jax.ShapeDtypeStruct((T, LANES), F32)],
        compiler_params=_params(("parallel",)),
        name="post",
    )(x2, *aos, *lses, mem, proj, proj, w_a, w_m, w_o, g_ffn, w_r, w_r_lo, b_r)


def _expert_kernel(be_ref, nv_ref, nu_ref, idx_hbm, xn_hbm, wgu_ref, wd_ref, bgu_ref, bd_ref, y_hbm,
                   idx_smem, xbuf, ybuf, xb_ref, wgu_bf, wdx_bf, wd_tmp, isem, gsem, ssem):
    i = pl.program_id(0)
    n_idx = idx_hbm.shape[0]
    n_used = nu_ref[0]
    BLK = MOE_BLOCK
    n_islots = idx_smem.shape[0]
    n_bufs = xbuf.shape[0]

    def tile(t):
        return pl.ds(pl.multiple_of(t * SUBLANES, SUBLANES), SUBLANES)

    def tiles(n):
        return pl.ds(0, pl.multiple_of(n * SUBLANES, SUBLANES))

    def idx_copy(blk):
        slot = blk % n_islots
        return pltpu.make_async_copy(idx_hbm.at[blk], idx_smem.at[slot], isem.at[slot])

    def gather_start(blk):
        islot, bslot = blk % n_islots, blk % n_bufs
        for r in range(BLK):
            tok = idx_smem[islot, r]
            pltpu.make_async_copy(xn_hbm.at[tile(tok)], xbuf.at[bslot, pl.ds(r * SUBLANES, SUBLANES)],
                                  gsem.at[bslot]).start()

    def gather_wait(blk):
        bslot = blk % n_bufs
        pltpu.make_async_copy(xn_hbm.at[tiles(BLK)], xbuf.at[bslot], gsem.at[bslot]).wait()

    def scatter_start(blk, n):
        islot, bslot = blk % n_islots, blk % n_bufs
        for r in range(BLK):
            @pl.when(r < n)
            def _():
                dst = idx_smem[islot, BLK + r]
                pltpu.make_async_copy(ybuf.at[bslot, pl.ds(r * SUBLANES, SUBLANES)], y_hbm.at[tile(dst)],
                                      ssem.at[bslot]).start()

    def scatter_wait(blk, n):
        bslot = blk % n_bufs

        @pl.when(n > 0)
        def _():
            pltpu.make_async_copy(ybuf.at[bslot, tiles(n)], y_hbm.at[tiles(n)], ssem.at[bslot]).wait()

    def n_rows(blk):
        return jnp.where(blk >= 0, nv_ref[jnp.maximum(blk, 0)], 0)

    @pl.when(i == 0)
    def _():
        idx_copy(0).start()
        idx_copy(1).start()
        idx_copy(2).start()
        idx_copy(0).wait()
        idx_copy(1).wait()
        gather_start(0)
        gather_start(1)
        wd_tmp[...] = jnp.zeros_like(wd_tmp)

    @pl.when(i + 3 < n_idx)
    def _():
        idx_copy(i + 3).start()

    @pl.when(i + 2 < n_idx)
    def _():
        idx_copy(i + 2).wait()

    @pl.when(jnp.logical_and(i < n_used, jnp.logical_or(i == 0, be_ref[i] != be_ref[jnp.maximum(i - 1, 0)])))
    def _():
        wgu_bf[...] = wgu_ref[0].astype(BF16)
        rows = wd_tmp.shape[1] // 2
        for c in range(wd_ref.shape[1] // rows):
            for s in range(wd_ref.shape[2] // LANES):
                wd_tmp.at[s][pl.ds(0, rows, stride=2), :] = wd_ref[0, c * rows:(c + 1) * rows,
                                                                   s * LANES:(s + 1) * LANES]
            for s in range(wd_ref.shape[2] // LANES):
                wdx_bf[2 * c * rows:2 * (c + 1) * rows, s * LANES:(s + 1) * LANES] = wd_tmp[s].astype(BF16)

    @pl.when(i < n_used)
    def _():
        gather_wait(i)
        scatter_wait(i - 3, n_rows(i - 3))

    @pl.when(i < n_used)
    def _():
        bslot = i % n_bufs
        for s in range(SUBLANES):
            xb_ref[:, s * LANES:(s + 1) * LANES] = xbuf[bslot, pl.ds(s, BLK, stride=SUBLANES), :].astype(BF16)
        gather_start(i + 2)
        scatter_start(i - 1, n_rows(i - 1))
        hg = jnp.dot(xb_ref[...], wgu_bf[...], preferred_element_type=F32) + bgu_ref[0]
        nxt = pltpu.roll(hg, hg.shape[1] - 1, 1)
        gate = jnp.minimum(hg, SWIGLU_LIMIT)
        up = jnp.clip(nxt, -SWIGLU_LIMIT, SWIGLU_LIMIT)
        act = (up + 1.0) * (gate * _sigmoid(SWIGLU_ALPHA * gate))
        y = jnp.dot(act.astype(BF16), wdx_bf[...], preferred_element_type=F32) + bd_ref[0]
        for s in range(SUBLANES):
            ybuf[bslot, pl.ds(s, BLK, stride=SUBLANES), :] = y[:, s * LANES:(s + 1) * LANES]

    @pl.when(i == n_used)
    def _():
        gather_wait(i)
        gather_wait(i + 1)
        scatter_start(i - 1, n_rows(i - 1))
        scatter_wait(i - 3, n_rows(i - 3))
        scatter_wait(i - 2, n_rows(i - 2))
        scatter_wait(i - 1, n_rows(i - 1))


def _experts(block_e, n_valid, n_used, idx, xn2, wgu, wd, bgu, bd, n_out_rows):
    nb = block_e.shape[0]
    D = wd.shape[2]
    F2 = wgu.shape[2]
    emap = lambda i, be, nv, nu: (be[i], 0, 0)
    grid_spec = pltpu.PrefetchScalarGridSpec(
        num_scalar_prefetch=3,
        grid=(nb,),
        in_specs=[
            pl.BlockSpec(memory_space=pl.ANY),
            pl.BlockSpec(memory_space=pl.ANY),
            pl.BlockSpec((1, D, F2), emap),
            pl.BlockSpec((1, F2 // 2, D), emap),
            pl.BlockSpec((1, 1, F2), emap),
            pl.BlockSpec((1, 1, D), emap),
        ],
        out_specs=pl.BlockSpec(memory_space=pl.ANY),
        scratch_shapes=[
            pltpu.SMEM((8, 2 * MOE_BLOCK), jnp.int32),
            pltpu.VMEM((3, MOE_BLOCK * SUBLANES, LANES), F32),
            pltpu.VMEM((3, MOE_BLOCK * SUBLANES, LANES), F32),
            pltpu.VMEM((MOE_BLOCK, D), BF16),
            pltpu.VMEM((D, F2), BF16),
            pltpu.VMEM((F2, D), BF16),
            pltpu.VMEM((D // LANES, F2 // 4, LANES), F32),
            pltpu.SemaphoreType.DMA((8,)),
            pltpu.SemaphoreType.DMA((3,)),
            pltpu.SemaphoreType.DMA((3,)),
        ],
    )
    return pl.pallas_call(
        _expert_kernel,
        grid_spec=grid_spec,
        out_shape=jax.ShapeDtypeStruct((n_out_rows * SUBLANES, LANES), F32),
        compiler_params=_params(("arbitrary",)),
        name="experts",
    )(block_e, n_valid, n_used, idx, xn2, wgu, wd, bgu, bd)


def _final_kernel(h1_ref, y0_ref, y1_ref, y2_ref, y3_ref, gate_ref, g_ref, o_ref):
    tm = h1_ref.shape[0]
    gts = gate_ref[...]
    g4 = [gts[:, kk:kk + 1] for kk in range(TOP_K)]
    pieces = []
    ss = None
    for s in range(SUBLANES):
        h = h1_ref[:, s * LANES:(s + 1) * LANES]
        for kk, y_ref in enumerate((y0_ref, y1_ref, y2_ref, y3_ref)):
            h = h + g4[kk] * y_ref[pl.ds(s, tm, stride=SUBLANES), :]
        pieces.append(h)
        sq = jnp.sum(h * h, axis=-1, keepdims=True)
        ss = sq if ss is None else ss + sq
    inv = lax.rsqrt(ss / h1_ref.shape[1] + RMS_EPS)
    for s in range(SUBLANES):
        o_ref[:, s * LANES:(s + 1) * LANES] = pieces[s] * inv * g_ref[:, s * LANES:(s + 1) * LANES]


def _final(h1, y4, gates, g_final):
    T, D = h1.shape
    tm = FINAL_TM
    nt = T // tm
    yspec = lambda kk: pl.BlockSpec((tm * SUBLANES, LANES), lambda i: (kk * nt + i, 0))
    return pl.pallas_call(
        _final_kernel,
        grid=(nt,),
        in_specs=[pl.BlockSpec((tm, D), lambda i: (i, 0)), yspec(0), yspec(1), yspec(2), yspec(3),
                  pl.BlockSpec((tm, LANES), lambda i: (i, 0)), pl.BlockSpec((1, D), lambda i: (0, 0))],
        out_specs=pl.BlockSpec((tm, D), lambda i: (i, 0)),
        out_shape=jax.ShapeDtypeStruct((T, D), F32),
        compiler_params=_params(("parallel",)),
        name="final",
    )(h1, y4, y4, y4, y4, gates, g_final)


def _rope_tables(positions):
    half = ROPE_DIM // 2
    inv_freq = ROPE_THETA ** (-jnp.arange(half, dtype=F32) / half)
    ang = positions.astype(F32).reshape(-1)[:, None] * inv_freq
    cos, sin = jnp.cos(ang), jnp.sin(ang)
    T = ang.shape[0]
    cos_t = jnp.concatenate([cos, cos, jnp.ones((T, HEAD_DIM - ROPE_DIM), F32)], axis=1)
    sa_t = jnp.concatenate([-sin, jnp.zeros((T, HEAD_DIM - half), F32)], axis=1)
    sb_t = jnp.concatenate([jnp.zeros((T, half), F32), sin, jnp.zeros((T, HEAD_DIM - ROPE_DIM), F32)], axis=1)
    return cos_t, sa_t, sb_t


def _routing(top_idx, T):
    A = T * TOP_K
    BLK = MOE_BLOCK
    n_blocks = A // BLK + N_EXPERTS
    flat_e = top_idx.reshape(A)
    experts = jnp.arange(N_EXPERTS, dtype=jnp.int32)
    counts = jnp.sum((flat_e[None, :] == experts[:, None]).astype(jnp.int32), axis=1)
    padded = (counts + BLK - 1) // BLK * BLK
    padded_ends = jnp.cumsum(padded)
    padded_starts = padded_ends - padded
    block_start = jnp.arange(n_blocks, dtype=jnp.int32) * BLK
    block_e = jnp.minimum(jnp.sum((padded_ends[None, :] <= block_start[:, None]).astype(jnp.int32), axis=1),
                          N_EXPERTS - 1)
    n_valid = jnp.clip(padded_starts[block_e] + counts[block_e] - block_start, 0, BLK).astype(jnp.int32)
    n_used = (padded_ends[-1] // BLK).astype(jnp.int32).reshape(1)
    n_valid = jnp.where(block_start < padded_ends[-1], n_valid, 0)
    pad_e = jnp.repeat(experts, BLK)
    pad_r = jnp.tile(jnp.arange(BLK, dtype=jnp.int32), N_EXPERTS)
    pad_key = jnp.where(pad_r < (padded - counts)[pad_e], pad_e * (2 * A) + A + pad_r, N_EXPERTS * 2 * A)
    keys = jnp.sort(jnp.concatenate([flat_e * (2 * A) + jnp.arange(A, dtype=jnp.int32), pad_key]))
    row_a = keys % (2 * A)
    row_a = jnp.where(row_a < A, row_a, 0).reshape(n_blocks, BLK)
    row_tok = row_a // TOP_K
    row_slot = (row_a % TOP_K) * T + row_tok
    idx = jnp.concatenate([row_tok, row_slot], axis=1)
    block_e = jnp.concatenate([block_e, block_e[-1:]])
    n_valid = jnp.concatenate([n_valid, jnp.zeros((1,), jnp.int32)])
    idx = jnp.concatenate([idx, jnp.zeros((2, 2 * BLK), jnp.int32)], axis=0)
    return block_e, n_valid, n_used, idx, A


def kernel(x, positions, g_mix, w_in, conv_qk, b_if, g_mlstm_norm, w_attn_out, w_mlstm_out, w_mix_out, g_ffn,
           w_router, b_router, w_gate_up, b_gate_up, w_down, b_down, g_final):
    B, S, D = x.shape
    T = B * S
    l = 0
    x2 = x.reshape(T, D)

    w = w_in[l]
    o = np.cumsum((0, ATTN_WIDTH, ATTN_WIDTH, ATTN_WIDTH, M_WIDTH, M_WIDTH, M_WIDTH, M_WIDTH, M_HEADS, M_HEADS,
                   D_MODEL, D_MODEL))
    seg = lambda a: w[:, int(o[a]):int(o[a + 1])]
    w_main = jnp.concatenate([seg(0), seg(1), seg(9), seg(10), seg(3), seg(4), seg(5), seg(6), seg(2)],
                             axis=1).astype(BF16)
    w_if = jnp.pad(w[:, int(o[7]):int(o[9])], ((0, 0), (0, LANES - 2 * M_HEADS))).astype(BF16)
    cos_t, sa_t, sb_t = _rope_tables(positions)

    proj, ifo = _proj(x2, g_mix[l][None, :], w_main, w_if, cos_t, sa_t, sb_t)

    proj3 = proj.reshape(B, S, N_MAIN)
    attn_out = [_attention(proj3, g) for g in range(N_GROUPS)]
    aos = [o.reshape(T, ATTN_OUT) for o, _ in attn_out]
    lses = [l.reshape(T, LANES) for _, l in attn_out]

    nC = S // MLSTM_CHUNK
    gates = ifo[:, :2 * M_HEADS].reshape(B, S, 2 * M_HEADS).transpose(0, 2, 1).reshape(B, 2 * M_HEADS, nC, MLSTM_CHUNK)
    mem = _mlstm(proj3, gates, conv_qk[l][:, :M_WIDTH], conv_qk[l][:, M_WIDTH:], b_if[l],
                 g_mlstm_norm[l][None, :])

    w_r32 = jnp.pad(w_router[l], ((0, 0), (0, LANES - N_EXPERTS)))
    w_r = w_r32.astype(BF16)
    w_r_lo = (w_r32 - w_r.astype(F32)).astype(BF16)
    b_r = jnp.concatenate([b_router[l], jnp.full((LANES - N_EXPERTS,), NEG, F32)])[None, :]
    h1, xn2, top_idx, gates4 = _post(x2, aos, lses, mem.reshape(T, M_WIDTH), proj, w_attn_out[l].astype(BF16),
                                     w_mlstm_out[l].astype(BF16), w_mix_out[l].astype(BF16), g_ffn[l][None, :],
                                     w_r, w_r_lo, b_r)

    block_e, n_valid, n_used, idx, n_out_rows = _routing(top_idx[:, :TOP_K], T)
    y4 = _experts(block_e, n_valid, n_used, idx, xn2, w_gate_up[l], w_down[l], b_gate_up[l][:, None, :],
                  b_down[l][:, None, :], n_out_rows)

    out = _final(h1, y4, gates4, g_final[None, :])
    return out.reshape(B, S, D)
```

```python
import functools

import jax
import jax.numpy as jnp
import numpy as np
from jax import lax
from jax.experimental import pallas as pl
from jax.experimental.pallas import tpu as pltpu

F32 = jnp.float32
BF16 = jnp.bfloat16

D_MODEL = 1024
N_GROUPS = 3
GROUP_DILATION = (1, 4, 16)
HEADS_PER_GROUP = 4
HEAD_DIM = 128
ATTN_WIDTH = N_GROUPS * HEADS_PER_GROUP * HEAD_DIM
ATTN_OUT = HEADS_PER_GROUP * HEAD_DIM
ATTN_BLOCK = 128
ROPE_THETA = 500000.0
ROPE_DIM = HEAD_DIM // 4
M_HEADS = 4
M_WIDTH = D_MODEL
M_HEAD_DIM = M_WIDTH // M_HEADS
CONV_WIDTH = 4
N_EXPERTS = 32
TOP_K = 4
D_FF = D_MODEL
SWIGLU_LIMIT = 7.0
SWIGLU_ALPHA = 1.702
RMS_EPS = 1e-5
LN_EPS = 1e-5

LANES = 128
SUBLANES = 8
VMEM_LIMIT = 56 * 1024 * 1024

PROJ_TM = 1024
PROJ_TN = 1536
MLSTM_CHUNK = 128
MLSTM_SEGMENT = 1024
POST_TM = 512
MOE_BLOCK = 256
FINAL_TM = 512
NEG = -1e30

COL_AQ = 0
COL_AK = ATTN_WIDTH
COL_GA = 2 * ATTN_WIDTH
COL_GM = COL_GA + D_MODEL
COL_MQ = COL_GM + D_MODEL
COL_MK = COL_MQ + M_WIDTH
COL_MV = COL_MK + M_WIDTH
COL_MO = COL_MV + M_WIDTH
COL_AV = COL_MO + M_WIDTH
N_MAIN = COL_AV + ATTN_WIDTH


def _sigmoid(x):
    return 0.5 * jnp.tanh(0.5 * x) + 0.5


def _params(sem):
    return pltpu.CompilerParams(dimension_semantics=sem, vmem_limit_bytes=VMEM_LIMIT)


def _proj_kernel(x_ref, g_ref, w_ref, wif_ref, cos_ref, sa_ref, sb_ref, o_ref, if_ref, xn_ref, *, rope_tiles):
    j = pl.program_id(1)

    @pl.when(j == 0)
    def _():
        x = x_ref[...]
        ms = jnp.mean(x * x, axis=-1, keepdims=True)
        xn = (x * lax.rsqrt(ms + RMS_EPS) * g_ref[...]).astype(BF16)
        xn_ref[...] = xn
        if_ref[...] = jnp.dot(xn, wif_ref[...], preferred_element_type=F32)

    acc = jnp.dot(xn_ref[...], w_ref[...], preferred_element_type=F32)

    @pl.when(j < rope_tiles)
    def _():
        c = cos_ref[...]
        sa = sa_ref[...]
        sb = sb_ref[...]
        for h in range(acc.shape[1] // HEAD_DIM):
            a = acc[:, h * HEAD_DIM:(h + 1) * HEAD_DIM]
            y = a * c + pltpu.roll(a, HEAD_DIM - ROPE_DIM // 2, 1) * sa + pltpu.roll(a, ROPE_DIM // 2, 1) * sb
            o_ref[:, h * HEAD_DIM:(h + 1) * HEAD_DIM] = y.astype(o_ref.dtype)

    @pl.when(j >= rope_tiles)
    def _():
        o_ref[...] = acc.astype(o_ref.dtype)


def _proj(x2, g_mix, w_main, w_if, cos_t, sa_t, sb_t):
    T, D = x2.shape
    N = w_main.shape[1]
    tm, tn = PROJ_TM, PROJ_TN
    grid = (T // tm, N // tn)
    return pl.pallas_call(
        functools.partial(_proj_kernel, rope_tiles=(2 * ATTN_WIDTH) // tn),
        grid=grid,
        in_specs=[
            pl.BlockSpec((tm, D), lambda i, j: (i, 0)),
            pl.BlockSpec((1, D), lambda i, j: (0, 0)),
            pl.BlockSpec((D, tn), lambda i, j: (0, j)),
            pl.BlockSpec((D, LANES), lambda i, j: (0, 0)),
            pl.BlockSpec((tm, LANES), lambda i, j: (i, 0)),
            pl.BlockSpec((tm, LANES), lambda i, j: (i, 0)),
            pl.BlockSpec((tm, LANES), lambda i, j: (i, 0)),
        ],
        out_specs=[
            pl.BlockSpec((tm, tn), lambda i, j: (i, j)),
            pl.BlockSpec((tm, LANES), lambda i, j: (i, 0)),
        ],
        out_shape=[jax.ShapeDtypeStruct((T, N), BF16), jax.ShapeDtypeStruct((T, LANES), F32)],
        scratch_shapes=[pltpu.VMEM((tm, D), BF16)],
        compiler_params=_params(("parallel", "arbitrary")),
        name="proj",
    )(x2, g_mix, w_main, w_if, cos_t, sa_t, sb_t)


def _attn_kernel(q_ref, k_ref, v_ref, o_ref, lse_ref, stage_ref, q_rm, k_rm, v_rm, o_rm, *, dilation):
    S = q_ref.shape[1]
    d = dilation
    L = S // d
    per_class = L // ATTN_BLOCK
    nblk = S // ATTN_BLOCK
    h = pl.program_id(1)

    if d == 1:
        q_of = lambda sl: q_ref[0, sl, :]
        k_of = lambda sl: k_ref[0, sl, :]
        v_of = lambda sl: v_ref[0, sl, :]
    else:
        for src, dst in ((q_ref, q_rm), (k_ref, k_rm), (v_ref, v_rm)):
            stage_ref[...] = src[0].astype(F32)
            if d <= SUBLANES:
                for r in range(d):
                    dst[r * L:(r + 1) * L, :] = stage_ref[pl.ds(r, L, stride=d), :].astype(BF16)
            else:
                d0 = int(round(d ** 0.5))
                assert d0 * d0 == d and d0 <= SUBLANES
                for a in range(d0):
                    o_rm[a * (S // d0):(a + 1) * (S // d0), :] = stage_ref[pl.ds(a, S // d0, stride=d0), :]
                for a in range(d0):
                    for b in range(d0):
                        r = a + d0 * b
                        dst[r * L:(r + 1) * L, :] = o_rm[pl.ds(a * (S // d0) + b, L, stride=d0), :].astype(BF16)
        q_of = lambda sl: q_rm[sl, :]
        k_of = lambda sl: k_rm[sl, :]
        v_of = lambda sl: v_rm[sl, :]

    @pl.when(h == 0)
    def _():
        lse_ref[...] = jnp.zeros_like(lse_ref)

    row = lax.broadcasted_iota(jnp.int32, (ATTN_BLOCK, ATTN_BLOCK), 0)
    col = lax.broadcasted_iota(jnp.int32, (ATTN_BLOCK, ATTN_BLOCK), 1)
    bias_cur = jnp.where(col <= row, 0.0, NEG).astype(F32)
    bias_prev = jnp.where(col >= row, 0.0, NEG).astype(F32)
    my_lane = col == h
    scale = HEAD_DIM ** -0.5
    dn = (((1,), (1,)), ((), ()))
    for n in range(nblk):
        sl = slice(n * ATTN_BLOCK, (n + 1) * ATTN_BLOCK)
        has_prev = n % per_class != 0
        q = q_of(sl)
        s_c = lax.dot_general(q, k_of(sl), dn, preferred_element_type=F32) * scale + bias_cur
        m = jnp.max(s_c, axis=1, keepdims=True)
        if has_prev:
            sp = slice((n - 1) * ATTN_BLOCK, n * ATTN_BLOCK)
            s_p = lax.dot_general(q, k_of(sp), dn, preferred_element_type=F32) * scale + bias_prev
            m = jnp.maximum(m, jnp.max(s_p, axis=1, keepdims=True))
        p_c = jnp.exp(s_c - m)
        den = jnp.sum(p_c, axis=1, keepdims=True)
        acc = jnp.dot(p_c.astype(BF16), v_of(sl), preferred_element_type=F32)
        if has_prev:
            p_p = jnp.exp(s_p - m)
            den = den + jnp.sum(p_p, axis=1, keepdims=True)
            acc = acc + jnp.dot(p_p.astype(BF16), v_of(sp), preferred_element_type=F32)
        lse = m + jnp.log(den)
        start = n // per_class + d * (n % per_class) * ATTN_BLOCK
        rows = pl.ds(start, ATTN_BLOCK, stride=d) if d > 1 else sl
        o_rm[rows, :] = acc / den
        lse_ref[0, rows, :] = jnp.where(my_lane, lse, lse_ref[0, rows, :])
    o_ref[0] = o_rm[...].astype(o_ref.dtype)


def _attention(proj3, g):
    B, S, _ = proj3.shape
    Dh = HEAD_DIM

    def col(base):
        return pl.BlockSpec((1, S, Dh), lambda b, h: (b, 0, (base + g * ATTN_OUT) // Dh + h))

    return pl.pallas_call(
        functools.partial(_attn_kernel, dilation=GROUP_DILATION[g]),
        grid=(B, HEADS_PER_GROUP),
        in_specs=[col(COL_AQ), col(COL_AK), col(COL_AV)],
        out_specs=[pl.BlockSpec((1, S, Dh), lambda b, h: (b, 0, h)),
                   pl.BlockSpec((1, S, LANES), lambda b, h: (b, 0, 0))],
        out_shape=[jax.ShapeDtypeStruct((B, S, ATTN_OUT), BF16), jax.ShapeDtypeStruct((B, S, LANES), F32)],
        scratch_shapes=[pltpu.VMEM((S, Dh), F32), pltpu.VMEM((S, Dh), BF16), pltpu.VMEM((S, Dh), BF16),
                        pltpu.VMEM((S, Dh), BF16), pltpu.VMEM((S, Dh), F32)],
        compiler_params=_params(("parallel", "arbitrary")),
        name=f"attn{g}",
    )(proj3, proj3, proj3)


def _mlstm_kernel(bif_ref, mq_ref, mk_ref, mv_ref, mo_ref, g_ref, cwq_ref, cwk_ref, gn_ref, o_ref,
                  pad_ref, qs_ref, ks_ref, c_ref, n_ref, m_ref, hq_ref, hk_ref):
    SEG = mq_ref.shape[1]
    LC = MLSTM_CHUNK
    Dh = M_HEAD_DIM
    H = M_HEADS
    seg = pl.program_id(1)

    @pl.when(seg == 0)
    def _():
        c_ref[...] = jnp.zeros_like(c_ref)
        n_ref[...] = jnp.zeros_like(n_ref)
        m_ref[...] = jnp.zeros_like(m_ref)
        hq_ref[...] = jnp.zeros_like(hq_ref)
        hk_ref[...] = jnp.zeros_like(hk_ref)

    def conv_silu(src_ref, hist_ref, w_ref, cols, dst_ref, scale):
        pad_ref[0:SUBLANES, :] = hist_ref[:, cols]
        pad_ref[SUBLANES:SUBLANES + SEG, :] = src_ref[0, :, cols].astype(F32)
        hist_ref[:, cols] = pad_ref[SEG:SEG + SUBLANES, :]
        rb = 2 * LC
        for r0 in range(0, SEG, rb):
            acc = None
            for j in range(CONV_WIDTH):
                off = r0 + SUBLANES - (CONV_WIDTH - 1) + j
                term = w_ref[j:j + 1, cols] * pad_ref[off:off + rb, :]
                acc = term if acc is None else acc + term
            dst_ref[r0:r0 + rb, cols] = (acc * _sigmoid(acc) * scale).astype(dst_ref.dtype)

    for hd in range(H):
        cols = slice(hd * Dh, (hd + 1) * Dh)
        conv_silu(mq_ref, hq_ref, cwq_ref, cols, qs_ref, 1.0)
        conv_silu(mk_ref, hk_ref, cwk_ref, cols, ks_ref, Dh ** -0.5)

    row = lax.broadcasted_iota(jnp.int32, (LC, LC), 0)
    col = lax.broadcasted_iota(jnp.int32, (LC, LC), 1)
    causal = col <= row
    eye = col == row
    lane8 = lax.broadcasted_iota(jnp.int32, (SUBLANES, LC), 1)

    def head_chunk(hd, c, r0, m):
        cols = slice(hd * Dh, (hd + 1) * Dh)
        c_ref_h = c_ref.at[hd]
        gn = gn_ref[:, cols]
        q = qs_ref[pl.ds(r0, LC), cols]
        k = ks_ref[pl.ds(r0, LC), cols]
        v = mv_ref[0, pl.ds(r0, LC), cols]
        i_row = g_ref[0, hd, pl.ds(c, 1), :] + bif_ref[hd]
        f_row = g_ref[0, H + hd, pl.ds(c, 1), :] + bif_ref[H + hd]
        logf = jnp.minimum(f_row, 0.0) - jnp.log(1.0 + jnp.exp(-jnp.abs(f_row)))
        b8 = jnp.broadcast_to(logf, (SUBLANES, LC))
        s = 1
        while s < LC:
            b8 = b8 + jnp.where(lane8 >= s, pltpu.roll(b8, s, 1), 0.0)
            s *= 2
        b_row = b8[0:1, :]
        d_row = i_row - b_row
        b_col = jnp.sum(jnp.where(eye, b_row, 0.0), axis=1, keepdims=True)
        d_col = jnp.sum(jnp.where(eye, d_row, 0.0), axis=1, keepdims=True)
        dm = jnp.where(causal, b_col + d_row, NEG)
        inter = b_col + m
        m_t = jnp.maximum(inter, jnp.max(dm, axis=1, keepdims=True))
        w_intra = jnp.exp(dm - m_t)
        w_inter = jnp.exp(inter - m_t)
        kb = k.astype(BF16)
        a = lax.dot_general(q, kb, (((1,), (1,)), ((), ())), preferred_element_type=F32) * w_intra
        num = (jnp.dot(a.astype(BF16), v, preferred_element_type=F32)
               + w_inter * jnp.dot(q, c_ref_h[...].astype(BF16), preferred_element_type=F32))
        den = (jnp.sum(a, axis=1, keepdims=True)
               + w_inter * jnp.sum(q.astype(F32) * n_ref[:, cols], axis=1, keepdims=True))
        hh = num / jnp.maximum(jnp.abs(den), jnp.exp(-m_t))
        mu = jnp.mean(hh, axis=1, keepdims=True)
        xc = hh - mu
        var = jnp.mean(xc * xc, axis=1, keepdims=True)
        hn = xc * lax.rsqrt(var + LN_EPS) * gn
        gate = _sigmoid(mo_ref[0, pl.ds(r0, LC), cols].astype(F32))
        o_ref[0, pl.ds(r0, LC), cols] = (gate * hn).astype(o_ref.dtype)
        b_last = b_row[:, LC - 1:LC]
        m_new = jnp.maximum(b_last + m, jnp.max(b_last + d_row, axis=1, keepdims=True))
        decay = jnp.exp(b_last + m - m_new)
        kw = k * jnp.exp(b_last + d_col - m_new)
        c_ref_h[...] = decay * c_ref_h[...] + lax.dot_general(
            kw.astype(BF16), v, (((0,), (0,)), ((), ())), preferred_element_type=F32)
        n_ref[:, cols] = decay * n_ref[:, cols] + jnp.sum(kw, axis=0, keepdims=True)
        return m_new

    def chunk(c, ms):
        r0 = pl.multiple_of(c * LC, LC)
        return tuple(head_chunk(hd, c, r0, ms[hd]) for hd in range(H))

    ms = lax.fori_loop(0, SEG // LC, chunk, tuple(m_ref[:, hd:hd + 1] for hd in range(H)))
    for hd in range(H):
        m_ref[:, hd:hd + 1] = ms[hd]


def _mlstm(proj3, gates, conv_q, conv_k, b_if, g_norm):
    B, S, _ = proj3.shape
    W = M_WIDTH
    SEG = MLSTM_SEGMENT
    nseg = S // SEG
    nC = SEG // MLSTM_CHUNK

    def col(base):
        return pl.BlockSpec((1, SEG, W), lambda b, s, pre: (b, s, base // W))

    full = lambda a: pl.BlockSpec(a.shape, lambda b, s, pre: (0,) * a.ndim)
    grid_spec = pltpu.PrefetchScalarGridSpec(
        num_scalar_prefetch=1,
        grid=(B, nseg),
        in_specs=[
            col(COL_MQ), col(COL_MK), col(COL_MV), col(COL_MO),
            pl.BlockSpec((1, 2 * M_HEADS, nC, MLSTM_CHUNK), lambda b, s, pre: (b, 0, s, 0)),
            full(conv_q), full(conv_k), full(g_norm),
        ],
        out_specs=pl.BlockSpec((1, SEG, W), lambda b, s, pre: (b, s, 0)),
        scratch_shapes=[
            pltpu.VMEM((SEG + SUBLANES, M_HEAD_DIM), F32),
            pltpu.VMEM((SEG, W), BF16),
            pltpu.VMEM((SEG, W), F32),
            pltpu.VMEM((M_HEADS, M_HEAD_DIM, M_HEAD_DIM), F32),
            pltpu.VMEM((1, W), F32),
            pltpu.VMEM((1, LANES), F32),
            pltpu.VMEM((SUBLANES, W), F32),
            pltpu.VMEM((SUBLANES, W), F32),
        ],
    )
    return pl.pallas_call(
        _mlstm_kernel,
        grid_spec=grid_spec,
        out_shape=jax.ShapeDtypeStruct((B, S, W), BF16),
        compiler_params=_params(("parallel", "arbitrary")),
        name="mlstm",
    )(b_if, proj3, proj3, proj3, proj3, gates, conv_q, conv_k, g_norm)


def _post_kernel(x_ref, ao0_ref, ao1_ref, ao2_ref, lse0_ref, lse1_ref, lse2_ref, mem_ref, ga_ref, gm_ref,
                 wa_ref, wm_ref, wo_ref, gf_ref, wr_ref, wrl_ref, br_ref, h1_ref, xn_ref, idx_ref, gate_ref):
    tm = x_ref.shape[0]
    ao_refs = (ao0_ref, ao1_ref, ao2_ref)
    lses = (lse0_ref[...], lse1_ref[...], lse2_ref[...])
    heads = []
    for h in range(HEADS_PER_GROUP):
        ls = [l[:, h:h + 1] for l in lses]
        mx = jnp.maximum(jnp.maximum(ls[0], ls[1]), ls[2])
        es = [jnp.exp(l - mx) for l in ls]
        tot = es[0] + es[1] + es[2]
        acc = None
        for g in range(N_GROUPS):
            term = (es[g] / tot) * ao_refs[g][:, h * HEAD_DIM:(h + 1) * HEAD_DIM].astype(F32)
            acc = term if acc is None else acc + term
        heads.append(acc.astype(BF16))
    attn = jnp.concatenate(heads, axis=1)
    ya = jnp.dot(attn, wa_ref[...], preferred_element_type=F32)
    ym = jnp.dot(mem_ref[...], wm_ref[...], preferred_element_type=F32)
    merged = _sigmoid(ga_ref[...].astype(F32)) * ya + _sigmoid(gm_ref[...].astype(F32)) * ym
    h1 = x_ref[...] + jnp.dot(merged.astype(BF16), wo_ref[...], preferred_element_type=F32)
    h1_ref[...] = h1
    ms = jnp.mean(h1 * h1, axis=-1, keepdims=True)
    xn = h1 * lax.rsqrt(ms + RMS_EPS) * gf_ref[...]
    for s in range(SUBLANES):
        xn_ref[pl.ds(s, tm, stride=SUBLANES), :] = xn[:, s * LANES:(s + 1) * LANES]
    xn_hi = xn.astype(BF16)
    xn_lo = (xn - xn_hi.astype(F32)).astype(BF16)
    logits = (jnp.dot(xn_hi, wr_ref[...], preferred_element_type=F32)
              + jnp.dot(xn_hi, wrl_ref[...], preferred_element_type=F32)
              + jnp.dot(xn_lo, wr_ref[...], preferred_element_type=F32)) + br_ref[...]
    lane = lax.broadcasted_iota(jnp.int32, (tm, LANES), 1).astype(F32)
    vals = logits
    tops, idxs = [], []
    for _ in range(TOP_K):
        mx = jnp.max(vals, axis=1, keepdims=True)
        ix = jnp.min(jnp.where(vals == mx, lane, float(LANES)), axis=1, keepdims=True)
        tops.append(mx)
        idxs.append(ix)
        vals = jnp.where(lane == ix, NEG, vals)
    es = [jnp.exp(t - tops[0]) for t in tops]
    tot = es[0] + es[1] + es[2] + es[3]
    idx_out = jnp.zeros((tm, LANES), F32)
    gate_out = jnp.zeros((tm, LANES), F32)
    for kk in range(TOP_K):
        idx_out = jnp.where(lane == float(kk), idxs[kk], idx_out)
        gate_out = jnp.where(lane == float(kk), es[kk] / tot, gate_out)
    idx_ref[...] = idx_out.astype(jnp.int32)
    gate_ref[...] = gate_out


def _post(x2, aos, lses, mem, proj, w_a, w_m, w_o, g_ffn, w_r, w_r_lo, b_r):
    T, D = x2.shape
    tm = POST_TM
    row = lambda w: pl.BlockSpec((tm, w), lambda i: (i, 0))
    full = lambda a: pl.BlockSpec(a.shape, lambda i: (0,) * a.ndim)
    return pl.pallas_call(
        _post_kernel,
        grid=(T // tm,),
        in_specs=[
            row(D), row(ATTN_OUT), row(ATTN_OUT), row(ATTN_OUT), row(LANES), row(LANES), row(LANES), row(M_WIDTH),
            pl.BlockSpec((tm, D), lambda i: (i, COL_GA // D)),
            pl.BlockSpec((tm, D), lambda i: (i, COL_GM // D)),
            full(w_a), full(w_m), full(w_o), full(g_ffn), full(w_r), full(w_r_lo), full(b_r),
        ],
        out_specs=[row(D), pl.BlockSpec((tm * SUBLANES, LANES), lambda i: (i, 0)), row(LANES), row(LANES)],
        out_shape=[jax.ShapeDtypeStruct((T, D), F32), jax.ShapeDtypeStruct((T * SUBLANES, LANES), F32),
                   jax.ShapeDtypeStruct((T, LANES), jnp.int32), jax.ShapeDtypeStruct((T, LANES), F32)],
        compiler_params=_params(("parallel",)),
        name="post",
    )(x2, *aos, *lses, mem, proj, proj, w_a, w_m, w_o, g_ffn, w_r, w_r_lo, b_r)


def _expert_kernel(be_ref, ib_ref, lo_ref, hi_ref, nu_ref, idx_hbm, xn_hbm, wgu_ref, wd_ref, bgu_ref, bd_ref, y_hbm,
                   idx_smem, xbuf, ybuf, xb_ref, wgu_bf, wdx_bf, wd_tmp, isem, gsem, ssem):
    i = pl.program_id(0)
    n_idx = ib_ref.shape[0]
    n_used = nu_ref[0]
    BLK = MOE_BLOCK
    n_islots = idx_smem.shape[0]
    n_bufs = xbuf.shape[0]

    def tile(t):
        return pl.ds(pl.multiple_of(t * SUBLANES, SUBLANES), SUBLANES)

    def tiles(n):
        return pl.ds(0, pl.multiple_of(n * SUBLANES, SUBLANES))

    def idx_copy(blk):
        slot = blk % n_islots
        return pltpu.make_async_copy(idx_hbm.at[ib_ref[blk]], idx_smem.at[slot], isem.at[slot])

    def gather_start(blk):
        islot, bslot = blk % n_islots, blk % n_bufs
        for r in range(BLK):
            tok = idx_smem[islot, r]
            pltpu.make_async_copy(xn_hbm.at[tile(tok)], xbuf.at[bslot, pl.ds(r * SUBLANES, SUBLANES)],
                                  gsem.at[bslot]).start()

    def gather_wait(blk):
        bslot = blk % n_bufs
        pltpu.make_async_copy(xn_hbm.at[tiles(BLK)], xbuf.at[bslot], gsem.at[bslot]).wait()

    def row_range(blk):
        ok = blk >= 0
        b = jnp.maximum(blk, 0)
        return jnp.where(ok, lo_ref[b], 0), jnp.where(ok, hi_ref[b], 0)

    def scatter_start(blk):
        islot, bslot = blk % n_islots, blk % n_bufs
        lo, hi = row_range(blk)
        n = (hi - lo).astype(jnp.uint32)
        for r in range(BLK):
            @pl.when((r - lo).astype(jnp.uint32) < n)
            def _():
                dst = idx_smem[islot, BLK + r]
                pltpu.make_async_copy(ybuf.at[bslot, pl.ds(r * SUBLANES, SUBLANES)], y_hbm.at[tile(dst)],
                                      ssem.at[bslot]).start()

    def scatter_wait(blk):
        bslot = blk % n_bufs
        lo, hi = row_range(blk)
        n = hi - lo

        @pl.when(n > 0)
        def _():
            pltpu.make_async_copy(ybuf.at[bslot, tiles(n)], y_hbm.at[tiles(n)], ssem.at[bslot]).wait()

    @pl.when(i == 0)
    def _():
        idx_copy(0).start()
        idx_copy(1).start()
        idx_copy(2).start()
        idx_copy(0).wait()
        idx_copy(1).wait()
        gather_start(0)
        gather_start(1)
        wd_tmp[...] = jnp.zeros_like(wd_tmp)

    @pl.when(i + 3 < n_idx)
    def _():
        idx_copy(i + 3).start()

    @pl.when(i + 2 < n_idx)
    def _():
        idx_copy(i + 2).wait()

    @pl.when(jnp.logical_and(i < n_used, jnp.logical_or(i == 0, be_ref[i] != be_ref[jnp.maximum(i - 1, 0)])))
    def _():
        wgu_bf[...] = wgu_ref[0].astype(BF16)
        rows = wd_tmp.shape[1] // 2
        for c in range(wd_ref.shape[1] // rows):
            for s in range(wd_ref.shape[2] // LANES):
                wd_tmp.at[s][pl.ds(0, rows, stride=2), :] = wd_ref[0, c * rows:(c + 1) * rows,
                                                                   s * LANES:(s + 1) * LANES]
            for s in range(wd_ref.shape[2] // LANES):
                wdx_bf[2 * c * rows:2 * (c + 1) * rows, s * LANES:(s + 1) * LANES] = wd_tmp[s].astype(BF16)

    @pl.when(i < n_used)
    def _():
        gather_wait(i)
        scatter_wait(i - 3)

    @pl.when(i < n_used)
    def _():
        bslot = i % n_bufs
        for s in range(SUBLANES):
            xb_ref[:, s * LANES:(s + 1) * LANES] = xbuf[bslot, pl.ds(s, BLK, stride=SUBLANES), :].astype(BF16)
        gather_start(i + 2)
        scatter_start(i - 1)
        hg = jnp.dot(xb_ref[...], wgu_bf[...], preferred_element_type=F32) + bgu_ref[0]
        nxt = pltpu.roll(hg, hg.shape[1] - 1, 1)
        gate = jnp.minimum(hg, SWIGLU_LIMIT)
        up = jnp.clip(nxt, -SWIGLU_LIMIT, SWIGLU_LIMIT)
        act = (up + 1.0) * (gate * _sigmoid(SWIGLU_ALPHA * gate))
        y = jnp.dot(act.astype(BF16), wdx_bf[...], preferred_element_type=F32) + bd_ref[0]
        for s in range(SUBLANES):
            ybuf[bslot, pl.ds(s, BLK, stride=SUBLANES), :] = y[:, s * LANES:(s + 1) * LANES]

    @pl.when(i == n_used)
    def _():
        gather_wait(i)
        gather_wait(i + 1)
        scatter_start(i - 1)
        scatter_wait(i - 3)
        scatter_wait(i - 2)
        scatter_wait(i - 1)


def _experts(item_e, item_blk, item_lo, item_hi, n_used, idx, xn2, wgu, wd, bgu, bd, n_out_rows):
    nb = item_e.shape[0] - 1
    D = wd.shape[2]
    F2 = wgu.shape[2]
    emap = lambda i, be, ib, lo, hi, nu: (be[i], 0, 0)
    grid_spec = pltpu.PrefetchScalarGridSpec(
        num_scalar_prefetch=5,
        grid=(nb,),
        in_specs=[
            pl.BlockSpec(memory_space=pl.ANY),
            pl.BlockSpec(memory_space=pl.ANY),
            pl.BlockSpec((1, D, F2), emap),
            pl.BlockSpec((1, F2 // 2, D), emap),
            pl.BlockSpec((1, 1, F2), emap),
            pl.BlockSpec((1, 1, D), emap),
        ],
        out_specs=pl.BlockSpec(memory_space=pl.ANY),
        scratch_shapes=[
            pltpu.SMEM((8, 2 * MOE_BLOCK), jnp.int32),
            pltpu.VMEM((3, MOE_BLOCK * SUBLANES, LANES), F32),
            pltpu.VMEM((3, MOE_BLOCK * SUBLANES, LANES), F32),
            pltpu.VMEM((MOE_BLOCK, D), BF16),
            pltpu.VMEM((D, F2), BF16),
            pltpu.VMEM((F2, D), BF16),
            pltpu.VMEM((D // LANES, F2 // 4, LANES), F32),
            pltpu.SemaphoreType.DMA((8,)),
            pltpu.SemaphoreType.DMA((3,)),
            pltpu.SemaphoreType.DMA((3,)),
        ],
    )
    return pl.pallas_call(
        _expert_kernel,
        grid_spec=grid_spec,
        out_shape=jax.ShapeDtypeStruct((n_out_rows * SUBLANES, LANES), F32),
        compiler_params=_params(("arbitrary",)),
        name="experts",
    )(item_e, item_blk, item_lo, item_hi, n_used, idx, xn2, wgu, wd, bgu, bd)


def _final_kernel(h1_ref, y0_ref, y1_ref, y2_ref, y3_ref, gate_ref, g_ref, o_ref):
    tm = h1_ref.shape[0]
    gts = gate_ref[...]
    g4 = [gts[:, kk:kk + 1] for kk in range(TOP_K)]
    pieces = []
    ss = None
    for s in range(SUBLANES):
        h = h1_ref[:, s * LANES:(s + 1) * LANES]
        for kk, y_ref in enumerate((y0_ref, y1_ref, y2_ref, y3_ref)):
            h = h + g4[kk] * y_ref[pl.ds(s, tm, stride=SUBLANES), :]
        pieces.append(h)
        sq = jnp.sum(h * h, axis=-1, keepdims=True)
        ss = sq if ss is None else ss + sq
    inv = lax.rsqrt(ss / h1_ref.shape[1] + RMS_EPS)
    for s in range(SUBLANES):
        o_ref[:, s * LANES:(s + 1) * LANES] = pieces[s] * inv * g_ref[:, s * LANES:(s + 1) * LANES]


def _final(h1, y4, gates, g_final):
    T, D = h1.shape
    tm = FINAL_TM
    nt = T // tm
    yspec = lambda kk: pl.BlockSpec((tm * SUBLANES, LANES), lambda i: (kk * nt + i, 0))
    return pl.pallas_call(
        _final_kernel,
        grid=(nt,),
        in_specs=[pl.BlockSpec((tm, D), lambda i: (i, 0)), yspec(0), yspec(1), yspec(2), yspec(3),
                  pl.BlockSpec((tm, LANES), lambda i: (i, 0)), pl.BlockSpec((1, D), lambda i: (0, 0))],
        out_specs=pl.BlockSpec((tm, D), lambda i: (i, 0)),
        out_shape=jax.ShapeDtypeStruct((T, D), F32),
        compiler_params=_params(("parallel",)),
        name="final",
    )(h1, y4, y4, y4, y4, gates, g_final)


def _rope_tables(positions):
    half = ROPE_DIM // 2
    inv_freq = ROPE_THETA ** (-jnp.arange(half, dtype=F32) / half)
    ang = positions.astype(F32).reshape(-1)[:, None] * inv_freq
    cos, sin = jnp.cos(ang), jnp.sin(ang)
    T = ang.shape[0]
    cos_t = jnp.concatenate([cos, cos, jnp.ones((T, HEAD_DIM - ROPE_DIM), F32)], axis=1)
    sa_t = jnp.concatenate([-sin, jnp.zeros((T, HEAD_DIM - half), F32)], axis=1)
    sb_t = jnp.concatenate([jnp.zeros((T, half), F32), sin, jnp.zeros((T, HEAD_DIM - ROPE_DIM), F32)], axis=1)
    return cos_t, sa_t, sb_t


def _routing(top_idx, T):
    A = T * TOP_K
    BLK = MOE_BLOCK
    n_blocks = A // BLK
    n_items = n_blocks + N_EXPERTS
    flat_e = top_idx.reshape(A)
    row_a = (jnp.sort(flat_e * A + jnp.arange(A, dtype=jnp.int32)) % A).reshape(n_blocks, BLK)
    row_tok = row_a // TOP_K
    row_slot = (row_a % TOP_K) * T + row_tok
    idx = jnp.concatenate([row_tok, row_slot], axis=1)
    experts = jnp.arange(N_EXPERTS, dtype=jnp.int32)
    counts = jnp.sum((flat_e[None, :] == experts[:, None]).astype(jnp.int32), axis=1)
    ends = jnp.cumsum(counts)
    starts = ends - counts
    first_blk = starts // BLK
    per_expert = jnp.where(counts > 0, (ends - 1) // BLK - first_blk + 1, 0)
    item_ends = jnp.cumsum(per_expert)
    item_starts = item_ends - per_expert
    n_used = item_ends[-1].astype(jnp.int32).reshape(1)
    it = jnp.arange(n_items + 2, dtype=jnp.int32)
    item_e = jnp.minimum(jnp.sum((item_ends[None, :] <= it[:, None]).astype(jnp.int32), axis=1), N_EXPERTS - 1)
    item_blk = jnp.clip(first_blk[item_e] + it - item_starts[item_e], 0, n_blocks - 1)
    live = it < item_ends[-1]
    item_lo = jnp.where(live, jnp.clip(starts[item_e] - item_blk * BLK, 0, BLK), 0).astype(jnp.int32)
    item_hi = jnp.where(live, jnp.clip(ends[item_e] - item_blk * BLK, 0, BLK), 0).astype(jnp.int32)
    return item_e.astype(jnp.int32), item_blk.astype(jnp.int32), item_lo, item_hi, n_used, idx, A


def kernel(x, positions, g_mix, w_in, conv_qk, b_if, g_mlstm_norm, w_attn_out, w_mlstm_out, w_mix_out, g_ffn,
           w_router, b_router, w_gate_up, b_gate_up, w_down, b_down, g_final):
    B, S, D = x.shape
    T = B * S
    l = 0
    x2 = x.reshape(T, D)

    w = w_in[l]
    o = np.cumsum((0, ATTN_WIDTH, ATTN_WIDTH, ATTN_WIDTH, M_WIDTH, M_WIDTH, M_WIDTH, M_WIDTH, M_HEADS, M_HEADS,
                   D_MODEL, D_MODEL))
    seg = lambda a: w[:, int(o[a]):int(o[a + 1])]
    w_main = jnp.concatenate([seg(0), seg(1), seg(9), seg(10), seg(3), seg(4), seg(5), seg(6), seg(2)],
                             axis=1).astype(BF16)
    w_if = jnp.pad(w[:, int(o[7]):int(o[9])], ((0, 0), (0, LANES - 2 * M_HEADS))).astype(BF16)
    cos_t, sa_t, sb_t = _rope_tables(positions)

    proj, ifo = _proj(x2, g_mix[l][None, :], w_main, w_if, cos_t, sa_t, sb_t)

    proj3 = proj.reshape(B, S, N_MAIN)
    attn_out = [_attention(proj3, g) for g in range(N_GROUPS)]
    aos = [o.reshape(T, ATTN_OUT) for o, _ in attn_out]
    lses = [l.reshape(T, LANES) for _, l in attn_out]

    nC = S // MLSTM_CHUNK
    gates = ifo[:, :2 * M_HEADS].reshape(B, S, 2 * M_HEADS).transpose(0, 2, 1).reshape(B, 2 * M_HEADS, nC, MLSTM_CHUNK)
    mem = _mlstm(proj3, gates, conv_qk[l][:, :M_WIDTH], conv_qk[l][:, M_WIDTH:], b_if[l],
                 g_mlstm_norm[l][None, :])

    w_r32 = jnp.pad(w_router[l], ((0, 0), (0, LANES - N_EXPERTS)))
    w_r = w_r32.astype(BF16)
    w_r_lo = (w_r32 - w_r.astype(F32)).astype(BF16)
    b_r = jnp.concatenate([b_router[l], jnp.full((LANES - N_EXPERTS,), NEG, F32)])[None, :]
    h1, xn2, top_idx, gates4 = _post(x2, aos, lses, mem.reshape(T, M_WIDTH), proj, w_attn_out[l].astype(BF16),
                                     w_mlstm_out[l].astype(BF16), w_mix_out[l].astype(BF16), g_ffn[l][None, :],
                                     w_r, w_r_lo, b_r)

    item_e, item_blk, item_lo, item_hi, n_used, idx, n_out_rows = _routing(top_idx[:, :TOP_K], T)
    y4 = _experts(item_e, item_blk, item_lo, item_hi, n_used, idx, xn2, w_gate_up[l], w_down[l], b_gate_up[l][:, None, :],
                  b_down[l][:, None, :], n_out_rows)

    out = _final(h1, y4, gates4, g_final[None, :])
    return out.reshape(B, S, D)
```

```python
import functools

import jax
import jax.numpy as jnp
import numpy as np
from jax import lax
from jax.experimental import pallas as pl
from jax.experimental.pallas import tpu as pltpu

F32 = jnp.float32
BF16 = jnp.bfloat16

D_MODEL = 1024
N_GROUPS = 3
GROUP_DILATION = (1, 4, 16)
HEADS_PER_GROUP = 4
HEAD_DIM = 128
ATTN_WIDTH = N_GROUPS * HEADS_PER_GROUP * HEAD_DIM
ATTN_OUT = HEADS_PER_GROUP * HEAD_DIM
ATTN_BLOCK = 128
ROPE_THETA = 500000.0
ROPE_DIM = HEAD_DIM // 4
M_HEADS = 4
M_WIDTH = D_MODEL
M_HEAD_DIM = M_WIDTH // M_HEADS
CONV_WIDTH = 4
N_EXPERTS = 32
TOP_K = 4
D_FF = D_MODEL
SWIGLU_LIMIT = 7.0
SWIGLU_ALPHA = 1.702
RMS_EPS = 1e-5
LN_EPS = 1e-5

LANES = 128
SUBLANES = 8
VMEM_LIMIT = 56 * 1024 * 1024

PROJ_TM = 1024
PROJ_TN = 1536
MLSTM_CHUNK = 128
MLSTM_SEGMENT = 1024
POST_TM = 512
MOE_BLOCK = 256
FINAL_TM = 512
NEG = -1e30

COL_AQ = 0
COL_AK = ATTN_WIDTH
COL_GA = 2 * ATTN_WIDTH
COL_GM = COL_GA + D_MODEL
COL_MQ = COL_GM + D_MODEL
COL_MK = COL_MQ + M_WIDTH
COL_MV = COL_MK + M_WIDTH
COL_MO = COL_MV + M_WIDTH
COL_AV = COL_MO + M_WIDTH
N_MAIN = COL_AV + ATTN_WIDTH


def _sigmoid(x):
    return 0.5 * jnp.tanh(0.5 * x) + 0.5


def _params(sem):
    return pltpu.CompilerParams(dimension_semantics=sem, vmem_limit_bytes=VMEM_LIMIT)


def _proj_kernel(x_ref, g_ref, w_ref, wif_ref, cos_ref, sa_ref, sb_ref, o_ref, if_ref, xn_ref, *, rope_tiles):
    j = pl.program_id(1)

    @pl.when(j == 0)
    def _():
        x = x_ref[...]
        ms = jnp.mean(x * x, axis=-1, keepdims=True)
        xn = (x * lax.rsqrt(ms + RMS_EPS) * g_ref[...]).astype(BF16)
        xn_ref[...] = xn
        if_ref[...] = jnp.dot(xn, wif_ref[...], preferred_element_type=F32)

    acc = jnp.dot(xn_ref[...], w_ref[...], preferred_element_type=F32)

    @pl.when(j < rope_tiles)
    def _():
        c = cos_ref[...]
        sa = sa_ref[...]
        sb = sb_ref[...]
        for h in range(acc.shape[1] // HEAD_DIM):
            a = acc[:, h * HEAD_DIM:(h + 1) * HEAD_DIM]
            y = a * c + pltpu.roll(a, HEAD_DIM - ROPE_DIM // 2, 1) * sa + pltpu.roll(a, ROPE_DIM // 2, 1) * sb
            o_ref[:, h * HEAD_DIM:(h + 1) * HEAD_DIM] = y.astype(o_ref.dtype)

    @pl.when(j >= rope_tiles)
    def _():
        o_ref[...] = acc.astype(o_ref.dtype)


def _proj(x2, g_mix, w_main, w_if, cos_t, sa_t, sb_t):
    T, D = x2.shape
    N = w_main.shape[1]
    tm, tn = PROJ_TM, PROJ_TN
    grid = (T // tm, N // tn)
    return pl.pallas_call(
        functools.partial(_proj_kernel, rope_tiles=(2 * ATTN_WIDTH) // tn),
        grid=grid,
        in_specs=[
            pl.BlockSpec((tm, D), lambda i, j: (i, 0)),
            pl.BlockSpec((1, D), lambda i, j: (0, 0)),
            pl.BlockSpec((D, tn), lambda i, j: (0, j)),
            pl.BlockSpec((D, LANES), lambda i, j: (0, 0)),
            pl.BlockSpec((tm, LANES), lambda i, j: (i, 0)),
            pl.BlockSpec((tm, LANES), lambda i, j: (i, 0)),
            pl.BlockSpec((tm, LANES), lambda i, j: (i, 0)),
        ],
        out_specs=[
            pl.BlockSpec((tm, tn), lambda i, j: (i, j)),
            pl.BlockSpec((tm, LANES), lambda i, j: (i, 0)),
        ],
        out_shape=[jax.ShapeDtypeStruct((T, N), BF16), jax.ShapeDtypeStruct((T, LANES), F32)],
        scratch_shapes=[pltpu.VMEM((tm, D), BF16)],
        compiler_params=_params(("parallel", "arbitrary")),
        name="proj",
    )(x2, g_mix, w_main, w_if, cos_t, sa_t, sb_t)


def _attn_kernel(q_ref, k_ref, v_ref, o_ref, lse_ref, stage_ref, q_rm, k_rm, v_rm, o_rm, *, dilation):
    S = q_ref.shape[1]
    d = dilation
    L = S // d
    per_class = L // ATTN_BLOCK
    nblk = S // ATTN_BLOCK
    h = pl.program_id(1)

    if d == 1:
        q_of = lambda sl: q_ref[0, sl, :]
        k_of = lambda sl: k_ref[0, sl, :]
        v_of = lambda sl: v_ref[0, sl, :]
    else:
        for src, dst in ((q_ref, q_rm), (k_ref, k_rm), (v_ref, v_rm)):
            stage_ref[...] = src[0].astype(F32)
            if d <= SUBLANES:
                for r in range(d):
                    dst[r * L:(r + 1) * L, :] = stage_ref[pl.ds(r, L, stride=d), :].astype(BF16)
            else:
                d0 = int(round(d ** 0.5))
                assert d0 * d0 == d and d0 <= SUBLANES
                for a in range(d0):
                    o_rm[a * (S // d0):(a + 1) * (S // d0), :] = stage_ref[pl.ds(a, S // d0, stride=d0), :]
                for a in range(d0):
                    for b in range(d0):
                        r = a + d0 * b
                        dst[r * L:(r + 1) * L, :] = o_rm[pl.ds(a * (S // d0) + b, L, stride=d0), :].astype(BF16)
        q_of = lambda sl: q_rm[sl, :]
        k_of = lambda sl: k_rm[sl, :]
        v_of = lambda sl: v_rm[sl, :]

    @pl.when(h == 0)
    def _():
        lse_ref[...] = jnp.zeros_like(lse_ref)

    row = lax.broadcasted_iota(jnp.int32, (ATTN_BLOCK, ATTN_BLOCK), 0)
    col = lax.broadcasted_iota(jnp.int32, (ATTN_BLOCK, ATTN_BLOCK), 1)
    bias_cur = jnp.where(col <= row, 0.0, NEG).astype(F32)
    bias_prev = jnp.where(col >= row, 0.0, NEG).astype(F32)
    my_lane = col == h
    scale = HEAD_DIM ** -0.5
    dn = (((1,), (1,)), ((), ()))
    for n in range(nblk):
        sl = slice(n * ATTN_BLOCK, (n + 1) * ATTN_BLOCK)
        has_prev = n % per_class != 0
        q = q_of(sl)
        s_c = lax.dot_general(q, k_of(sl), dn, preferred_element_type=F32) * scale + bias_cur
        m = jnp.max(s_c, axis=1, keepdims=True)
        if has_prev:
            sp = slice((n - 1) * ATTN_BLOCK, n * ATTN_BLOCK)
            s_p = lax.dot_general(q, k_of(sp), dn, preferred_element_type=F32) * scale + bias_prev
            m = jnp.maximum(m, jnp.max(s_p, axis=1, keepdims=True))
        p_c = jnp.exp(s_c - m)
        den = jnp.sum(p_c, axis=1, keepdims=True)
        acc = jnp.dot(p_c.astype(BF16), v_of(sl), preferred_element_type=F32)
        if has_prev:
            p_p = jnp.exp(s_p - m)
            den = den + jnp.sum(p_p, axis=1, keepdims=True)
            acc = acc + jnp.dot(p_p.astype(BF16), v_of(sp), preferred_element_type=F32)
        lse = m + jnp.log(den)
        start = n // per_class + d * (n % per_class) * ATTN_BLOCK
        rows = pl.ds(start, ATTN_BLOCK, stride=d) if d > 1 else sl
        o_rm[rows, :] = acc / den
        lse_ref[0, rows, :] = jnp.where(my_lane, lse, lse_ref[0, rows, :])
    o_ref[0] = o_rm[...].astype(o_ref.dtype)


def _attention(proj3, g):
    B, S, _ = proj3.shape
    Dh = HEAD_DIM

    def col(base):
        return pl.BlockSpec((1, S, Dh), lambda b, h: (b, 0, (base + g * ATTN_OUT) // Dh + h))

    return pl.pallas_call(
        functools.partial(_attn_kernel, dilation=GROUP_DILATION[g]),
        grid=(B, HEADS_PER_GROUP),
        in_specs=[col(COL_AQ), col(COL_AK), col(COL_AV)],
        out_specs=[pl.BlockSpec((1, S, Dh), lambda b, h: (b, 0, h)),
                   pl.BlockSpec((1, S, LANES), lambda b, h: (b, 0, 0))],
        out_shape=[jax.ShapeDtypeStruct((B, S, ATTN_OUT), BF16), jax.ShapeDtypeStruct((B, S, LANES), F32)],
        scratch_shapes=[pltpu.VMEM((S, Dh), F32), pltpu.VMEM((S, Dh), BF16), pltpu.VMEM((S, Dh), BF16),
                        pltpu.VMEM((S, Dh), BF16), pltpu.VMEM((S, Dh), F32)],
        compiler_params=_params(("parallel", "arbitrary")),
        name=f"attn{g}",
    )(proj3, proj3, proj3)


def _mlstm_kernel(bif_ref, mq_ref, mk_ref, mv_ref, mo_ref, g_ref, cwq_ref, cwk_ref, gn_ref, o_ref,
                  pad_ref, qs_ref, ks_ref, c_ref, n_ref, m_ref, hq_ref, hk_ref):
    SEG = mq_ref.shape[1]
    LC = MLSTM_CHUNK
    Dh = M_HEAD_DIM
    H = M_HEADS
    seg = pl.program_id(1)

    @pl.when(seg == 0)
    def _():
        c_ref[...] = jnp.zeros_like(c_ref)
        n_ref[...] = jnp.zeros_like(n_ref)
        m_ref[...] = jnp.zeros_like(m_ref)
        hq_ref[...] = jnp.zeros_like(hq_ref)
        hk_ref[...] = jnp.zeros_like(hk_ref)

    def conv_silu(src_ref, hist_ref, w_ref, cols, dst_ref, scale):
        pad_ref[0:SUBLANES, :] = hist_ref[:, cols]
        pad_ref[SUBLANES:SUBLANES + SEG, :] = src_ref[0, :, cols].astype(F32)
        hist_ref[:, cols] = pad_ref[SEG:SEG + SUBLANES, :]
        rb = 2 * LC
        for r0 in range(0, SEG, rb):
            acc = None
            for j in range(CONV_WIDTH):
                off = r0 + SUBLANES - (CONV_WIDTH - 1) + j
                term = w_ref[j:j + 1, cols] * pad_ref[off:off + rb, :]
                acc = term if acc is None else acc + term
            dst_ref[r0:r0 + rb, cols] = (acc * _sigmoid(acc) * scale).astype(dst_ref.dtype)

    for hd in range(H):
        cols = slice(hd * Dh, (hd + 1) * Dh)
        conv_silu(mq_ref, hq_ref, cwq_ref, cols, qs_ref, 1.0)
        conv_silu(mk_ref, hk_ref, cwk_ref, cols, ks_ref, Dh ** -0.5)

    row = lax.broadcasted_iota(jnp.int32, (LC, LC), 0)
    col = lax.broadcasted_iota(jnp.int32, (LC, LC), 1)
    causal = col <= row
    eye = col == row
    lane8 = lax.broadcasted_iota(jnp.int32, (SUBLANES, LC), 1)

    def head_chunk(hd, c, r0, m):
        cols = slice(hd * Dh, (hd + 1) * Dh)
        c_ref_h = c_ref.at[hd]
        gn = gn_ref[:, cols]
        q = qs_ref[pl.ds(r0, LC), cols]
        k = ks_ref[pl.ds(r0, LC), cols]
        v = mv_ref[0, pl.ds(r0, LC), cols]
        i_row = g_ref[0, hd, pl.ds(c, 1), :] + bif_ref[hd]
        f_row = g_ref[0, H + hd, pl.ds(c, 1), :] + bif_ref[H + hd]
        logf = jnp.minimum(f_row, 0.0) - jnp.log(1.0 + jnp.exp(-jnp.abs(f_row)))
        b8 = jnp.broadcast_to(logf, (SUBLANES, LC))
        s = 1
        while s < LC:
            b8 = b8 + jnp.where(lane8 >= s, pltpu.roll(b8, s, 1), 0.0)
            s *= 2
        b_row = b8[0:1, :]
        d_row = i_row - b_row
        b_col = jnp.sum(jnp.where(eye, b_row, 0.0), axis=1, keepdims=True)
        d_col = jnp.sum(jnp.where(eye, d_row, 0.0), axis=1, keepdims=True)
        dm = jnp.where(causal, b_col + d_row, NEG)
        inter = b_col + m
        m_t = jnp.maximum(inter, jnp.max(dm, axis=1, keepdims=True))
        w_intra = jnp.exp(dm - m_t)
        w_inter = jnp.exp(inter - m_t)
        kb = k.astype(BF16)
        a = lax.dot_general(q, kb, (((1,), (1,)), ((), ())), preferred_element_type=F32) * w_intra
        num = (jnp.dot(a.astype(BF16), v, preferred_element_type=F32)
               + w_inter * jnp.dot(q, c_ref_h[...].astype(BF16), preferred_element_type=F32))
        den = (jnp.sum(a, axis=1, keepdims=True)
               + w_inter * jnp.sum(q.astype(F32) * n_ref[:, cols], axis=1, keepdims=True))
        hh = num / jnp.maximum(jnp.abs(den), jnp.exp(-m_t))
        mu = jnp.mean(hh, axis=1, keepdims=True)
        xc = hh - mu
        var = jnp.mean(xc * xc, axis=1, keepdims=True)
        hn = xc * lax.rsqrt(var + LN_EPS) * gn
        gate = _sigmoid(mo_ref[0, pl.ds(r0, LC), cols].astype(F32))
        o_ref[0, pl.ds(r0, LC), cols] = (gate * hn).astype(o_ref.dtype)
        b_last = b_row[:, LC - 1:LC]
        m_new = jnp.maximum(b_last + m, jnp.max(b_last + d_row, axis=1, keepdims=True))
        decay = jnp.exp(b_last + m - m_new)
        kw = k * jnp.exp(b_last + d_col - m_new)
        c_ref_h[...] = decay * c_ref_h[...] + lax.dot_general(
            kw.astype(BF16), v, (((0,), (0,)), ((), ())), preferred_element_type=F32)
        n_ref[:, cols] = decay * n_ref[:, cols] + jnp.sum(kw, axis=0, keepdims=True)
        return m_new

    def chunk(c, ms):
        r0 = pl.multiple_of(c * LC, LC)
        return tuple(head_chunk(hd, c, r0, ms[hd]) for hd in range(H))

    ms = lax.fori_loop(0, SEG // LC, chunk, tuple(m_ref[:, hd:hd + 1] for hd in range(H)))
    for hd in range(H):
        m_ref[:, hd:hd + 1] = ms[hd]


def _mlstm(proj3, gates, conv_q, conv_k, b_if, g_norm):
    B, S, _ = proj3.shape
    W = M_WIDTH
    SEG = MLSTM_SEGMENT
    nseg = S // SEG
    nC = SEG // MLSTM_CHUNK

    def col(base):
        return pl.BlockSpec((1, SEG, W), lambda b, s, pre: (b, s, base // W))

    full = lambda a: pl.BlockSpec(a.shape, lambda b, s, pre: (0,) * a.ndim)
    grid_spec = pltpu.PrefetchScalarGridSpec(
        num_scalar_prefetch=1,
        grid=(B, nseg),
        in_specs=[
            col(COL_MQ), col(COL_MK), col(COL_MV), col(COL_MO),
            pl.BlockSpec((1, 2 * M_HEADS, nC, MLSTM_CHUNK), lambda b, s, pre: (b, 0, s, 0)),
            full(conv_q), full(conv_k), full(g_norm),
        ],
        out_specs=pl.BlockSpec((1, SEG, W), lambda b, s, pre: (b, s, 0)),
        scratch_shapes=[
            pltpu.VMEM((SEG + SUBLANES, M_HEAD_DIM), F32),
            pltpu.VMEM((SEG, W), BF16),
            pltpu.VMEM((SEG, W), F32),
            pltpu.VMEM((M_HEADS, M_HEAD_DIM, M_HEAD_DIM), F32),
            pltpu.VMEM((1, W), F32),
            pltpu.VMEM((1, LANES), F32),
            pltpu.VMEM((SUBLANES, W), F32),
            pltpu.VMEM((SUBLANES, W), F32),
        ],
    )
    return pl.pallas_call(
        _mlstm_kernel,
        grid_spec=grid_spec,
        out_shape=jax.ShapeDtypeStruct((B, S, W), BF16),
        compiler_params=_params(("parallel", "arbitrary")),
        name="mlstm",
    )(b_if, proj3, proj3, proj3, proj3, gates, conv_q, conv_k, g_norm)


def _post_kernel(x_ref, ao0_ref, ao1_ref, ao2_ref, lse0_ref, lse1_ref, lse2_ref, mem_ref, ga_ref, gm_ref,
                 wa_ref, wm_ref, wo_ref, gf_ref, wr_ref, wrl_ref, br_ref, h1_ref, xn_ref, idx_ref, gate_ref):
    tm = x_ref.shape[0]
    ao_refs = (ao0_ref, ao1_ref, ao2_ref)
    lses = (lse0_ref[...], lse1_ref[...], lse2_ref[...])
    heads = []
    for h in range(HEADS_PER_GROUP):
        ls = [l[:, h:h + 1] for l in lses]
        mx = jnp.maximum(jnp.maximum(ls[0], ls[1]), ls[2])
        es = [jnp.exp(l - mx) for l in ls]
        tot = es[0] + es[1] + es[2]
        acc = None
        for g in range(N_GROUPS):
            term = (es[g] / tot) * ao_refs[g][:, h * HEAD_DIM:(h + 1) * HEAD_DIM].astype(F32)
            acc = term if acc is None else acc + term
        heads.append(acc.astype(BF16))
    attn = jnp.concatenate(heads, axis=1)
    ya = jnp.dot(attn, wa_ref[...], preferred_element_type=F32)
    ym = jnp.dot(mem_ref[...], wm_ref[...], preferred_element_type=F32)
    merged = _sigmoid(ga_ref[...].astype(F32)) * ya + _sigmoid(gm_ref[...].astype(F32)) * ym
    h1 = x_ref[...] + jnp.dot(merged.astype(BF16), wo_ref[...], preferred_element_type=F32)
    h1_ref[...] = h1
    ms = jnp.mean(h1 * h1, axis=-1, keepdims=True)
    xn = h1 * lax.rsqrt(ms + RMS_EPS) * gf_ref[...]
    for s in range(SUBLANES):
        xn_ref[pl.ds(s, tm, stride=SUBLANES), :] = xn[:, s * LANES:(s + 1) * LANES]
    xn_hi = xn.astype(BF16)
    xn_lo = (xn - xn_hi.astype(F32)).astype(BF16)
    logits = (jnp.dot(xn_hi, wr_ref[...], preferred_element_type=F32)
              + jnp.dot(xn_hi, wrl_ref[...], preferred_element_type=F32)
              + jnp.dot(xn_lo, wr_ref[...], preferred_element_type=F32)) + br_ref[...]
    lane = lax.broadcasted_iota(jnp.int32, (tm, LANES), 1).astype(F32)
    vals = logits
    tops, idxs = [], []
    for _ in range(TOP_K):
        mx = jnp.max(vals, axis=1, keepdims=True)
        ix = jnp.min(jnp.where(vals == mx, lane, float(LANES)), axis=1, keepdims=True)
        tops.append(mx)
        idxs.append(ix)
        vals = jnp.where(lane == ix, NEG, vals)
    es = [jnp.exp(t - tops[0]) for t in tops]
    tot = es[0] + es[1] + es[2] + es[3]
    idx_out = jnp.zeros((tm, LANES), F32)
    gate_out = jnp.zeros((tm, LANES), F32)
    for kk in range(TOP_K):
        idx_out = jnp.where(lane == float(kk), idxs[kk], idx_out)
        gate_out = jnp.where(lane == float(kk), es[kk] / tot, gate_out)
    idx_ref[...] = idx_out.astype(jnp.int32)
    gate_ref[...] = gate_out


def _post(x2, aos, lses, mem, proj, w_a, w_m, w_o, g_ffn, w_r, w_r_lo, b_r):
    T, D = x2.shape
    tm = POST_TM
    row = lambda w: pl.BlockSpec((tm, w), lambda i: (i, 0))
    full = lambda a: pl.BlockSpec(a.shape, lambda i: (0,) * a.ndim)
    return pl.pallas_call(
        _post_kernel,
        grid=(T // tm,),
        in_specs=[
            row(D), row(ATTN_OUT), row(ATTN_OUT), row(ATTN_OUT), row(LANES), row(LANES), row(LANES), row(M_WIDTH),
            pl.BlockSpec((tm, D), lambda i: (i, COL_GA // D)),
            pl.BlockSpec((tm, D), lambda i: (i, COL_GM // D)),
            full(w_a), full(w_m), full(w_o), full(g_ffn), full(w_r), full(w_r_lo), full(b_r),
        ],
        out_specs=[row(D), pl.BlockSpec((tm * SUBLANES, LANES), lambda i: (i, 0)), row(LANES), row(LANES)],
        out_shape=[jax.ShapeDtypeStruct((T, D), F32), jax.ShapeDtypeStruct((T * SUBLANES, LANES), F32),
                   jax.ShapeDtypeStruct((T, LANES), jnp.int32), jax.ShapeDtypeStruct((T, LANES), F32)],
        compiler_params=_params(("parallel",)),
        name="post",
    )(x2, *aos, *lses, mem, proj, proj, w_a, w_m, w_o, g_ffn, w_r, w_r_lo, b_r)


def _expert_kernel(be_ref, ib_ref, lo_ref, hi_ref, nu_ref, idx_hbm, xn_hbm, wgu_ref, wd_ref, bgu_ref, bd_ref, perm_ref,
                   y_hbm, idx_smem, xbuf, ybuf, xb_ref, wgu_bf, wd_bf, isem, gsem, ssem):
    i = pl.program_id(0)
    n_idx = ib_ref.shape[0]
    n_used = nu_ref[0]
    BLK = MOE_BLOCK
    n_islots = idx_smem.shape[0]
    n_bufs = xbuf.shape[0]

    def tile(t):
        return pl.ds(pl.multiple_of(t * SUBLANES, SUBLANES), SUBLANES)

    def tiles(n):
        return pl.ds(0, pl.multiple_of(n * SUBLANES, SUBLANES))

    def idx_copy(blk):
        slot = blk % n_islots
        return pltpu.make_async_copy(idx_hbm.at[ib_ref[blk]], idx_smem.at[slot], isem.at[slot])

    def gather_start(blk):
        islot, bslot = blk % n_islots, blk % n_bufs
        for r in range(BLK):
            tok = idx_smem[islot, r]
            pltpu.make_async_copy(xn_hbm.at[tile(tok)], xbuf.at[bslot, pl.ds(r * SUBLANES, SUBLANES)],
                                  gsem.at[bslot]).start()

    def gather_wait(blk):
        bslot = blk % n_bufs
        pltpu.make_async_copy(xn_hbm.at[tiles(BLK)], xbuf.at[bslot], gsem.at[bslot]).wait()

    def row_range(blk):
        ok = blk >= 0
        b = jnp.maximum(blk, 0)
        return jnp.where(ok, lo_ref[b], 0), jnp.where(ok, hi_ref[b], 0)

    def scatter_start(blk):
        islot, bslot = blk % n_islots, blk % n_bufs
        lo, hi = row_range(blk)
        n = (hi - lo).astype(jnp.uint32)
        for r in range(BLK):
            @pl.when((r - lo).astype(jnp.uint32) < n)
            def _():
                dst = idx_smem[islot, BLK + r]
                pltpu.make_async_copy(ybuf.at[bslot, pl.ds(r * SUBLANES, SUBLANES)], y_hbm.at[tile(dst)],
                                      ssem.at[bslot]).start()

    def scatter_wait(blk):
        bslot = blk % n_bufs
        lo, hi = row_range(blk)
        n = hi - lo

        @pl.when(n > 0)
        def _():
            pltpu.make_async_copy(ybuf.at[bslot, tiles(n)], y_hbm.at[tiles(n)], ssem.at[bslot]).wait()

    @pl.when(i == 0)
    def _():
        idx_copy(0).start()
        idx_copy(1).start()
        idx_copy(2).start()
        idx_copy(0).wait()
        idx_copy(1).wait()
        gather_start(0)
        gather_start(1)

    @pl.when(i + 3 < n_idx)
    def _():
        idx_copy(i + 3).start()

    @pl.when(i + 2 < n_idx)
    def _():
        idx_copy(i + 2).wait()

    @pl.when(jnp.logical_and(i < n_used, jnp.logical_or(i == 0, be_ref[i] != be_ref[jnp.maximum(i - 1, 0)])))
    def _():
        pw = perm_ref.shape[0]
        for c in range(wgu_ref.shape[2] // pw):
            cols = slice(c * pw, (c + 1) * pw)
            wgu_bf[:, cols] = jnp.dot(wgu_ref[0, :, cols].astype(BF16), perm_ref[...],
                                      preferred_element_type=F32).astype(BF16)
        wd_bf[...] = wd_ref[0].astype(BF16)

    @pl.when(i < n_used)
    def _():
        gather_wait(i)
        scatter_wait(i - 3)

    @pl.when(i < n_used)
    def _():
        bslot = i % n_bufs
        for s in range(SUBLANES):
            xb_ref[:, s * LANES:(s + 1) * LANES] = xbuf[bslot, pl.ds(s, BLK, stride=SUBLANES), :].astype(BF16)
        gather_start(i + 2)
        scatter_start(i - 1)
        hg = jnp.dot(xb_ref[...], wgu_bf[...], preferred_element_type=F32) + bgu_ref[0]
        acts = []
        for c in range(hg.shape[1] // (2 * LANES)):
            gate = jnp.minimum(hg[:, 2 * c * LANES:(2 * c + 1) * LANES], SWIGLU_LIMIT)
            up = jnp.clip(hg[:, (2 * c + 1) * LANES:(2 * c + 2) * LANES], -SWIGLU_LIMIT, SWIGLU_LIMIT)
            acts.append(((up + 1.0) * (gate * _sigmoid(SWIGLU_ALPHA * gate))).astype(BF16))
        act = jnp.concatenate(acts, axis=1)
        y = jnp.dot(act, wd_bf[...], preferred_element_type=F32) + bd_ref[0]
        for s in range(SUBLANES):
            ybuf[bslot, pl.ds(s, BLK, stride=SUBLANES), :] = y[:, s * LANES:(s + 1) * LANES]

    @pl.when(i == n_used)
    def _():
        gather_wait(i)
        gather_wait(i + 1)
        scatter_start(i - 1)
        scatter_wait(i - 3)
        scatter_wait(i - 2)
        scatter_wait(i - 1)


def _experts(item_e, item_blk, item_lo, item_hi, n_used, idx, xn2, wgu, wd, bgu, bd, perm, n_out_rows):
    nb = item_e.shape[0] - 1
    D = wd.shape[2]
    F2 = wgu.shape[2]
    emap = lambda i, be, ib, lo, hi, nu: (be[i], 0, 0)
    grid_spec = pltpu.PrefetchScalarGridSpec(
        num_scalar_prefetch=5,
        grid=(nb,),
        in_specs=[
            pl.BlockSpec(memory_space=pl.ANY),
            pl.BlockSpec(memory_space=pl.ANY),
            pl.BlockSpec((1, D, F2), emap),
            pl.BlockSpec((1, F2 // 2, D), emap),
            pl.BlockSpec((1, 1, F2), emap),
            pl.BlockSpec((1, 1, D), emap),
            pl.BlockSpec(perm.shape, lambda i, be, ib, lo, hi, nu: (0, 0)),
        ],
        out_specs=pl.BlockSpec(memory_space=pl.ANY),
        scratch_shapes=[
            pltpu.SMEM((8, 2 * MOE_BLOCK), jnp.int32),
            pltpu.VMEM((3, MOE_BLOCK * SUBLANES, LANES), F32),
            pltpu.VMEM((3, MOE_BLOCK * SUBLANES, LANES), F32),
            pltpu.VMEM((MOE_BLOCK, D), BF16),
            pltpu.VMEM((D, F2), BF16),
            pltpu.VMEM((F2 // 2, D), BF16),
            pltpu.SemaphoreType.DMA((8,)),
            pltpu.SemaphoreType.DMA((3,)),
            pltpu.SemaphoreType.DMA((3,)),
        ],
    )
    return pl.pallas_call(
        _expert_kernel,
        grid_spec=grid_spec,
        out_shape=jax.ShapeDtypeStruct((n_out_rows * SUBLANES, LANES), F32),
        compiler_params=_params(("arbitrary",)),
        name="experts",
    )(item_e, item_blk, item_lo, item_hi, n_used, idx, xn2, wgu, wd, bgu, bd, perm)


def _final_kernel(h1_ref, y0_ref, y1_ref, y2_ref, y3_ref, gate_ref, g_ref, o_ref):
    tm = h1_ref.shape[0]
    gts = gate_ref[...]
    g4 = [gts[:, kk:kk + 1] for kk in range(TOP_K)]
    pieces = []
    ss = None
    for s in range(SUBLANES):
        h = h1_ref[:, s * LANES:(s + 1) * LANES]
        for kk, y_ref in enumerate((y0_ref, y1_ref, y2_ref, y3_ref)):
            h = h + g4[kk] * y_ref[pl.ds(s, tm, stride=SUBLANES), :]
        pieces.append(h)
        sq = jnp.sum(h * h, axis=-1, keepdims=True)
        ss = sq if ss is None else ss + sq
    inv = lax.rsqrt(ss / h1_ref.shape[1] + RMS_EPS)
    for s in range(SUBLANES):
        o_ref[:, s * LANES:(s + 1) * LANES] = pieces[s] * inv * g_ref[:, s * LANES:(s + 1) * LANES]


def _final(h1, y4, gates, g_final):
    T, D = h1.shape
    tm = FINAL_TM
    nt = T // tm
    yspec = lambda kk: pl.BlockSpec((tm * SUBLANES, LANES), lambda i: (kk * nt + i, 0))
    return pl.pallas_call(
        _final_kernel,
        grid=(nt,),
        in_specs=[pl.BlockSpec((tm, D), lambda i: (i, 0)), yspec(0), yspec(1), yspec(2), yspec(3),
                  pl.BlockSpec((tm, LANES), lambda i: (i, 0)), pl.BlockSpec((1, D), lambda i: (0, 0))],
        out_specs=pl.BlockSpec((tm, D), lambda i: (i, 0)),
        out_shape=jax.ShapeDtypeStruct((T, D), F32),
        compiler_params=_params(("parallel",)),
        name="final",
    )(h1, y4, y4, y4, y4, gates, g_final)


def _rope_tables(positions):
    half = ROPE_DIM // 2
    inv_freq = ROPE_THETA ** (-jnp.arange(half, dtype=F32) / half)
    ang = positions.astype(F32).reshape(-1)[:, None] * inv_freq
    cos, sin = jnp.cos(ang), jnp.sin(ang)
    T = ang.shape[0]
    cos_t = jnp.concatenate([cos, cos, jnp.ones((T, HEAD_DIM - ROPE_DIM), F32)], axis=1)
    sa_t = jnp.concatenate([-sin, jnp.zeros((T, HEAD_DIM - half), F32)], axis=1)
    sb_t = jnp.concatenate([jnp.zeros((T, half), F32), sin, jnp.zeros((T, HEAD_DIM - ROPE_DIM), F32)], axis=1)
    return cos_t, sa_t, sb_t


def _routing(top_idx, T):
    A = T * TOP_K
    BLK = MOE_BLOCK
    n_blocks = A // BLK
    n_items = n_blocks + N_EXPERTS
    flat_e = top_idx.reshape(A)
    row_a = (jnp.sort(flat_e * A + jnp.arange(A, dtype=jnp.int32)) % A).reshape(n_blocks, BLK)
    row_tok = row_a // TOP_K
    row_slot = (row_a % TOP_K) * T + row_tok
    idx = jnp.concatenate([row_tok, row_slot], axis=1)
    experts = jnp.arange(N_EXPERTS, dtype=jnp.int32)
    counts = jnp.sum((flat_e[None, :] == experts[:, None]).astype(jnp.int32), axis=1)
    ends = jnp.cumsum(counts)
    starts = ends - counts
    first_blk = starts // BLK
    per_expert = jnp.where(counts > 0, (ends - 1) // BLK - first_blk + 1, 0)
    item_ends = jnp.cumsum(per_expert)
    item_starts = item_ends - per_expert
    n_used = item_ends[-1].astype(jnp.int32).reshape(1)
    it = jnp.arange(n_items + 2, dtype=jnp.int32)
    item_e = jnp.minimum(jnp.sum((item_ends[None, :] <= it[:, None]).astype(jnp.int32), axis=1), N_EXPERTS - 1)
    item_blk = jnp.clip(first_blk[item_e] + it - item_starts[item_e], 0, n_blocks - 1)
    live = it < item_ends[-1]
    item_lo = jnp.where(live, jnp.clip(starts[item_e] - item_blk * BLK, 0, BLK), 0).astype(jnp.int32)
    item_hi = jnp.where(live, jnp.clip(ends[item_e] - item_blk * BLK, 0, BLK), 0).astype(jnp.int32)
    return item_e.astype(jnp.int32), item_blk.astype(jnp.int32), item_lo, item_hi, n_used, idx, A


def kernel(x, positions, g_mix, w_in, conv_qk, b_if, g_mlstm_norm, w_attn_out, w_mlstm_out, w_mix_out, g_ffn,
           w_router, b_router, w_gate_up, b_gate_up, w_down, b_down, g_final):
    B, S, D = x.shape
    T = B * S
    l = 0
    x2 = x.reshape(T, D)

    w = w_in[l]
    o = np.cumsum((0, ATTN_WIDTH, ATTN_WIDTH, ATTN_WIDTH, M_WIDTH, M_WIDTH, M_WIDTH, M_WIDTH, M_HEADS, M_HEADS,
                   D_MODEL, D_MODEL))
    seg = lambda a: w[:, int(o[a]):int(o[a + 1])]
    w_main = jnp.concatenate([seg(0), seg(1), seg(9), seg(10), seg(3), seg(4), seg(5), seg(6), seg(2)],
                             axis=1).astype(BF16)
    w_if = jnp.pad(w[:, int(o[7]):int(o[9])], ((0, 0), (0, LANES - 2 * M_HEADS))).astype(BF16)
    cos_t, sa_t, sb_t = _rope_tables(positions)

    proj, ifo = _proj(x2, g_mix[l][None, :], w_main, w_if, cos_t, sa_t, sb_t)

    proj3 = proj.reshape(B, S, N_MAIN)
    attn_out = [_attention(proj3, g) for g in range(N_GROUPS)]
    aos = [o.reshape(T, ATTN_OUT) for o, _ in attn_out]
    lses = [l.reshape(T, LANES) for _, l in attn_out]

    nC = S // MLSTM_CHUNK
    gates = ifo[:, :2 * M_HEADS].reshape(B, S, 2 * M_HEADS).transpose(0, 2, 1).reshape(B, 2 * M_HEADS, nC, MLSTM_CHUNK)
    mem = _mlstm(proj3, gates, conv_qk[l][:, :M_WIDTH], conv_qk[l][:, M_WIDTH:], b_if[l],
                 g_mlstm_norm[l][None, :])

    w_r32 = jnp.pad(w_router[l], ((0, 0), (0, LANES - N_EXPERTS)))
    w_r = w_r32.astype(BF16)
    w_r_lo = (w_r32 - w_r.astype(F32)).astype(BF16)
    b_r = jnp.concatenate([b_router[l], jnp.full((LANES - N_EXPERTS,), NEG, F32)])[None, :]
    h1, xn2, top_idx, gates4 = _post(x2, aos, lses, mem.reshape(T, M_WIDTH), proj, w_attn_out[l].astype(BF16),
                                     w_mlstm_out[l].astype(BF16), w_mix_out[l].astype(BF16), g_ffn[l][None, :],
                                     w_r, w_r_lo, b_r)

    item_e, item_blk, item_lo, item_hi, n_used, idx, n_out_rows = _routing(top_idx[:, :TOP_K], T)
    pair = np.arange(LANES)
    perm_np = np.zeros((2 * LANES, 2 * LANES), np.float32)
    perm_np[2 * pair, pair] = 1.0
    perm_np[2 * pair + 1, LANES + pair] = 1.0
    bgu = b_gate_up[l].reshape(N_EXPERTS, 1, -1, LANES, 2).swapaxes(-1, -2).reshape(N_EXPERTS, 1, 2 * D_FF)
    y4 = _experts(item_e, item_blk, item_lo, item_hi, n_used, idx, xn2, w_gate_up[l], w_down[l], bgu,
                  b_down[l][:, None, :], jnp.asarray(perm_np, BF16), n_out_rows)

    out = _final(h1, y4, gates4, g_final[None, :])
    return out.reshape(B, S, D)
```

```python
import functools

import jax
import jax.numpy as jnp
import numpy as np
from jax import lax
from jax.experimental import pallas as pl
from jax.experimental.pallas import tpu as pltpu

F32 = jnp.float32
BF16 = jnp.bfloat16

D_MODEL = 1024
N_GROUPS = 3
GROUP_DILATION = (1, 4, 16)
HEADS_PER_GROUP = 4
HEAD_DIM = 128
ATTN_WIDTH = N_GROUPS * HEADS_PER_GROUP * HEAD_DIM
ATTN_OUT = HEADS_PER_GROUP * HEAD_DIM
ATTN_BLOCK = 128
ROPE_THETA = 500000.0
ROPE_DIM = HEAD_DIM // 4
M_HEADS = 4
M_WIDTH = D_MODEL
M_HEAD_DIM = M_WIDTH // M_HEADS
CONV_WIDTH = 4
N_EXPERTS = 32
TOP_K = 4
D_FF = D_MODEL
SWIGLU_LIMIT = 7.0
SWIGLU_ALPHA = 1.702
RMS_EPS = 1e-5
LN_EPS = 1e-5

LANES = 128
SUBLANES = 8
VMEM_LIMIT = 56 * 1024 * 1024

PROJ_TM = 1024
PROJ_TN = 1536
MLSTM_CHUNK = 128
MLSTM_SEGMENT = 1024
POST_TM = 512
MOE_BLOCK = 256
FINAL_TM = 512
NEG = -1e30

COL_AQ = 0
COL_AK = ATTN_WIDTH
COL_GA = 2 * ATTN_WIDTH
COL_GM = COL_GA + D_MODEL
COL_MQ = COL_GM + D_MODEL
COL_MK = COL_MQ + M_WIDTH
COL_MV = COL_MK + M_WIDTH
COL_MO = COL_MV + M_WIDTH
COL_AV = COL_MO + M_WIDTH
N_MAIN = COL_AV + ATTN_WIDTH


def _sigmoid(x):
    return 0.5 * jnp.tanh(0.5 * x) + 0.5


def _params(sem):
    return pltpu.CompilerParams(dimension_semantics=sem, vmem_limit_bytes=VMEM_LIMIT)


def _proj_kernel(x_ref, g_ref, w_ref, wif_ref, cos_ref, sa_ref, sb_ref, o_ref, if_ref, xn_ref, *, rope_tiles):
    j = pl.program_id(1)

    @pl.when(j == 0)
    def _():
        x = x_ref[...]
        ms = jnp.mean(x * x, axis=-1, keepdims=True)
        xn = (x * lax.rsqrt(ms + RMS_EPS) * g_ref[...]).astype(BF16)
        xn_ref[...] = xn
        if_ref[...] = jnp.dot(xn, wif_ref[...], preferred_element_type=F32)

    acc = jnp.dot(xn_ref[...], w_ref[...], preferred_element_type=F32)

    @pl.when(j < rope_tiles)
    def _():
        c = cos_ref[...]
        sa = sa_ref[...]
        sb = sb_ref[...]
        for h in range(acc.shape[1] // HEAD_DIM):
            a = acc[:, h * HEAD_DIM:(h + 1) * HEAD_DIM]
            y = a * c + pltpu.roll(a, HEAD_DIM - ROPE_DIM // 2, 1) * sa + pltpu.roll(a, ROPE_DIM // 2, 1) * sb
            o_ref[:, h * HEAD_DIM:(h + 1) * HEAD_DIM] = y.astype(o_ref.dtype)

    @pl.when(j >= rope_tiles)
    def _():
        o_ref[...] = acc.astype(o_ref.dtype)


def _proj(x2, g_mix, w_main, w_if, cos_t, sa_t, sb_t):
    T, D = x2.shape
    N = w_main.shape[1]
    tm, tn = PROJ_TM, PROJ_TN
    grid = (T // tm, N // tn)
    return pl.pallas_call(
        functools.partial(_proj_kernel, rope_tiles=(2 * ATTN_WIDTH) // tn),
        grid=grid,
        in_specs=[
            pl.BlockSpec((tm, D), lambda i, j: (i, 0)),
            pl.BlockSpec((1, D), lambda i, j: (0, 0)),
            pl.BlockSpec((D, tn), lambda i, j: (0, j)),
            pl.BlockSpec((D, LANES), lambda i, j: (0, 0)),
            pl.BlockSpec((tm, LANES), lambda i, j: (i, 0)),
            pl.BlockSpec((tm, LANES), lambda i, j: (i, 0)),
            pl.BlockSpec((tm, LANES), lambda i, j: (i, 0)),
        ],
        out_specs=[
            pl.BlockSpec((tm, tn), lambda i, j: (i, j)),
            pl.BlockSpec((tm, LANES), lambda i, j: (i, 0)),
        ],
        out_shape=[jax.ShapeDtypeStruct((T, N), BF16), jax.ShapeDtypeStruct((T, LANES), F32)],
        scratch_shapes=[pltpu.VMEM((tm, D), BF16)],
        compiler_params=_params(("parallel", "arbitrary")),
        name="proj",
    )(x2, g_mix, w_main, w_if, cos_t, sa_t, sb_t)


def _attn_kernel(q_ref, k_ref, v_ref, o_ref, lse_ref, stage_ref, q_rm, k_rm, v_rm, o_rm, *, dilation):
    S = q_ref.shape[1]
    d = dilation
    L = S // d
    per_class = L // ATTN_BLOCK
    nblk = S // ATTN_BLOCK
    h = pl.program_id(1)

    if d == 1:
        q_all, k_all, v_all = q_ref[0], k_ref[0], v_ref[0]
    else:
        for src, dst in ((q_ref, q_rm), (k_ref, k_rm), (v_ref, v_rm)):
            stage_ref[...] = src[0].astype(F32)
            if d <= SUBLANES:
                for r in range(d):
                    dst[r * L:(r + 1) * L, :] = stage_ref[pl.ds(r, L, stride=d), :].astype(BF16)
            else:
                d0 = int(round(d ** 0.5))
                assert d0 * d0 == d and d0 <= SUBLANES
                for a in range(d0):
                    o_rm[a * (S // d0):(a + 1) * (S // d0), :] = stage_ref[pl.ds(a, S // d0, stride=d0), :]
                for a in range(d0):
                    for b in range(d0):
                        r = a + d0 * b
                        dst[r * L:(r + 1) * L, :] = o_rm[pl.ds(a * (S // d0) + b, L, stride=d0), :].astype(BF16)
        q_all, k_all, v_all = q_rm[...], k_rm[...], v_rm[...]

    @pl.when(h == 0)
    def _():
        lse_ref[...] = jnp.zeros_like(lse_ref)

    B_ = ATTN_BLOCK
    q3 = q_all.reshape(nblk, B_, HEAD_DIM)
    k3 = k_all.reshape(nblk, B_, HEAD_DIM)
    v3 = v_all.reshape(nblk, B_, HEAD_DIM)
    blk = lax.broadcasted_iota(jnp.int32, (nblk, B_, B_), 0)
    row = lax.broadcasted_iota(jnp.int32, (nblk, B_, B_), 1)
    col = lax.broadcasted_iota(jnp.int32, (nblk, B_, B_), 2)
    scale = HEAD_DIM ** -0.5
    dn_qk = (((2,), (2,)), ((0,), (0,)))
    dn_pv = (((2,), (1,)), ((0,), (0,)))
    s_c = lax.dot_general(q3, k3, dn_qk, preferred_element_type=F32) * scale + jnp.where(col <= row, 0.0, NEG)
    m = jnp.max(s_c, axis=2, keepdims=True)
    if per_class > 1:
        k_prev = jnp.concatenate([k3[:1], k3[:-1]], axis=0)
        v_prev = jnp.concatenate([v3[:1], v3[:-1]], axis=0)
        ok_prev = jnp.logical_and(col >= row, blk % per_class != 0)
        s_p = lax.dot_general(q3, k_prev, dn_qk, preferred_element_type=F32) * scale + jnp.where(ok_prev, 0.0, NEG)
        m = jnp.maximum(m, jnp.max(s_p, axis=2, keepdims=True))
    p_c = jnp.exp(s_c - m)
    den = jnp.sum(p_c, axis=2, keepdims=True)
    acc = lax.dot_general(p_c.astype(BF16), v3, dn_pv, preferred_element_type=F32)
    if per_class > 1:
        p_p = jnp.exp(s_p - m)
        den = den + jnp.sum(p_p, axis=2, keepdims=True)
        acc = acc + lax.dot_general(p_p.astype(BF16), v_prev, dn_pv, preferred_element_type=F32)
    out = (acc / den).reshape(S, HEAD_DIM)
    lse = m + jnp.log(den)
    my_lane = lax.broadcasted_iota(jnp.int32, (B_, LANES), 1) == h
    if d == 1:
        o_ref[0] = out.astype(o_ref.dtype)
    else:
        stage_ref[...] = out
    for n in range(nblk):
        sl = slice(n * B_, (n + 1) * B_)
        start = n // per_class + d * (n % per_class) * B_
        rows = pl.ds(start, B_, stride=d) if d > 1 else sl
        if d > 1:
            o_rm[rows, :] = stage_ref[sl, :]
        lse_ref[0, rows, :] = jnp.where(my_lane, lse[n], lse_ref[0, rows, :])
    if d > 1:
        o_ref[0] = o_rm[...].astype(o_ref.dtype)


def _attention(proj3, g):
    B, S, _ = proj3.shape
    Dh = HEAD_DIM

    def col(base):
        return pl.BlockSpec((1, S, Dh), lambda b, h: (b, 0, (base + g * ATTN_OUT) // Dh + h))

    return pl.pallas_call(
        functools.partial(_attn_kernel, dilation=GROUP_DILATION[g]),
        grid=(B, HEADS_PER_GROUP),
        in_specs=[col(COL_AQ), col(COL_AK), col(COL_AV)],
        out_specs=[pl.BlockSpec((1, S, Dh), lambda b, h: (b, 0, h)),
                   pl.BlockSpec((1, S, LANES), lambda b, h: (b, 0, 0))],
        out_shape=[jax.ShapeDtypeStruct((B, S, ATTN_OUT), BF16), jax.ShapeDtypeStruct((B, S, LANES), F32)],
        scratch_shapes=[pltpu.VMEM((S, Dh), F32), pltpu.VMEM((S, Dh), BF16), pltpu.VMEM((S, Dh), BF16),
                        pltpu.VMEM((S, Dh), BF16), pltpu.VMEM((S, Dh), F32)],
        compiler_params=_params(("parallel", "arbitrary")),
        name=f"attn{g}",
    )(proj3, proj3, proj3)


def _mlstm_kernel(bif_ref, mq_ref, mk_ref, mv_ref, mo_ref, g_ref, cwq_ref, cwk_ref, gn_ref, o_ref,
                  pad_ref, qs_ref, ks_ref, c_ref, n_ref, m_ref, hq_ref, hk_ref):
    SEG = mq_ref.shape[1]
    LC = MLSTM_CHUNK
    Dh = M_HEAD_DIM
    H = M_HEADS
    seg = pl.program_id(1)

    @pl.when(seg == 0)
    def _():
        c_ref[...] = jnp.zeros_like(c_ref)
        n_ref[...] = jnp.zeros_like(n_ref)
        m_ref[...] = jnp.zeros_like(m_ref)
        hq_ref[...] = jnp.zeros_like(hq_ref)
        hk_ref[...] = jnp.zeros_like(hk_ref)

    def conv_silu(src_ref, hist_ref, w_ref, cols, dst_ref, scale):
        pad_ref[0:SUBLANES, :] = hist_ref[:, cols]
        pad_ref[SUBLANES:SUBLANES + SEG, :] = src_ref[0, :, cols].astype(F32)
        hist_ref[:, cols] = pad_ref[SEG:SEG + SUBLANES, :]
        rb = 2 * LC
        for r0 in range(0, SEG, rb):
            acc = None
            for j in range(CONV_WIDTH):
                off = r0 + SUBLANES - (CONV_WIDTH - 1) + j
                term = w_ref[j:j + 1, cols] * pad_ref[off:off + rb, :]
                acc = term if acc is None else acc + term
            dst_ref[r0:r0 + rb, cols] = (acc * _sigmoid(acc) * scale).astype(dst_ref.dtype)

    for hd in range(H):
        cols = slice(hd * Dh, (hd + 1) * Dh)
        conv_silu(mq_ref, hq_ref, cwq_ref, cols, qs_ref, 1.0)
        conv_silu(mk_ref, hk_ref, cwk_ref, cols, ks_ref, Dh ** -0.5)

    row = lax.broadcasted_iota(jnp.int32, (LC, LC), 0)
    col = lax.broadcasted_iota(jnp.int32, (LC, LC), 1)
    causal = col <= row
    eye = col == row
    lane8 = lax.broadcasted_iota(jnp.int32, (SUBLANES, LC), 1)

    def head_chunk(hd, c, r0, m):
        cols = slice(hd * Dh, (hd + 1) * Dh)
        c_ref_h = c_ref.at[hd]
        gn = gn_ref[:, cols]
        q = qs_ref[pl.ds(r0, LC), cols]
        k = ks_ref[pl.ds(r0, LC), cols]
        v = mv_ref[0, pl.ds(r0, LC), cols]
        i_row = g_ref[0, hd, pl.ds(c, 1), :] + bif_ref[hd]
        f_row = g_ref[0, H + hd, pl.ds(c, 1), :] + bif_ref[H + hd]
        logf = jnp.minimum(f_row, 0.0) - jnp.log(1.0 + jnp.exp(-jnp.abs(f_row)))
        b8 = jnp.broadcast_to(logf, (SUBLANES, LC))
        s = 1
        while s < LC:
            b8 = b8 + jnp.where(lane8 >= s, pltpu.roll(b8, s, 1), 0.0)
            s *= 2
        b_row = b8[0:1, :]
        d_row = i_row - b_row
        b_col = jnp.sum(jnp.where(eye, b_row, 0.0), axis=1, keepdims=True)
        d_col = jnp.sum(jnp.where(eye, d_row, 0.0), axis=1, keepdims=True)
        dm = jnp.where(causal, b_col + d_row, NEG)
        inter = b_col + m
        m_t = jnp.maximum(inter, jnp.max(dm, axis=1, keepdims=True))
        w_intra = jnp.exp(dm - m_t)
        w_inter = jnp.exp(inter - m_t)
        kb = k.astype(BF16)
        a = lax.dot_general(q, kb, (((1,), (1,)), ((), ())), preferred_element_type=F32) * w_intra
        num = (jnp.dot(a.astype(BF16), v, preferred_element_type=F32)
               + w_inter * jnp.dot(q, c_ref_h[...].astype(BF16), preferred_element_type=F32))
        den = (jnp.sum(a, axis=1, keepdims=True)
               + w_inter * jnp.sum(q.astype(F32) * n_ref[:, cols], axis=1, keepdims=True))
        hh = num / jnp.maximum(jnp.abs(den), jnp.exp(-m_t))
        mu = jnp.mean(hh, axis=1, keepdims=True)
        xc = hh - mu
        var = jnp.mean(xc * xc, axis=1, keepdims=True)
        hn = xc * lax.rsqrt(var + LN_EPS) * gn
        gate = _sigmoid(mo_ref[0, pl.ds(r0, LC), cols].astype(F32))
        o_ref[0, pl.ds(r0, LC), cols] = (gate * hn).astype(o_ref.dtype)
        b_last = b_row[:, LC - 1:LC]
        m_new = jnp.maximum(b_last + m, jnp.max(b_last + d_row, axis=1, keepdims=True))
        decay = jnp.exp(b_last + m - m_new)
        kw = k * jnp.exp(b_last + d_col - m_new)
        c_ref_h[...] = decay * c_ref_h[...] + lax.dot_general(
            kw.astype(BF16), v, (((0,), (0,)), ((), ())), preferred_element_type=F32)
        n_ref[:, cols] = decay * n_ref[:, cols] + jnp.sum(kw, axis=0, keepdims=True)
        return m_new

    def chunk(c, ms):
        r0 = pl.multiple_of(c * LC, LC)
        return tuple(head_chunk(hd, c, r0, ms[hd]) for hd in range(H))

    ms = lax.fori_loop(0, SEG // LC, chunk, tuple(m_ref[:, hd:hd + 1] for hd in range(H)))
    for hd in range(H):
        m_ref[:, hd:hd + 1] = ms[hd]


def _mlstm(proj3, gates, conv_q, conv_k, b_if, g_norm):
    B, S, _ = proj3.shape
    W = M_WIDTH
    SEG = MLSTM_SEGMENT
    nseg = S // SEG
    nC = SEG // MLSTM_CHUNK

    def col(base):
        return pl.BlockSpec((1, SEG, W), lambda b, s, pre: (b, s, base // W))

    full = lambda a: pl.BlockSpec(a.shape, lambda b, s, pre: (0,) * a.ndim)
    grid_spec = pltpu.PrefetchScalarGridSpec(
        num_scalar_prefetch=1,
        grid=(B, nseg),
        in_specs=[
            col(COL_MQ), col(COL_MK), col(COL_MV), col(COL_MO),
            pl.BlockSpec((1, 2 * M_HEADS, nC, MLSTM_CHUNK), lambda b, s, pre: (b, 0, s, 0)),
            full(conv_q), full(conv_k), full(g_norm),
        ],
        out_specs=pl.BlockSpec((1, SEG, W), lambda b, s, pre: (b, s, 0)),
        scratch_shapes=[
            pltpu.VMEM((SEG + SUBLANES, M_HEAD_DIM), F32),
            pltpu.VMEM((SEG, W), BF16),
            pltpu.VMEM((SEG, W), F32),
            pltpu.VMEM((M_HEADS, M_HEAD_DIM, M_HEAD_DIM), F32),
            pltpu.VMEM((1, W), F32),
            pltpu.VMEM((1, LANES), F32),
            pltpu.VMEM((SUBLANES, W), F32),
            pltpu.VMEM((SUBLANES, W), F32),
        ],
    )
    return pl.pallas_call(
        _mlstm_kernel,
        grid_spec=grid_spec,
        out_shape=jax.ShapeDtypeStruct((B, S, W), BF16),
        compiler_params=_params(("parallel", "arbitrary")),
        name="mlstm",
    )(b_if, proj3, proj3, proj3, proj3, gates, conv_q, conv_k, g_norm)


def _post_kernel(x_ref, ao0_ref, ao1_ref, ao2_ref, lse0_ref, lse1_ref, lse2_ref, mem_ref, ga_ref, gm_ref,
                 wa_ref, wm_ref, wo_ref, gf_ref, wr_ref, wrl_ref, br_ref, h1_ref, xn_ref, idx_ref, gate_ref):
    tm = x_ref.shape[0]
    ao_refs = (ao0_ref, ao1_ref, ao2_ref)
    lses = (lse0_ref[...], lse1_ref[...], lse2_ref[...])
    heads = []
    for h in range(HEADS_PER_GROUP):
        ls = [l[:, h:h + 1] for l in lses]
        mx = jnp.maximum(jnp.maximum(ls[0], ls[1]), ls[2])
        es = [jnp.exp(l - mx) for l in ls]
        tot = es[0] + es[1] + es[2]
        acc = None
        for g in range(N_GROUPS):
            term = (es[g] / tot) * ao_refs[g][:, h * HEAD_DIM:(h + 1) * HEAD_DIM].astype(F32)
            acc = term if acc is None else acc + term
        heads.append(acc.astype(BF16))
    attn = jnp.concatenate(heads, axis=1)
    ya = jnp.dot(attn, wa_ref[...], preferred_element_type=F32)
    ym = jnp.dot(mem_ref[...], wm_ref[...], preferred_element_type=F32)
    merged = _sigmoid(ga_ref[...].astype(F32)) * ya + _sigmoid(gm_ref[...].astype(F32)) * ym
    h1 = x_ref[...] + jnp.dot(merged.astype(BF16), wo_ref[...], preferred_element_type=F32)
    h1_ref[...] = h1
    ms = jnp.mean(h1 * h1, axis=-1, keepdims=True)
    xn = h1 * lax.rsqrt(ms + RMS_EPS) * gf_ref[...]
    for s in range(SUBLANES):
        xn_ref[pl.ds(s, tm, stride=SUBLANES), :] = xn[:, s * LANES:(s + 1) * LANES]
    xn_hi = xn.astype(BF16)
    xn_lo = (xn - xn_hi.astype(F32)).astype(BF16)
    logits = (jnp.dot(xn_hi, wr_ref[...], preferred_element_type=F32)
              + jnp.dot(xn_hi, wrl_ref[...], preferred_element_type=F32)
              + jnp.dot(xn_lo, wr_ref[...], preferred_element_type=F32)) + br_ref[...]
    lane = lax.broadcasted_iota(jnp.int32, (tm, LANES), 1).astype(F32)
    vals = logits
    tops, idxs = [], []
    for _ in range(TOP_K):
        mx = jnp.max(vals, axis=1, keepdims=True)
        ix = jnp.min(jnp.where(vals == mx, lane, float(LANES)), axis=1, keepdims=True)
        tops.append(mx)
        idxs.append(ix)
        vals = jnp.where(lane == ix, NEG, vals)
    es = [jnp.exp(t - tops[0]) for t in tops]
    tot = es[0] + es[1] + es[2] + es[3]
    idx_out = jnp.zeros((tm, LANES), F32)
    gate_out = jnp.zeros((tm, LANES), F32)
    for kk in range(TOP_K):
        idx_out = jnp.where(lane == float(kk), idxs[kk], idx_out)
        gate_out = jnp.where(lane == float(kk), es[kk] / tot, gate_out)
    idx_ref[...] = idx_out.astype(jnp.int32)
    gate_ref[...] = gate_out


def _post(x2, aos, lses, mem, proj, w_a, w_m, w_o, g_ffn, w_r, w_r_lo, b_r):
    T, D = x2.shape
    tm = POST_TM
    row = lambda w: pl.BlockSpec((tm, w), lambda i: (i, 0))
    full = lambda a: pl.BlockSpec(a.shape, lambda i: (0,) * a.ndim)
    return pl.pallas_call(
        _post_kernel,
        grid=(T // tm,),
        in_specs=[
            row(D), row(ATTN_OUT), row(ATTN_OUT), row(ATTN_OUT), row(LANES), row(LANES), row(LANES), row(M_WIDTH),
            pl.BlockSpec((tm, D), lambda i: (i, COL_GA // D)),
            pl.BlockSpec((tm, D), lambda i: (i, COL_GM // D)),
            full(w_a), full(w_m), full(w_o), full(g_ffn), full(w_r), full(w_r_lo), full(b_r),
        ],
        out_specs=[row(D), pl.BlockSpec((tm * SUBLANES, LANES), lambda i: (i, 0)), row(LANES), row(LANES)],
        out_shape=[jax.ShapeDtypeStruct((T, D), F32), jax.ShapeDtypeStruct((T * SUBLANES, LANES), F32),
                   jax.ShapeDtypeStruct((T, LANES), jnp.int32), jax.ShapeDtypeStruct((T, LANES), F32)],
        compiler_params=_params(("parallel",)),
        name="post",
    )(x2, *aos, *lses, mem, proj, proj, w_a, w_m, w_o, g_ffn, w_r, w_r_lo, b_r)


def _expert_kernel(be_ref, ib_ref, lo_ref, hi_ref, nu_ref, idx_hbm, xn_hbm, wgu_ref, wd_ref, bgu_ref, bd_ref, perm_ref,
                   y_hbm, idx_smem, xbuf, ybuf, xb_ref, wgu_bf, wd_bf, isem, gsem, ssem):
    i = pl.program_id(0)
    n_idx = ib_ref.shape[0]
    n_used = nu_ref[0]
    BLK = MOE_BLOCK
    n_islots = idx_smem.shape[0]
    n_bufs = xbuf.shape[0]

    def tile(t):
        return pl.ds(pl.multiple_of(t * SUBLANES, SUBLANES), SUBLANES)

    def tiles(n):
        return pl.ds(0, pl.multiple_of(n * SUBLANES, SUBLANES))

    def idx_copy(blk):
        slot = blk % n_islots
        return pltpu.make_async_copy(idx_hbm.at[ib_ref[blk]], idx_smem.at[slot], isem.at[slot])

    def gather_start(blk):
        islot, bslot = blk % n_islots, blk % n_bufs
        for r in range(BLK):
            tok = idx_smem[islot, r]
            pltpu.make_async_copy(xn_hbm.at[tile(tok)], xbuf.at[bslot, pl.ds(r * SUBLANES, SUBLANES)],
                                  gsem.at[bslot]).start()

    def gather_wait(blk):
        bslot = blk % n_bufs
        pltpu.make_async_copy(xn_hbm.at[tiles(BLK)], xbuf.at[bslot], gsem.at[bslot]).wait()

    def row_range(blk):
        ok = blk >= 0
        b = jnp.maximum(blk, 0)
        return jnp.where(ok, lo_ref[b], 0), jnp.where(ok, hi_ref[b], 0)

    def scatter_start(blk):
        islot, bslot = blk % n_islots, blk % n_bufs
        lo, hi = row_range(blk)
        n = (hi - lo).astype(jnp.uint32)
        for r in range(BLK):
            @pl.when((r - lo).astype(jnp.uint32) < n)
            def _():
                dst = idx_smem[islot, BLK + r]
                pltpu.make_async_copy(ybuf.at[bslot, pl.ds(r * SUBLANES, SUBLANES)], y_hbm.at[tile(dst)],
                                      ssem.at[bslot]).start()

    def scatter_wait(blk):
        bslot = blk % n_bufs
        lo, hi = row_range(blk)
        n = hi - lo

        @pl.when(n > 0)
        def _():
            pltpu.make_async_copy(ybuf.at[bslot, tiles(n)], y_hbm.at[tiles(n)], ssem.at[bslot]).wait()

    @pl.when(i == 0)
    def _():
        idx_copy(0).start()
        idx_copy(1).start()
        idx_copy(2).start()
        idx_copy(0).wait()
        idx_copy(1).wait()
        gather_start(0)
        gather_start(1)

    @pl.when(i + 3 < n_idx)
    def _():
        idx_copy(i + 3).start()

    @pl.when(i + 2 < n_idx)
    def _():
        idx_copy(i + 2).wait()

    @pl.when(jnp.logical_and(i < n_used, jnp.logical_or(i == 0, be_ref[i] != be_ref[jnp.maximum(i - 1, 0)])))
    def _():
        pw = perm_ref.shape[0]
        for c in range(wgu_ref.shape[2] // pw):
            cols = slice(c * pw, (c + 1) * pw)
            wgu_bf[:, cols] = jnp.dot(wgu_ref[0, :, cols].astype(BF16), perm_ref[...],
                                      preferred_element_type=F32).astype(BF16)
        wd_bf[...] = wd_ref[0].astype(BF16)

    @pl.when(i < n_used)
    def _():
        gather_wait(i)
        scatter_wait(i - 3)

    @pl.when(i < n_used)
    def _():
        bslot = i % n_bufs
        for s in range(SUBLANES):
            xb_ref[:, s * LANES:(s + 1) * LANES] = xbuf[bslot, pl.ds(s, BLK, stride=SUBLANES), :].astype(BF16)
        gather_start(i + 2)
        scatter_start(i - 1)
        hg = jnp.dot(xb_ref[...], wgu_bf[...], preferred_element_type=F32) + bgu_ref[0]
        acts = []
        for c in range(hg.shape[1] // (2 * LANES)):
            gate = jnp.minimum(hg[:, 2 * c * LANES:(2 * c + 1) * LANES], SWIGLU_LIMIT)
            up = jnp.clip(hg[:, (2 * c + 1) * LANES:(2 * c + 2) * LANES], -SWIGLU_LIMIT, SWIGLU_LIMIT)
            acts.append(((up + 1.0) * (gate * _sigmoid(SWIGLU_ALPHA * gate))).astype(BF16))
        act = jnp.concatenate(acts, axis=1)
        y = jnp.dot(act, wd_bf[...], preferred_element_type=F32) + bd_ref[0]
        for s in range(SUBLANES):
            ybuf[bslot, pl.ds(s, BLK, stride=SUBLANES), :] = y[:, s * LANES:(s + 1) * LANES]

    @pl.when(i == n_used)
    def _():
        gather_wait(i)
        gather_wait(i + 1)
        scatter_start(i - 1)
        scatter_wait(i - 3)
        scatter_wait(i - 2)
        scatter_wait(i - 1)


def _experts(item_e, item_blk, item_lo, item_hi, n_used, idx, xn2, wgu, wd, bgu, bd, perm, n_out_rows):
    nb = item_e.shape[0] - 1
    D = wd.shape[2]
    F2 = wgu.shape[2]
    emap = lambda i, be, ib, lo, hi, nu: (be[i], 0, 0)
    grid_spec = pltpu.PrefetchScalarGridSpec(
        num_scalar_prefetch=5,
        grid=(nb,),
        in_specs=[
            pl.BlockSpec(memory_space=pl.ANY),
            pl.BlockSpec(memory_space=pl.ANY),
            pl.BlockSpec((1, D, F2), emap),
            pl.BlockSpec((1, F2 // 2, D), emap),
            pl.BlockSpec((1, 1, F2), emap),
            pl.BlockSpec((1, 1, D), emap),
            pl.BlockSpec(perm.shape, lambda i, be, ib, lo, hi, nu: (0, 0)),
        ],
        out_specs=pl.BlockSpec(memory_space=pl.ANY),
        scratch_shapes=[
            pltpu.SMEM((8, 2 * MOE_BLOCK), jnp.int32),
            pltpu.VMEM((3, MOE_BLOCK * SUBLANES, LANES), F32),
            pltpu.VMEM((3, MOE_BLOCK * SUBLANES, LANES), F32),
            pltpu.VMEM((MOE_BLOCK, D), BF16),
            pltpu.VMEM((D, F2), BF16),
            pltpu.VMEM((F2 // 2, D), BF16),
            pltpu.SemaphoreType.DMA((8,)),
            pltpu.SemaphoreType.DMA((3,)),
            pltpu.SemaphoreType.DMA((3,)),
        ],
    )
    return pl.pallas_call(
        _expert_kernel,
        grid_spec=grid_spec,
        out_shape=jax.ShapeDtypeStruct((n_out_rows * SUBLANES, LANES), F32),
        compiler_params=_params(("arbitrary",)),
        name="experts",
    )(item_e, item_blk, item_lo, item_hi, n_used, idx, xn2, wgu, wd, bgu, bd, perm)


def _final_kernel(h1_ref, y0_ref, y1_ref, y2_ref, y3_ref, gate_ref, g_ref, o_ref):
    tm = h1_ref.shape[0]
    gts = gate_ref[...]
    g4 = [gts[:, kk:kk + 1] for kk in range(TOP_K)]
    pieces = []
    ss = None
    for s in range(SUBLANES):
        h = h1_ref[:, s * LANES:(s + 1) * LANES]
        for kk, y_ref in enumerate((y0_ref, y1_ref, y2_ref, y3_ref)):
            h = h + g4[kk] * y_ref[pl.ds(s, tm, stride=SUBLANES), :]
        pieces.append(h)
        sq = jnp.sum(h * h, axis=-1, keepdims=True)
        ss = sq if ss is None else ss + sq
    inv = lax.rsqrt(ss / h1_ref.shape[1] + RMS_EPS)
    for s in range(SUBLANES):
        o_ref[:, s * LANES:(s + 1) * LANES] = pieces[s] * inv * g_ref[:, s * LANES:(s + 1) * LANES]


def _final(h1, y4, gates, g_final):
    T, D = h1.shape
    tm = FINAL_TM
    nt = T // tm
    yspec = lambda kk: pl.BlockSpec((tm * SUBLANES, LANES), lambda i: (kk * nt + i, 0))
    return pl.pallas_call(
        _final_kernel,
        grid=(nt,),
        in_specs=[pl.BlockSpec((tm, D), lambda i: (i, 0)), yspec(0), yspec(1), yspec(2), yspec(3),
                  pl.BlockSpec((tm, LANES), lambda i: (i, 0)), pl.BlockSpec((1, D), lambda i: (0, 0))],
        out_specs=pl.BlockSpec((tm, D), lambda i: (i, 0)),
        out_shape=jax.ShapeDtypeStruct((T, D), F32),
        compiler_params=_params(("parallel",)),
        name="final",
    )(h1, y4, y4, y4, y4, gates, g_final)


def _rope_tables(positions):
    half = ROPE_DIM // 2
    inv_freq = ROPE_THETA ** (-jnp.arange(half, dtype=F32) / half)
    ang = positions.astype(F32).reshape(-1)[:, None] * inv_freq
    cos, sin = jnp.cos(ang), jnp.sin(ang)
    T = ang.shape[0]
    cos_t = jnp.concatenate([cos, cos, jnp.ones((T, HEAD_DIM - ROPE_DIM), F32)], axis=1)
    sa_t = jnp.concatenate([-sin, jnp.zeros((T, HEAD_DIM - half), F32)], axis=1)
    sb_t = jnp.concatenate([jnp.zeros((T, half), F32), sin, jnp.zeros((T, HEAD_DIM - ROPE_DIM), F32)], axis=1)
    return cos_t, sa_t, sb_t


def _routing(top_idx, T):
    A = T * TOP_K
    BLK = MOE_BLOCK
    n_blocks = A // BLK
    n_items = n_blocks + N_EXPERTS
    flat_e = top_idx.reshape(A)
    row_a = (jnp.sort(flat_e * A + jnp.arange(A, dtype=jnp.int32)) % A).reshape(n_blocks, BLK)
    row_tok = row_a // TOP_K
    row_slot = (row_a % TOP_K) * T + row_tok
    idx = jnp.concatenate([row_tok, row_slot], axis=1)
    experts = jnp.arange(N_EXPERTS, dtype=jnp.int32)
    counts = jnp.sum((flat_e[None, :] == experts[:, None]).astype(jnp.int32), axis=1)
    ends = jnp.cumsum(counts)
    starts = ends - counts
    first_blk = starts // BLK
    per_expert = jnp.where(counts > 0, (ends - 1) // BLK - first_blk + 1, 0)
    item_ends = jnp.cumsum(per_expert)
    item_starts = item_ends - per_expert
    n_used = item_ends[-1].astype(jnp.int32).reshape(1)
    it = jnp.arange(n_items + 2, dtype=jnp.int32)
    item_e = jnp.minimum(jnp.sum((item_ends[None, :] <= it[:, None]).astype(jnp.int32), axis=1), N_EXPERTS - 1)
    item_blk = jnp.clip(first_blk[item_e] + it - item_starts[item_e], 0, n_blocks - 1)
    live = it < item_ends[-1]
    item_lo = jnp.where(live, jnp.clip(starts[item_e] - item_blk * BLK, 0, BLK), 0).astype(jnp.int32)
    item_hi = jnp.where(live, jnp.clip(ends[item_e] - item_blk * BLK, 0, BLK), 0).astype(jnp.int32)
    return item_e.astype(jnp.int32), item_blk.astype(jnp.int32), item_lo, item_hi, n_used, idx, A


def kernel(x, positions, g_mix, w_in, conv_qk, b_if, g_mlstm_norm, w_attn_out, w_mlstm_out, w_mix_out, g_ffn,
           w_router, b_router, w_gate_up, b_gate_up, w_down, b_down, g_final):
    B, S, D = x.shape
    T = B * S
    l = 0
    x2 = x.reshape(T, D)

    w = w_in[l]
    o = np.cumsum((0, ATTN_WIDTH, ATTN_WIDTH, ATTN_WIDTH, M_WIDTH, M_WIDTH, M_WIDTH, M_WIDTH, M_HEADS, M_HEADS,
                   D_MODEL, D_MODEL))
    w_bf = w.astype(BF16)
    seg = lambda a: w_bf[:, int(o[a]):int(o[a + 1])]
    w_main = jnp.concatenate([seg(0), seg(1), seg(9), seg(10), seg(3), seg(4), seg(5), seg(6), seg(2)], axis=1)
    w_if = jnp.pad(w[:, int(o[7]):int(o[9])], ((0, 0), (0, LANES - 2 * M_HEADS))).astype(BF16)
    cos_t, sa_t, sb_t = _rope_tables(positions)

    proj, ifo = _proj(x2, g_mix[l][None, :], w_main, w_if, cos_t, sa_t, sb_t)

    proj3 = proj.reshape(B, S, N_MAIN)
    attn_out = [_attention(proj3, g) for g in range(N_GROUPS)]
    aos = [o.reshape(T, ATTN_OUT) for o, _ in attn_out]
    lses = [l.reshape(T, LANES) for _, l in attn_out]

    nC = S // MLSTM_CHUNK
    gates = ifo[:, :2 * M_HEADS].reshape(B, S, 2 * M_HEADS).transpose(0, 2, 1).reshape(B, 2 * M_HEADS, nC, MLSTM_CHUNK)
    mem = _mlstm(proj3, gates, conv_qk[l][:, :M_WIDTH], conv_qk[l][:, M_WIDTH:], b_if[l],
                 g_mlstm_norm[l][None, :])

    w_r32 = jnp.pad(w_router[l], ((0, 0), (0, LANES - N_EXPERTS)))
    w_r = w_r32.astype(BF16)
    w_r_lo = (w_r32 - w_r.astype(F32)).astype(BF16)
    b_r = jnp.concatenate([b_router[l], jnp.full((LANES - N_EXPERTS,), NEG, F32)])[None, :]
    h1, xn2, top_idx, gates4 = _post(x2, aos, lses, mem.reshape(T, M_WIDTH), proj, w_attn_out[l].astype(BF16),
                                     w_mlstm_out[l].astype(BF16), w_mix_out[l].astype(BF16), g_ffn[l][None, :],
                                     w_r, w_r_lo, b_r)

    item_e, item_blk, item_lo, item_hi, n_used, idx, n_out_rows = _routing(top_idx[:, :TOP_K], T)
    pair = np.arange(LANES)
    perm_np = np.zeros((2 * LANES, 2 * LANES), np.float32)
    perm_np[2 * pair, pair] = 1.0
    perm_np[2 * pair + 1, LANES + pair] = 1.0
    bgu = b_gate_up[l].reshape(N_EXPERTS, 1, -1, LANES, 2).swapaxes(-1, -2).reshape(N_EXPERTS, 1, 2 * D_FF)
    y4 = _experts(item_e, item_blk, item_lo, item_hi, n_used, idx, xn2, w_gate_up[l], w_down[l], bgu,
                  b_down[l][:, None, :], jnp.asarray(perm_np, BF16), n_out_rows)

    out = _final(h1, y4, gates4, g_final[None, :])
    return out.reshape(B, S, D)
```

```python
import functools

import jax
import jax.numpy as jnp
import numpy as np
from jax import lax
from jax.experimental import pallas as pl
from jax.experimental.pallas import tpu as pltpu

F32 = jnp.float32
BF16 = jnp.bfloat16

D_MODEL = 1024
N_GROUPS = 3
GROUP_DILATION = (1, 4, 16)
HEADS_PER_GROUP = 4
HEAD_DIM = 128
ATTN_WIDTH = N_GROUPS * HEADS_PER_GROUP * HEAD_DIM
ATTN_OUT = HEADS_PER_GROUP * HEAD_DIM
ATTN_BLOCK = 128
ROPE_THETA = 500000.0
ROPE_DIM = HEAD_DIM // 4
M_HEADS = 4
M_WIDTH = D_MODEL
M_HEAD_DIM = M_WIDTH // M_HEADS
CONV_WIDTH = 4
N_EXPERTS = 32
TOP_K = 4
D_FF = D_MODEL
SWIGLU_LIMIT = 7.0
SWIGLU_ALPHA = 1.702
RMS_EPS = 1e-5
LN_EPS = 1e-5

LANES = 128
SUBLANES = 8
VMEM_LIMIT = 56 * 1024 * 1024

PROJ_TM = 1024
PROJ_TN = 1536
MLSTM_CHUNK = 128
MLSTM_SEGMENT = 512
MLSTM_BATCH = 2
POST_TM = 512
MOE_BLOCK = 256
FINAL_TM = 512
NEG = -1e30

COL_AQ = 0
COL_AK = ATTN_WIDTH
COL_GA = 2 * ATTN_WIDTH
COL_GM = COL_GA + D_MODEL
COL_MQ = COL_GM + D_MODEL
COL_MK = COL_MQ + M_WIDTH
COL_MV = COL_MK + M_WIDTH
COL_MO = COL_MV + M_WIDTH
COL_AV = COL_MO + M_WIDTH
N_MAIN = COL_AV + ATTN_WIDTH


def _sigmoid(x):
    return 0.5 * jnp.tanh(0.5 * x) + 0.5


def _params(sem):
    return pltpu.CompilerParams(dimension_semantics=sem, vmem_limit_bytes=VMEM_LIMIT)


def _proj_kernel(x_ref, g_ref, w_ref, wif_ref, cos_ref, sa_ref, sb_ref, o_ref, if_ref, xn_ref, *, rope_tiles):
    j = pl.program_id(1)

    @pl.when(j == 0)
    def _():
        x = x_ref[...]
        ms = jnp.mean(x * x, axis=-1, keepdims=True)
        xn = (x * lax.rsqrt(ms + RMS_EPS) * g_ref[...]).astype(BF16)
        xn_ref[...] = xn
        if_ref[...] = jnp.dot(xn, wif_ref[...], preferred_element_type=F32)

    acc = jnp.dot(xn_ref[...], w_ref[...], preferred_element_type=F32)

    @pl.when(j < rope_tiles)
    def _():
        c = cos_ref[...]
        sa = sa_ref[...]
        sb = sb_ref[...]
        for h in range(acc.shape[1] // HEAD_DIM):
            a = acc[:, h * HEAD_DIM:(h + 1) * HEAD_DIM]
            y = a * c + pltpu.roll(a, HEAD_DIM - ROPE_DIM // 2, 1) * sa + pltpu.roll(a, ROPE_DIM // 2, 1) * sb
            o_ref[:, h * HEAD_DIM:(h + 1) * HEAD_DIM] = y.astype(o_ref.dtype)

    @pl.when(j >= rope_tiles)
    def _():
        o_ref[...] = acc.astype(o_ref.dtype)


def _proj(x2, g_mix, w_main, w_if, cos_t, sa_t, sb_t):
    T, D = x2.shape
    N = w_main.shape[1]
    tm, tn = PROJ_TM, PROJ_TN
    grid = (T // tm, N // tn)
    return pl.pallas_call(
        functools.partial(_proj_kernel, rope_tiles=(2 * ATTN_WIDTH) // tn),
        grid=grid,
        in_specs=[
            pl.BlockSpec((tm, D), lambda i, j: (i, 0)),
            pl.BlockSpec((1, D), lambda i, j: (0, 0)),
            pl.BlockSpec((D, tn), lambda i, j: (0, j)),
            pl.BlockSpec((D, LANES), lambda i, j: (0, 0)),
            pl.BlockSpec((tm, LANES), lambda i, j: (i, 0)),
            pl.BlockSpec((tm, LANES), lambda i, j: (i, 0)),
            pl.BlockSpec((tm, LANES), lambda i, j: (i, 0)),
        ],
        out_specs=[
            pl.BlockSpec((tm, tn), lambda i, j: (i, j)),
            pl.BlockSpec((tm, LANES), lambda i, j: (i, 0)),
        ],
        out_shape=[jax.ShapeDtypeStruct((T, N), BF16), jax.ShapeDtypeStruct((T, LANES), F32)],
        scratch_shapes=[pltpu.VMEM((tm, D), BF16)],
        compiler_params=_params(("parallel", "arbitrary")),
        name="proj",
    )(x2, g_mix, w_main, w_if, cos_t, sa_t, sb_t)


def _attn_kernel(q_ref, k_ref, v_ref, o_ref, lse_ref, stage_ref, q_rm, k_rm, v_rm, o_rm, *, dilation):
    S = q_ref.shape[1]
    d = dilation
    L = S // d
    per_class = L // ATTN_BLOCK
    nblk = S // ATTN_BLOCK
    h = pl.program_id(1)

    if d == 1:
        q_all, k_all, v_all = q_ref[0], k_ref[0], v_ref[0]
    else:
        for src, dst in ((q_ref, q_rm), (k_ref, k_rm), (v_ref, v_rm)):
            stage_ref[...] = src[0].astype(F32)
            if d <= SUBLANES:
                for r in range(d):
                    dst[r * L:(r + 1) * L, :] = stage_ref[pl.ds(r, L, stride=d), :].astype(BF16)
            else:
                d0 = int(round(d ** 0.5))
                assert d0 * d0 == d and d0 <= SUBLANES
                for a in range(d0):
                    o_rm[a * (S // d0):(a + 1) * (S // d0), :] = stage_ref[pl.ds(a, S // d0, stride=d0), :]
                for a in range(d0):
                    for b in range(d0):
                        r = a + d0 * b
                        dst[r * L:(r + 1) * L, :] = o_rm[pl.ds(a * (S // d0) + b, L, stride=d0), :].astype(BF16)
        q_all, k_all, v_all = q_rm[...], k_rm[...], v_rm[...]

    @pl.when(h == 0)
    def _():
        lse_ref[...] = jnp.zeros_like(lse_ref)

    B_ = ATTN_BLOCK
    q3 = q_all.reshape(nblk, B_, HEAD_DIM)
    k3 = k_all.reshape(nblk, B_, HEAD_DIM)
    v3 = v_all.reshape(nblk, B_, HEAD_DIM)
    blk = lax.broadcasted_iota(jnp.int32, (nblk, B_, B_), 0)
    row = lax.broadcasted_iota(jnp.int32, (nblk, B_, B_), 1)
    col = lax.broadcasted_iota(jnp.int32, (nblk, B_, B_), 2)
    scale = HEAD_DIM ** -0.5
    dn_qk = (((2,), (2,)), ((0,), (0,)))
    dn_pv = (((2,), (1,)), ((0,), (0,)))
    s_c = lax.dot_general(q3, k3, dn_qk, preferred_element_type=F32) * scale + jnp.where(col <= row, 0.0, NEG)
    m = jnp.max(s_c, axis=2, keepdims=True)
    if per_class > 1:
        k_prev = jnp.concatenate([k3[:1], k3[:-1]], axis=0)
        v_prev = jnp.concatenate([v3[:1], v3[:-1]], axis=0)
        ok_prev = jnp.logical_and(col >= row, blk % per_class != 0)
        s_p = lax.dot_general(q3, k_prev, dn_qk, preferred_element_type=F32) * scale + jnp.where(ok_prev, 0.0, NEG)
        m = jnp.maximum(m, jnp.max(s_p, axis=2, keepdims=True))
    p_c = jnp.exp(s_c - m)
    den = jnp.sum(p_c, axis=2, keepdims=True)
    acc = lax.dot_general(p_c.astype(BF16), v3, dn_pv, preferred_element_type=F32)
    if per_class > 1:
        p_p = jnp.exp(s_p - m)
        den = den + jnp.sum(p_p, axis=2, keepdims=True)
        acc = acc + lax.dot_general(p_p.astype(BF16), v_prev, dn_pv, preferred_element_type=F32)
    out = (acc / den).reshape(S, HEAD_DIM)
    lse = m + jnp.log(den)
    my_lane = lax.broadcasted_iota(jnp.int32, (B_, LANES), 1) == h
    if d == 1:
        o_ref[0] = out.astype(o_ref.dtype)
    else:
        stage_ref[...] = out
    for n in range(nblk):
        sl = slice(n * B_, (n + 1) * B_)
        start = n // per_class + d * (n % per_class) * B_
        rows = pl.ds(start, B_, stride=d) if d > 1 else sl
        if d > 1:
            o_rm[rows, :] = stage_ref[sl, :]
        lse_ref[0, rows, :] = jnp.where(my_lane, lse[n], lse_ref[0, rows, :])
    if d > 1:
        o_ref[0] = o_rm[...].astype(o_ref.dtype)


def _attention(proj3, g):
    B, S, _ = proj3.shape
    Dh = HEAD_DIM

    def col(base):
        return pl.BlockSpec((1, S, Dh), lambda b, h: (b, 0, (base + g * ATTN_OUT) // Dh + h))

    return pl.pallas_call(
        functools.partial(_attn_kernel, dilation=GROUP_DILATION[g]),
        grid=(B, HEADS_PER_GROUP),
        in_specs=[col(COL_AQ), col(COL_AK), col(COL_AV)],
        out_specs=[pl.BlockSpec((1, S, Dh), lambda b, h: (b, 0, h)),
                   pl.BlockSpec((1, S, LANES), lambda b, h: (b, 0, 0))],
        out_shape=[jax.ShapeDtypeStruct((B, S, ATTN_OUT), BF16), jax.ShapeDtypeStruct((B, S, LANES), F32)],
        scratch_shapes=[pltpu.VMEM((S, Dh), F32), pltpu.VMEM((S, Dh), BF16), pltpu.VMEM((S, Dh), BF16),
                        pltpu.VMEM((S, Dh), BF16), pltpu.VMEM((S, Dh), F32)],
        compiler_params=_params(("parallel", "arbitrary")),
        name=f"attn{g}",
    )(proj3, proj3, proj3)


def _mlstm_kernel(bif_ref, mq_ref, mk_ref, mv_ref, mo_ref, g_ref, cwq_ref, cwk_ref, gn_ref, o_ref,
                  pad_ref, qs_ref, ks_ref, c_ref, n_ref, m_ref, hq_ref, hk_ref):
    NB, SEG = mq_ref.shape[0], mq_ref.shape[1]
    LC = MLSTM_CHUNK
    Dh = M_HEAD_DIM
    H = M_HEADS
    seg = pl.program_id(1)

    @pl.when(seg == 0)
    def _():
        c_ref[...] = jnp.zeros_like(c_ref)
        n_ref[...] = jnp.zeros_like(n_ref)
        m_ref[...] = jnp.zeros_like(m_ref)
        hq_ref[...] = jnp.zeros_like(hq_ref)
        hk_ref[...] = jnp.zeros_like(hk_ref)

    def conv_silu(src_ref, hist_ref, w_ref, bb, cols, dst_ref, scale):
        pad_ref[0:SUBLANES, :] = hist_ref[bb, :, cols]
        pad_ref[SUBLANES:SUBLANES + SEG, :] = src_ref[bb, :, cols].astype(F32)
        hist_ref[bb, :, cols] = pad_ref[SEG:SEG + SUBLANES, :]
        rb = 2 * LC
        for r0 in range(0, SEG, rb):
            acc = None
            for j in range(CONV_WIDTH):
                off = r0 + SUBLANES - (CONV_WIDTH - 1) + j
                term = w_ref[j:j + 1, cols] * pad_ref[off:off + rb, :]
                acc = term if acc is None else acc + term
            dst_ref[bb, r0:r0 + rb, cols] = (acc * _sigmoid(acc) * scale).astype(dst_ref.dtype)

    for bb in range(NB):
        for hd in range(H):
            cols = slice(hd * Dh, (hd + 1) * Dh)
            conv_silu(mq_ref, hq_ref, cwq_ref, bb, cols, qs_ref, 1.0)
            conv_silu(mk_ref, hk_ref, cwk_ref, bb, cols, ks_ref, Dh ** -0.5)

    row = lax.broadcasted_iota(jnp.int32, (LC, LC), 0)
    col = lax.broadcasted_iota(jnp.int32, (LC, LC), 1)
    causal = col <= row
    eye = col == row
    lane8 = lax.broadcasted_iota(jnp.int32, (SUBLANES, LC), 1)

    def head_chunk(bb, hd, c, r0, m):
        cols = slice(hd * Dh, (hd + 1) * Dh)
        c_ref_h = c_ref.at[bb * H + hd]
        gn = gn_ref[:, cols]
        q = qs_ref[bb, pl.ds(r0, LC), cols]
        k = ks_ref[bb, pl.ds(r0, LC), cols]
        v = mv_ref[bb, pl.ds(r0, LC), cols]
        i_row = g_ref[bb, 0, hd, pl.ds(c, 1), :] + bif_ref[hd]
        f_row = g_ref[bb, 0, H + hd, pl.ds(c, 1), :] + bif_ref[H + hd]
        logf = jnp.minimum(f_row, 0.0) - jnp.log(1.0 + jnp.exp(-jnp.abs(f_row)))
        b8 = jnp.broadcast_to(logf, (SUBLANES, LC))
        s = 1
        while s < LC:
            b8 = b8 + jnp.where(lane8 >= s, pltpu.roll(b8, s, 1), 0.0)
            s *= 2
        b_row = b8[0:1, :]
        d_row = i_row - b_row
        b_col = jnp.sum(jnp.where(eye, b_row, 0.0), axis=1, keepdims=True)
        d_col = jnp.sum(jnp.where(eye, d_row, 0.0), axis=1, keepdims=True)
        dm = jnp.where(causal, b_col + d_row, NEG)
        inter = b_col + m
        m_t = jnp.maximum(inter, jnp.max(dm, axis=1, keepdims=True))
        w_intra = jnp.exp(dm - m_t)
        w_inter = jnp.exp(inter - m_t)
        kb = k.astype(BF16)
        a = lax.dot_general(q, kb, (((1,), (1,)), ((), ())), preferred_element_type=F32) * w_intra
        num = (jnp.dot(a.astype(BF16), v, preferred_element_type=F32)
               + w_inter * jnp.dot(q, c_ref_h[...].astype(BF16), preferred_element_type=F32))
        den = (jnp.sum(a, axis=1, keepdims=True)
               + w_inter * jnp.sum(q.astype(F32) * n_ref[bb:bb + 1, cols], axis=1, keepdims=True))
        hh = num / jnp.maximum(jnp.abs(den), jnp.exp(-m_t))
        mu = jnp.mean(hh, axis=1, keepdims=True)
        xc = hh - mu
        var = jnp.mean(xc * xc, axis=1, keepdims=True)
        hn = xc * lax.rsqrt(var + LN_EPS) * gn
        gate = _sigmoid(mo_ref[bb, pl.ds(r0, LC), cols].astype(F32))
        o_ref[bb, pl.ds(r0, LC), cols] = (gate * hn).astype(o_ref.dtype)
        b_last = b_row[:, LC - 1:LC]
        m_new = jnp.maximum(b_last + m, jnp.max(b_last + d_row, axis=1, keepdims=True))
        decay = jnp.exp(b_last + m - m_new)
        kw = k * jnp.exp(b_last + d_col - m_new)
        c_ref_h[...] = decay * c_ref_h[...] + lax.dot_general(
            kw.astype(BF16), v, (((0,), (0,)), ((), ())), preferred_element_type=F32)
        n_ref[bb:bb + 1, cols] = decay * n_ref[bb:bb + 1, cols] + jnp.sum(kw, axis=0, keepdims=True)
        return m_new

    def chunk(c, ms):
        r0 = pl.multiple_of(c * LC, LC)
        return tuple(head_chunk(bb, hd, c, r0, ms[bb * H + hd]) for bb in range(NB) for hd in range(H))

    ms = lax.fori_loop(0, SEG // LC, chunk,
                       tuple(m_ref[bb:bb + 1, hd:hd + 1] for bb in range(NB) for hd in range(H)))
    for bb in range(NB):
        for hd in range(H):
            m_ref[bb:bb + 1, hd:hd + 1] = ms[bb * H + hd]


def _mlstm(proj3, gates, conv_q, conv_k, b_if, g_norm):
    B, S, _ = proj3.shape
    W = M_WIDTH
    SEG = MLSTM_SEGMENT
    NB = MLSTM_BATCH
    nseg = S // SEG
    nC = SEG // MLSTM_CHUNK

    def col(base):
        return pl.BlockSpec((NB, SEG, W), lambda b, s, pre: (b, s, base // W))

    full = lambda a: pl.BlockSpec(a.shape, lambda b, s, pre: (0,) * a.ndim)
    grid_spec = pltpu.PrefetchScalarGridSpec(
        num_scalar_prefetch=1,
        grid=(B // NB, nseg),
        in_specs=[
            col(COL_MQ), col(COL_MK), col(COL_MV), col(COL_MO),
            pl.BlockSpec((NB, 1, 2 * M_HEADS, nC, MLSTM_CHUNK), lambda b, s, pre: (b, s, 0, 0, 0)),
            full(conv_q), full(conv_k), full(g_norm),
        ],
        out_specs=pl.BlockSpec((NB, SEG, W), lambda b, s, pre: (b, s, 0)),
        scratch_shapes=[
            pltpu.VMEM((SEG + SUBLANES, M_HEAD_DIM), F32),
            pltpu.VMEM((NB, SEG, W), BF16),
            pltpu.VMEM((NB, SEG, W), F32),
            pltpu.VMEM((NB * M_HEADS, M_HEAD_DIM, M_HEAD_DIM), F32),
            pltpu.VMEM((NB, W), F32),
            pltpu.VMEM((NB, LANES), F32),
            pltpu.VMEM((NB, SUBLANES, W), F32),
            pltpu.VMEM((NB, SUBLANES, W), F32),
        ],
    )
    return pl.pallas_call(
        _mlstm_kernel,
        grid_spec=grid_spec,
        out_shape=jax.ShapeDtypeStruct((B, S, W), BF16),
        compiler_params=_params(("parallel", "arbitrary")),
        name="mlstm",
    )(b_if, proj3, proj3, proj3, proj3, gates, conv_q, conv_k, g_norm)


def _post_kernel(x_ref, ao0_ref, ao1_ref, ao2_ref, lse0_ref, lse1_ref, lse2_ref, mem_ref, ga_ref, gm_ref,
                 wa_ref, wm_ref, wo_ref, gf_ref, wr_ref, wrl_ref, br_ref, h1_ref, xn_ref, idx_ref, gate_ref):
    tm = x_ref.shape[0]
    ao_refs = (ao0_ref, ao1_ref, ao2_ref)
    lses = (lse0_ref[...], lse1_ref[...], lse2_ref[...])
    heads = []
    for h in range(HEADS_PER_GROUP):
        ls = [l[:, h:h + 1] for l in lses]
        mx = jnp.maximum(jnp.maximum(ls[0], ls[1]), ls[2])
        es = [jnp.exp(l - mx) for l in ls]
        tot = es[0] + es[1] + es[2]
        acc = None
        for g in range(N_GROUPS):
            term = (es[g] / tot) * ao_refs[g][:, h * HEAD_DIM:(h + 1) * HEAD_DIM].astype(F32)
            acc = term if acc is None else acc + term
        heads.append(acc.astype(BF16))
    attn = jnp.concatenate(heads, axis=1)
    ya = jnp.dot(attn, wa_ref[...], preferred_element_type=F32)
    ym = jnp.dot(mem_ref[...], wm_ref[...], preferred_element_type=F32)
    merged = _sigmoid(ga_ref[...].astype(F32)) * ya + _sigmoid(gm_ref[...].astype(F32)) * ym
    h1 = x_ref[...] + jnp.dot(merged.astype(BF16), wo_ref[...], preferred_element_type=F32)
    h1_ref[...] = h1
    ms = jnp.mean(h1 * h1, axis=-1, keepdims=True)
    xn = h1 * lax.rsqrt(ms + RMS_EPS) * gf_ref[...]
    for s in range(SUBLANES):
        xn_ref[pl.ds(s, tm, stride=SUBLANES), :] = xn[:, s * LANES:(s + 1) * LANES]
    xn_hi = xn.astype(BF16)
    xn_lo = (xn - xn_hi.astype(F32)).astype(BF16)
    logits = (jnp.dot(xn_hi, wr_ref[...], preferred_element_type=F32)
              + jnp.dot(xn_hi, wrl_ref[...], preferred_element_type=F32)
              + jnp.dot(xn_lo, wr_ref[...], preferred_element_type=F32)) + br_ref[...]
    lane = lax.broadcasted_iota(jnp.int32, (tm, LANES), 1).astype(F32)
    vals = logits
    tops, idxs = [], []
    for _ in range(TOP_K):
        mx = jnp.max(vals, axis=1, keepdims=True)
        ix = jnp.min(jnp.where(vals == mx, lane, float(LANES)), axis=1, keepdims=True)
        tops.append(mx)
        idxs.append(ix)
        vals = jnp.where(lane == ix, NEG, vals)
    es = [jnp.exp(t - tops[0]) for t in tops]
    tot = es[0] + es[1] + es[2] + es[3]
    idx_out = jnp.zeros((tm, LANES), F32)
    gate_out = jnp.zeros((tm, LANES), F32)
    for kk in range(TOP_K):
        idx_out = jnp.where(lane == float(kk), idxs[kk], idx_out)
        gate_out = jnp.where(lane == float(kk), es[kk] / tot, gate_out)
    idx_ref[...] = idx_out.astype(jnp.int32)
    gate_ref[...] = gate_out


def _post(x2, aos, lses, mem, proj, w_a, w_m, w_o, g_ffn, w_r, w_r_lo, b_r):
    T, D = x2.shape
    tm = POST_TM
    row = lambda w: pl.BlockSpec((tm, w), lambda i: (i, 0))
    full = lambda a: pl.BlockSpec(a.shape, lambda i: (0,) * a.ndim)
    return pl.pallas_call(
        _post_kernel,
        grid=(T // tm,),
        in_specs=[
            row(D), row(ATTN_OUT), row(ATTN_OUT), row(ATTN_OUT), row(LANES), row(LANES), row(LANES), row(M_WIDTH),
            pl.BlockSpec((tm, D), lambda i: (i, COL_GA // D)),
            pl.BlockSpec((tm, D), lambda i: (i, COL_GM // D)),
            full(w_a), full(w_m), full(w_o), full(g_ffn), full(w_r), full(w_r_lo), full(b_r),
        ],
        out_specs=[row(D), pl.BlockSpec((tm * SUBLANES, LANES), lambda i: (i, 0)), row(LANES), row(LANES)],
        out_shape=[jax.ShapeDtypeStruct((T, D), F32), jax.ShapeDtypeStruct((T * SUBLANES, LANES), F32),
                   jax.ShapeDtypeStruct((T, LANES), jnp.int32), jax.ShapeDtypeStruct((T, LANES), F32)],
        compiler_params=_params(("parallel",)),
        name="post",
    )(x2, *aos, *lses, mem, proj, proj, w_a, w_m, w_o, g_ffn, w_r, w_r_lo, b_r)


def _expert_kernel(be_ref, ib_ref, lo_ref, hi_ref, nu_ref, idx_hbm, xn_hbm, wgu_ref, wd_ref, bgu_ref, bd_ref, perm_ref,
                   y_hbm, idx_smem, xbuf, ybuf, xb_ref, wgu_bf, wd_bf, isem, gsem, ssem):
    i = pl.program_id(0)
    n_idx = ib_ref.shape[0]
    n_used = nu_ref[0]
    BLK = MOE_BLOCK
    n_islots = idx_smem.shape[0]
    n_bufs = xbuf.shape[0]

    def tile(t):
        return pl.ds(pl.multiple_of(t * SUBLANES, SUBLANES), SUBLANES)

    def tiles(n):
        return pl.ds(0, pl.multiple_of(n * SUBLANES, SUBLANES))

    def idx_copy(blk):
        slot = blk % n_islots
        return pltpu.make_async_copy(idx_hbm.at[ib_ref[blk]], idx_smem.at[slot], isem.at[slot])

    def gather_start(blk):
        islot, bslot = blk % n_islots, blk % n_bufs
        for r in range(BLK):
            tok = idx_smem[islot, r]
            pltpu.make_async_copy(xn_hbm.at[tile(tok)], xbuf.at[bslot, pl.ds(r * SUBLANES, SUBLANES)],
                                  gsem.at[bslot]).start()

    def gather_wait(blk):
        bslot = blk % n_bufs
        pltpu.make_async_copy(xn_hbm.at[tiles(BLK)], xbuf.at[bslot], gsem.at[bslot]).wait()

    def row_range(blk):
        ok = blk >= 0
        b = jnp.maximum(blk, 0)
        return jnp.where(ok, lo_ref[b], 0), jnp.where(ok, hi_ref[b], 0)

    def scatter_start(blk):
        islot, bslot = blk % n_islots, blk % n_bufs
        lo, hi = row_range(blk)
        n = (hi - lo).astype(jnp.uint32)
        for r in range(BLK):
            @pl.when((r - lo).astype(jnp.uint32) < n)
            def _():
                dst = idx_smem[islot, BLK + r]
                pltpu.make_async_copy(ybuf.at[bslot, pl.ds(r * SUBLANES, SUBLANES)], y_hbm.at[tile(dst)],
                                      ssem.at[bslot]).start()

    def scatter_wait(blk):
        bslot = blk % n_bufs
        lo, hi = row_range(blk)
        n = hi - lo

        @pl.when(n > 0)
        def _():
            pltpu.make_async_copy(ybuf.at[bslot, tiles(n)], y_hbm.at[tiles(n)], ssem.at[bslot]).wait()

    @pl.when(i == 0)
    def _():
        idx_copy(0).start()
        idx_copy(1).start()
        idx_copy(2).start()
        idx_copy(0).wait()
        idx_copy(1).wait()
        gather_start(0)
        gather_start(1)

    @pl.when(i + 3 < n_idx)
    def _():
        idx_copy(i + 3).start()

    @pl.when(i + 2 < n_idx)
    def _():
        idx_copy(i + 2).wait()

    @pl.when(jnp.logical_and(i < n_used, jnp.logical_or(i == 0, be_ref[i] != be_ref[jnp.maximum(i - 1, 0)])))
    def _():
        pw = perm_ref.shape[0]
        for c in range(wgu_ref.shape[2] // pw):
            cols = slice(c * pw, (c + 1) * pw)
            wgu_bf[:, cols] = jnp.dot(wgu_ref[0, :, cols].astype(BF16), perm_ref[...],
                                      preferred_element_type=F32).astype(BF16)
        wd_bf[...] = wd_ref[0].astype(BF16)

    @pl.when(i < n_used)
    def _():
        gather_wait(i)
        scatter_wait(i - 3)

    @pl.when(i < n_used)
    def _():
        bslot = i % n_bufs
        for s in range(SUBLANES):
            xb_ref[:, s * LANES:(s + 1) * LANES] = xbuf[bslot, pl.ds(s, BLK, stride=SUBLANES), :].astype(BF16)
        gather_start(i + 2)
        scatter_start(i - 1)
        hg = jnp.dot(xb_ref[...], wgu_bf[...], preferred_element_type=F32) + bgu_ref[0]
        acts = []
        for c in range(hg.shape[1] // (2 * LANES)):
            gate = jnp.minimum(hg[:, 2 * c * LANES:(2 * c + 1) * LANES], SWIGLU_LIMIT)
            up = jnp.clip(hg[:, (2 * c + 1) * LANES:(2 * c + 2) * LANES], -SWIGLU_LIMIT, SWIGLU_LIMIT)
            acts.append(((up + 1.0) * (gate * _sigmoid(SWIGLU_ALPHA * gate))).astype(BF16))
        act = jnp.concatenate(acts, axis=1)
        y = jnp.dot(act, wd_bf[...], preferred_element_type=F32) + bd_ref[0]
        for s in range(SUBLANES):
            ybuf[bslot, pl.ds(s, BLK, stride=SUBLANES), :] = y[:, s * LANES:(s + 1) * LANES]

    @pl.when(i == n_used)
    def _():
        gather_wait(i)
        gather_wait(i + 1)
        scatter_start(i - 1)
        scatter_wait(i - 3)
        scatter_wait(i - 2)
        scatter_wait(i - 1)


def _experts(item_e, item_blk, item_lo, item_hi, n_used, idx, xn2, wgu, wd, bgu, bd, perm, n_out_rows):
    nb = item_e.shape[0] - 1
    D = wd.shape[2]
    F2 = wgu.shape[2]
    emap = lambda i, be, ib, lo, hi, nu: (be[i], 0, 0)
    grid_spec = pltpu.PrefetchScalarGridSpec(
        num_scalar_prefetch=5,
        grid=(nb,),
        in_specs=[
            pl.BlockSpec(memory_space=pl.ANY),
            pl.BlockSpec(memory_space=pl.ANY),
            pl.BlockSpec((1, D, F2), emap),
            pl.BlockSpec((1, F2 // 2, D), emap),
            pl.BlockSpec((1, 1, F2), emap),
            pl.BlockSpec((1, 1, D), emap),
            pl.BlockSpec(perm.shape, lambda i, be, ib, lo, hi, nu: (0, 0)),
        ],
        out_specs=pl.BlockSpec(memory_space=pl.ANY),
        scratch_shapes=[
            pltpu.SMEM((8, 2 * MOE_BLOCK), jnp.int32),
            pltpu.VMEM((3, MOE_BLOCK * SUBLANES, LANES), F32),
            pltpu.VMEM((3, MOE_BLOCK * SUBLANES, LANES), F32),
            pltpu.VMEM((MOE_BLOCK, D), BF16),
            pltpu.VMEM((D, F2), BF16),
            pltpu.VMEM((F2 // 2, D), BF16),
            pltpu.SemaphoreType.DMA((8,)),
            pltpu.SemaphoreType.DMA((3,)),
            pltpu.SemaphoreType.DMA((3,)),
        ],
    )
    return pl.pallas_call(
        _expert_kernel,
        grid_spec=grid_spec,
        out_shape=jax.ShapeDtypeStruct((n_out_rows * SUBLANES, LANES), F32),
        compiler_params=_params(("arbitrary",)),
        name="experts",
    )(item_e, item_blk, item_lo, item_hi, n_used, idx, xn2, wgu, wd, bgu, bd, perm)


def _final_kernel(h1_ref, y0_ref, y1_ref, y2_ref, y3_ref, gate_ref, g_ref, o_ref):
    tm = h1_ref.shape[0]
    gts = gate_ref[...]
    g4 = [gts[:, kk:kk + 1] for kk in range(TOP_K)]
    pieces = []
    ss = None
    for s in range(SUBLANES):
        h = h1_ref[:, s * LANES:(s + 1) * LANES]
        for kk, y_ref in enumerate((y0_ref, y1_ref, y2_ref, y3_ref)):
            h = h + g4[kk] * y_ref[pl.ds(s, tm, stride=SUBLANES), :]
        pieces.append(h)
        sq = jnp.sum(h * h, axis=-1, keepdims=True)
        ss = sq if ss is None else ss + sq
    inv = lax.rsqrt(ss / h1_ref.shape[1] + RMS_EPS)
    for s in range(SUBLANES):
        o_ref[:, s * LANES:(s + 1) * LANES] = pieces[s] * inv * g_ref[:, s * LANES:(s + 1) * LANES]


def _final(h1, y4, gates, g_final):
    T, D = h1.shape
    tm = FINAL_TM
    nt = T // tm
    yspec = lambda kk: pl.BlockSpec((tm * SUBLANES, LANES), lambda i: (kk * nt + i, 0))
    return pl.pallas_call(
        _final_kernel,
        grid=(nt,),
        in_specs=[pl.BlockSpec((tm, D), lambda i: (i, 0)), yspec(0), yspec(1), yspec(2), yspec(3),
                  pl.BlockSpec((tm, LANES), lambda i: (i, 0)), pl.BlockSpec((1, D), lambda i: (0, 0))],
        out_specs=pl.BlockSpec((tm, D), lambda i: (i, 0)),
        out_shape=jax.ShapeDtypeStruct((T, D), F32),
        compiler_params=_params(("parallel",)),
        name="final",
    )(h1, y4, y4, y4, y4, gates, g_final)


def _rope_tables(positions):
    half = ROPE_DIM // 2
    inv_freq = ROPE_THETA ** (-jnp.arange(half, dtype=F32) / half)
    ang = positions.astype(F32).reshape(-1)[:, None] * inv_freq
    cos, sin = jnp.cos(ang), jnp.sin(ang)
    T = ang.shape[0]
    cos_t = jnp.concatenate([cos, cos, jnp.ones((T, HEAD_DIM - ROPE_DIM), F32)], axis=1)
    sa_t = jnp.concatenate([-sin, jnp.zeros((T, HEAD_DIM - half), F32)], axis=1)
    sb_t = jnp.concatenate([jnp.zeros((T, half), F32), sin, jnp.zeros((T, HEAD_DIM - ROPE_DIM), F32)], axis=1)
    return cos_t, sa_t, sb_t


def _routing(top_idx, T):
    A = T * TOP_K
    BLK = MOE_BLOCK
    n_blocks = A // BLK
    n_items = n_blocks + N_EXPERTS
    flat_e = top_idx.reshape(A)
    row_a = (jnp.sort(flat_e * A + jnp.arange(A, dtype=jnp.int32)) % A).reshape(n_blocks, BLK)
    row_tok = row_a // TOP_K
    row_slot = (row_a % TOP_K) * T + row_tok
    idx = jnp.concatenate([row_tok, row_slot], axis=1)
    experts = jnp.arange(N_EXPERTS, dtype=jnp.int32)
    counts = jnp.sum((flat_e[None, :] == experts[:, None]).astype(jnp.int32), axis=1)
    ends = jnp.cumsum(counts)
    starts = ends - counts
    first_blk = starts // BLK
    per_expert = jnp.where(counts > 0, (ends - 1) // BLK - first_blk + 1, 0)
    item_ends = jnp.cumsum(per_expert)
    item_starts = item_ends - per_expert
    n_used = item_ends[-1].astype(jnp.int32).reshape(1)
    it = jnp.arange(n_items + 2, dtype=jnp.int32)
    item_e = jnp.minimum(jnp.sum((item_ends[None, :] <= it[:, None]).astype(jnp.int32), axis=1), N_EXPERTS - 1)
    item_blk = jnp.clip(first_blk[item_e] + it - item_starts[item_e], 0, n_blocks - 1)
    live = it < item_ends[-1]
    item_lo = jnp.where(live, jnp.clip(starts[item_e] - item_blk * BLK, 0, BLK), 0).astype(jnp.int32)
    item_hi = jnp.where(live, jnp.clip(ends[item_e] - item_blk * BLK, 0, BLK), 0).astype(jnp.int32)
    return item_e.astype(jnp.int32), item_blk.astype(jnp.int32), item_lo, item_hi, n_used, idx, A


def kernel(x, positions, g_mix, w_in, conv_qk, b_if, g_mlstm_norm, w_attn_out, w_mlstm_out, w_mix_out, g_ffn,
           w_router, b_router, w_gate_up, b_gate_up, w_down, b_down, g_final):
    B, S, D = x.shape
    T = B * S
    l = 0
    x2 = x.reshape(T, D)

    w = w_in[l]
    o = np.cumsum((0, ATTN_WIDTH, ATTN_WIDTH, ATTN_WIDTH, M_WIDTH, M_WIDTH, M_WIDTH, M_WIDTH, M_HEADS, M_HEADS,
                   D_MODEL, D_MODEL))
    w_bf = w.astype(BF16)
    seg = lambda a: w_bf[:, int(o[a]):int(o[a + 1])]
    w_main = jnp.concatenate([seg(0), seg(1), seg(9), seg(10), seg(3), seg(4), seg(5), seg(6), seg(2)], axis=1)
    w_if = jnp.pad(w[:, int(o[7]):int(o[9])], ((0, 0), (0, LANES - 2 * M_HEADS))).astype(BF16)
    cos_t, sa_t, sb_t = _rope_tables(positions)

    proj, ifo = _proj(x2, g_mix[l][None, :], w_main, w_if, cos_t, sa_t, sb_t)

    proj3 = proj.reshape(B, S, N_MAIN)
    attn_out = [_attention(proj3, g) for g in range(N_GROUPS)]
    aos = [o.reshape(T, ATTN_OUT) for o, _ in attn_out]
    lses = [l.reshape(T, LANES) for _, l in attn_out]

    gates = ifo[:, :2 * M_HEADS].reshape(B, S // MLSTM_SEGMENT, MLSTM_SEGMENT // MLSTM_CHUNK, MLSTM_CHUNK,
                                         2 * M_HEADS).transpose(0, 1, 4, 2, 3)
    mem = _mlstm(proj3, gates, conv_qk[l][:, :M_WIDTH], conv_qk[l][:, M_WIDTH:], b_if[l],
                 g_mlstm_norm[l][None, :])

    w_r32 = jnp.pad(w_router[l], ((0, 0), (0, LANES - N_EXPERTS)))
    w_r = w_r32.astype(BF16)
    w_r_lo = (w_r32 - w_r.astype(F32)).astype(BF16)
    b_r = jnp.concatenate([b_router[l], jnp.full((LANES - N_EXPERTS,), NEG, F32)])[None, :]
    h1, xn2, top_idx, gates4 = _post(x2, aos, lses, mem.reshape(T, M_WIDTH), proj, w_attn_out[l].astype(BF16),
                                     w_mlstm_out[l].astype(BF16), w_mix_out[l].astype(BF16), g_ffn[l][None, :],
                                     w_r, w_r_lo, b_r)

    item_e, item_blk, item_lo, item_hi, n_used, idx, n_out_rows = _routing(top_idx[:, :TOP_K], T)
    pair = np.arange(LANES)
    perm_np = np.zeros((2 * LANES, 2 * LANES), np.float32)
    perm_np[2 * pair, pair] = 1.0
    perm_np[2 * pair + 1, LANES + pair] = 1.0
    bgu = b_gate_up[l].reshape(N_EXPERTS, 1, -1, LANES, 2).swapaxes(-1, -2).reshape(N_EXPERTS, 1, 2 * D_FF)
    y4 = _experts(item_e, item_blk, item_lo, item_hi, n_used, idx, xn2, w_gate_up[l], w_down[l], bgu,
                  b_down[l][:, None, :], jnp.asarray(perm_np, BF16), n_out_rows)

    out = _final(h1, y4, gates4, g_final[None, :])
    return out.reshape(B, S, D)
```

```python
import functools

import jax
import jax.numpy as jnp
import numpy as np
from jax import lax
from jax.experimental import pallas as pl
from jax.experimental.pallas import tpu as pltpu

F32 = jnp.float32
BF16 = jnp.bfloat16

D_MODEL = 1024
N_GROUPS = 3
GROUP_DILATION = (1, 4, 16)
HEADS_PER_GROUP = 4
HEAD_DIM = 128
ATTN_WIDTH = N_GROUPS * HEADS_PER_GROUP * HEAD_DIM
ATTN_OUT = HEADS_PER_GROUP * HEAD_DIM
ATTN_BLOCK = 128
ROPE_THETA = 500000.0
ROPE_DIM = HEAD_DIM // 4
M_HEADS = 4
M_WIDTH = D_MODEL
M_HEAD_DIM = M_WIDTH // M_HEADS
CONV_WIDTH = 4
N_EXPERTS = 32
TOP_K = 4
D_FF = D_MODEL
SWIGLU_LIMIT = 7.0
SWIGLU_ALPHA = 1.702
RMS_EPS = 1e-5
LN_EPS = 1e-5

LANES = 128
SUBLANES = 8
VMEM_LIMIT = 56 * 1024 * 1024

PROJ_TM = 1024
PROJ_TN = 1536
MLSTM_CHUNK = 128
MLSTM_SEGMENT = 1024
POST_TM = 512
MOE_BLOCK = 512
FINAL_TM = 512
NEG = -1e30

COL_AQ = 0
COL_AK = ATTN_WIDTH
COL_GA = 2 * ATTN_WIDTH
COL_GM = COL_GA + D_MODEL
COL_MQ = COL_GM + D_MODEL
COL_MK = COL_MQ + M_WIDTH
COL_MV = COL_MK + M_WIDTH
COL_MO = COL_MV + M_WIDTH
COL_AV = COL_MO + M_WIDTH
N_MAIN = COL_AV + ATTN_WIDTH


def _sigmoid(x):
    return 0.5 * jnp.tanh(0.5 * x) + 0.5


def _params(sem):
    return pltpu.CompilerParams(dimension_semantics=sem, vmem_limit_bytes=VMEM_LIMIT)


def _proj_kernel(x_ref, g_ref, w_ref, wif_ref, cos_ref, sa_ref, sb_ref, o_ref, if_ref, xn_ref, *, rope_tiles):
    j = pl.program_id(1)

    @pl.when(j == 0)
    def _():
        x = x_ref[...]
        ms = jnp.mean(x * x, axis=-1, keepdims=True)
        xn = (x * lax.rsqrt(ms + RMS_EPS) * g_ref[...]).astype(BF16)
        xn_ref[...] = xn
        if_ref[...] = jnp.dot(xn, wif_ref[...], preferred_element_type=F32)

    piece = 4 * HEAD_DIM
    n_pieces = o_ref.shape[1] // piece

    @pl.when(j < rope_tiles)
    def _():
        c = cos_ref[...]
        sa = sa_ref[...]
        sb = sb_ref[...]
        for p in range(n_pieces):
            acc = jnp.dot(xn_ref[...], w_ref[:, p * piece:(p + 1) * piece], preferred_element_type=F32)
            for h in range(piece // HEAD_DIM):
                a = acc[:, h * HEAD_DIM:(h + 1) * HEAD_DIM]
                y = a * c + pltpu.roll(a, HEAD_DIM - ROPE_DIM // 2, 1) * sa + pltpu.roll(a, ROPE_DIM // 2, 1) * sb
                c0 = p * piece + h * HEAD_DIM
                o_ref[:, c0:c0 + HEAD_DIM] = y.astype(o_ref.dtype)

    @pl.when(j >= rope_tiles)
    def _():
        for p in range(n_pieces):
            cols = slice(p * piece, (p + 1) * piece)
            o_ref[:, cols] = jnp.dot(xn_ref[...], w_ref[:, cols], preferred_element_type=F32).astype(o_ref.dtype)


def _proj(x2, g_mix, w_main, w_if, cos_t, sa_t, sb_t):
    T, D = x2.shape
    N = w_main.shape[1]
    tm, tn = PROJ_TM, PROJ_TN
    grid = (T // tm, N // tn)
    return pl.pallas_call(
        functools.partial(_proj_kernel, rope_tiles=(2 * ATTN_WIDTH) // tn),
        grid=grid,
        in_specs=[
            pl.BlockSpec((tm, D), lambda i, j: (i, 0)),
            pl.BlockSpec((1, D), lambda i, j: (0, 0)),
            pl.BlockSpec((D, tn), lambda i, j: (0, j)),
            pl.BlockSpec((D, LANES), lambda i, j: (0, 0)),
            pl.BlockSpec((tm, LANES), lambda i, j: (i, 0)),
            pl.BlockSpec((tm, LANES), lambda i, j: (i, 0)),
            pl.BlockSpec((tm, LANES), lambda i, j: (i, 0)),
        ],
        out_specs=[
            pl.BlockSpec((tm, tn), lambda i, j: (i, j)),
            pl.BlockSpec((tm, LANES), lambda i, j: (i, 0)),
        ],
        out_shape=[jax.ShapeDtypeStruct((T, N), BF16), jax.ShapeDtypeStruct((T, LANES), F32)],
        scratch_shapes=[pltpu.VMEM((tm, D), BF16)],
        compiler_params=_params(("parallel", "arbitrary")),
        name="proj",
    )(x2, g_mix, w_main, w_if, cos_t, sa_t, sb_t)


def _attn_kernel(q_ref, k_ref, v_ref, o_ref, lse_ref, stage_ref, q_rm, k_rm, v_rm, o_rm, *, dilation):
    S = q_ref.shape[1]
    d = dilation
    L = S // d
    per_class = L // ATTN_BLOCK
    nblk = S // ATTN_BLOCK
    h = pl.program_id(1)

    if d == 1:
        q_all, k_all, v_all = q_ref[0], k_ref[0], v_ref[0]
    else:
        for src, dst in ((q_ref, q_rm), (k_ref, k_rm), (v_ref, v_rm)):
            stage_ref[...] = src[0].astype(F32)
            if d <= SUBLANES:
                for r in range(d):
                    dst[r * L:(r + 1) * L, :] = stage_ref[pl.ds(r, L, stride=d), :].astype(BF16)
            else:
                d0 = int(round(d ** 0.5))
                assert d0 * d0 == d and d0 <= SUBLANES
                for a in range(d0):
                    o_rm[a * (S // d0):(a + 1) * (S // d0), :] = stage_ref[pl.ds(a, S // d0, stride=d0), :]
                for a in range(d0):
                    for b in range(d0):
                        r = a + d0 * b
                        dst[r * L:(r + 1) * L, :] = o_rm[pl.ds(a * (S // d0) + b, L, stride=d0), :].astype(BF16)
        q_all, k_all, v_all = q_rm[...], k_rm[...], v_rm[...]

    @pl.when(h == 0)
    def _():
        lse_ref[...] = jnp.zeros_like(lse_ref)

    B_ = ATTN_BLOCK
    q3 = q_all.reshape(nblk, B_, HEAD_DIM)
    k3 = k_all.reshape(nblk, B_, HEAD_DIM)
    v3 = v_all.reshape(nblk, B_, HEAD_DIM)
    blk = lax.broadcasted_iota(jnp.int32, (nblk, B_, B_), 0)
    row = lax.broadcasted_iota(jnp.int32, (nblk, B_, B_), 1)
    col = lax.broadcasted_iota(jnp.int32, (nblk, B_, B_), 2)
    scale = HEAD_DIM ** -0.5
    dn_qk = (((2,), (2,)), ((0,), (0,)))
    dn_pv = (((2,), (1,)), ((0,), (0,)))
    s_c = lax.dot_general(q3, k3, dn_qk, preferred_element_type=F32) * scale + jnp.where(col <= row, 0.0, NEG)
    m = jnp.max(s_c, axis=2, keepdims=True)
    if per_class > 1:
        k_prev = jnp.concatenate([k3[:1], k3[:-1]], axis=0)
        v_prev = jnp.concatenate([v3[:1], v3[:-1]], axis=0)
        ok_prev = jnp.logical_and(col >= row, blk % per_class != 0)
        s_p = lax.dot_general(q3, k_prev, dn_qk, preferred_element_type=F32) * scale + jnp.where(ok_prev, 0.0, NEG)
        m = jnp.maximum(m, jnp.max(s_p, axis=2, keepdims=True))
    p_c = jnp.exp(s_c - m)
    den = jnp.sum(p_c, axis=2, keepdims=True)
    acc = lax.dot_general(p_c.astype(BF16), v3, dn_pv, preferred_element_type=F32)
    if per_class > 1:
        p_p = jnp.exp(s_p - m)
        den = den + jnp.sum(p_p, axis=2, keepdims=True)
        acc = acc + lax.dot_general(p_p.astype(BF16), v_prev, dn_pv, preferred_element_type=F32)
    out = (acc / den).reshape(S, HEAD_DIM)
    lse = m + jnp.log(den)
    my_lane = lax.broadcasted_iota(jnp.int32, (B_, LANES), 1) == h
    if d == 1:
        o_ref[0] = out.astype(o_ref.dtype)
    else:
        stage_ref[...] = out
    for n in range(nblk):
        sl = slice(n * B_, (n + 1) * B_)
        start = n // per_class + d * (n % per_class) * B_
        rows = pl.ds(start, B_, stride=d) if d > 1 else sl
        if d > 1:
            o_rm[rows, :] = stage_ref[sl, :]
        lse_ref[0, rows, :] = jnp.where(my_lane, lse[n], lse_ref[0, rows, :])
    if d > 1:
        o_ref[0] = o_rm[...].astype(o_ref.dtype)


def _attention(proj3, g):
    B, S, _ = proj3.shape
    Dh = HEAD_DIM

    def col(base):
        return pl.BlockSpec((1, S, Dh), lambda b, h: (b, 0, (base + g * ATTN_OUT) // Dh + h))

    return pl.pallas_call(
        functools.partial(_attn_kernel, dilation=GROUP_DILATION[g]),
        grid=(B, HEADS_PER_GROUP),
        in_specs=[col(COL_AQ), col(COL_AK), col(COL_AV)],
        out_specs=[pl.BlockSpec((1, S, Dh), lambda b, h: (b, 0, h)),
                   pl.BlockSpec((1, S, LANES), lambda b, h: (b, 0, 0))],
        out_shape=[jax.ShapeDtypeStruct((B, S, ATTN_OUT), BF16), jax.ShapeDtypeStruct((B, S, LANES), F32)],
        scratch_shapes=[pltpu.VMEM((S, Dh), F32), pltpu.VMEM((S, Dh), BF16), pltpu.VMEM((S, Dh), BF16),
                        pltpu.VMEM((S, Dh), BF16), pltpu.VMEM((S, Dh), F32)],
        compiler_params=_params(("parallel", "arbitrary")),
        name=f"attn{g}",
    )(proj3, proj3, proj3)


def _mlstm_kernel(bif_ref, mq_ref, mk_ref, mv_ref, mo_ref, g_ref, cwq_ref, cwk_ref, gn_ref, o_ref,
                  pad_ref, qs_ref, ks_ref, c_ref, n_ref, m_ref, hq_ref, hk_ref):
    SEG = mq_ref.shape[1]
    LC = MLSTM_CHUNK
    Dh = M_HEAD_DIM
    H = M_HEADS
    seg = pl.program_id(1)

    @pl.when(seg == 0)
    def _():
        c_ref[...] = jnp.zeros_like(c_ref)
        n_ref[...] = jnp.zeros_like(n_ref)
        m_ref[...] = jnp.zeros_like(m_ref)
        hq_ref[...] = jnp.zeros_like(hq_ref)
        hk_ref[...] = jnp.zeros_like(hk_ref)

    def conv_silu(src_ref, hist_ref, w_ref, cols, dst_ref, scale):
        pad_ref[0:SUBLANES, :] = hist_ref[:, cols]
        pad_ref[SUBLANES:SUBLANES + SEG, :] = src_ref[0, :, cols].astype(F32)
        hist_ref[:, cols] = pad_ref[SEG:SEG + SUBLANES, :]
        rb = LC
        for r0 in range(0, SEG, rb):
            acc = None
            for j in range(CONV_WIDTH):
                off = r0 + SUBLANES - (CONV_WIDTH - 1) + j
                term = w_ref[j:j + 1, cols] * pad_ref[off:off + rb, :]
                acc = term if acc is None else acc + term
            dst_ref[r0:r0 + rb, cols] = (acc * _sigmoid(acc) * scale).astype(dst_ref.dtype)

    for hd in range(H):
        cols = slice(hd * Dh, (hd + 1) * Dh)
        conv_silu(mq_ref, hq_ref, cwq_ref, cols, qs_ref, 1.0)
        conv_silu(mk_ref, hk_ref, cwk_ref, cols, ks_ref, Dh ** -0.5)

    row = lax.broadcasted_iota(jnp.int32, (LC, LC), 0)
    col = lax.broadcasted_iota(jnp.int32, (LC, LC), 1)
    causal = col <= row
    eye = col == row
    lane8 = lax.broadcasted_iota(jnp.int32, (SUBLANES, LC), 1)

    def head_chunk(hd, c, r0, m):
        cols = slice(hd * Dh, (hd + 1) * Dh)
        c_ref_h = c_ref.at[hd]
        gn = gn_ref[:, cols]
        q = qs_ref[pl.ds(r0, LC), cols]
        k = ks_ref[pl.ds(r0, LC), cols]
        v = mv_ref[0, pl.ds(r0, LC), cols]
        i_row = g_ref[0, hd, pl.ds(c, 1), :] + bif_ref[hd]
        f_row = g_ref[0, H + hd, pl.ds(c, 1), :] + bif_ref[H + hd]
        logf = jnp.minimum(f_row, 0.0) - jnp.log(1.0 + jnp.exp(-jnp.abs(f_row)))
        b8 = jnp.broadcast_to(logf, (SUBLANES, LC))
        s = 1
        while s < LC:
            b8 = b8 + jnp.where(lane8 >= s, pltpu.roll(b8, s, 1), 0.0)
            s *= 2
        b_row = b8[0:1, :]
        d_row = i_row - b_row
        b_col = jnp.sum(jnp.where(eye, b_row, 0.0), axis=1, keepdims=True)
        d_col = jnp.sum(jnp.where(eye, d_row, 0.0), axis=1, keepdims=True)
        dm = jnp.where(causal, b_col + d_row, NEG)
        inter = b_col + m
        m_t = jnp.maximum(inter, jnp.max(dm, axis=1, keepdims=True))
        w_intra = jnp.exp(dm - m_t)
        w_inter = jnp.exp(inter - m_t)
        kb = k.astype(BF16)
        a = lax.dot_general(q, kb, (((1,), (1,)), ((), ())), preferred_element_type=F32) * w_intra
        num = (jnp.dot(a.astype(BF16), v, preferred_element_type=F32)
               + w_inter * jnp.dot(q, c_ref_h[...].astype(BF16), preferred_element_type=F32))
        den = (jnp.sum(a, axis=1, keepdims=True)
               + w_inter * jnp.sum(q.astype(F32) * n_ref[:, cols], axis=1, keepdims=True))
        hh = num / jnp.maximum(jnp.abs(den), jnp.exp(-m_t))
        mu = jnp.mean(hh, axis=1, keepdims=True)
        xc = hh - mu
        var = jnp.mean(xc * xc, axis=1, keepdims=True)
        hn = xc * lax.rsqrt(var + LN_EPS) * gn
        gate = _sigmoid(mo_ref[0, pl.ds(r0, LC), cols].astype(F32))
        o_ref[0, pl.ds(r0, LC), cols] = (gate * hn).astype(o_ref.dtype)
        b_last = b_row[:, LC - 1:LC]
        m_new = jnp.maximum(b_last + m, jnp.max(b_last + d_row, axis=1, keepdims=True))
        decay = jnp.exp(b_last + m - m_new)
        kw = k * jnp.exp(b_last + d_col - m_new)
        c_ref_h[...] = decay * c_ref_h[...] + lax.dot_general(
            kw.astype(BF16), v, (((0,), (0,)), ((), ())), preferred_element_type=F32)
        n_ref[:, cols] = decay * n_ref[:, cols] + jnp.sum(kw, axis=0, keepdims=True)
        return m_new

    def chunk(c, ms):
        r0 = pl.multiple_of(c * LC, LC)
        return tuple(head_chunk(hd, c, r0, ms[hd]) for hd in range(H))

    ms = lax.fori_loop(0, SEG // LC, chunk, tuple(m_ref[:, hd:hd + 1] for hd in range(H)))
    for hd in range(H):
        m_ref[:, hd:hd + 1] = ms[hd]


def _mlstm(proj3, gates, conv_q, conv_k, b_if, g_norm):
    B, S, _ = proj3.shape
    W = M_WIDTH
    SEG = MLSTM_SEGMENT
    nseg = S // SEG
    nC = SEG // MLSTM_CHUNK

    def col(base):
        return pl.BlockSpec((1, SEG, W), lambda b, s, pre: (b, s, base // W))

    full = lambda a: pl.BlockSpec(a.shape, lambda b, s, pre: (0,) * a.ndim)
    grid_spec = pltpu.PrefetchScalarGridSpec(
        num_scalar_prefetch=1,
        grid=(B, nseg),
        in_specs=[
            col(COL_MQ), col(COL_MK), col(COL_MV), col(COL_MO),
            pl.BlockSpec((1, 2 * M_HEADS, nC, MLSTM_CHUNK), lambda b, s, pre: (b, 0, s, 0)),
            full(conv_q), full(conv_k), full(g_norm),
        ],
        out_specs=pl.BlockSpec((1, SEG, W), lambda b, s, pre: (b, s, 0)),
        scratch_shapes=[
            pltpu.VMEM((SEG + SUBLANES, M_HEAD_DIM), F32),
            pltpu.VMEM((SEG, W), BF16),
            pltpu.VMEM((SEG, W), F32),
            pltpu.VMEM((M_HEADS, M_HEAD_DIM, M_HEAD_DIM), F32),
            pltpu.VMEM((1, W), F32),
            pltpu.VMEM((1, LANES), F32),
            pltpu.VMEM((SUBLANES, W), F32),
            pltpu.VMEM((SUBLANES, W), F32),
        ],
    )
    return pl.pallas_call(
        _mlstm_kernel,
        grid_spec=grid_spec,
        out_shape=jax.ShapeDtypeStruct((B, S, W), BF16),
        compiler_params=_params(("parallel", "arbitrary")),
        name="mlstm",
    )(b_if, proj3, proj3, proj3, proj3, gates, conv_q, conv_k, g_norm)


def _post_kernel(x_ref, ao0_ref, ao1_ref, ao2_ref, lse0_ref, lse1_ref, lse2_ref, mem_ref, ga_ref, gm_ref,
                 wa_ref, wm_ref, wo_ref, gf_ref, wr_ref, wrl_ref, br_ref, h1_ref, xn_ref, idx_ref, gate_ref):
    tm = x_ref.shape[0]
    ao_refs = (ao0_ref, ao1_ref, ao2_ref)
    lses = (lse0_ref[...], lse1_ref[...], lse2_ref[...])
    heads = []
    for h in range(HEADS_PER_GROUP):
        ls = [l[:, h:h + 1] for l in lses]
        mx = jnp.maximum(jnp.maximum(ls[0], ls[1]), ls[2])
        es = [jnp.exp(l - mx) for l in ls]
        tot = es[0] + es[1] + es[2]
        acc = None
        for g in range(N_GROUPS):
            term = (es[g] / tot) * ao_refs[g][:, h * HEAD_DIM:(h + 1) * HEAD_DIM].astype(F32)
            acc = term if acc is None else acc + term
        heads.append(acc.astype(BF16))
    attn = jnp.concatenate(heads, axis=1)
    ya = jnp.dot(attn, wa_ref[...], preferred_element_type=F32)
    ym = jnp.dot(mem_ref[...], wm_ref[...], preferred_element_type=F32)
    merged = _sigmoid(ga_ref[...].astype(F32)) * ya + _sigmoid(gm_ref[...].astype(F32)) * ym
    h1 = x_ref[...] + jnp.dot(merged.astype(BF16), wo_ref[...], preferred_element_type=F32)
    h1_ref[...] = h1
    ms = jnp.mean(h1 * h1, axis=-1, keepdims=True)
    xn = h1 * lax.rsqrt(ms + RMS_EPS) * gf_ref[...]
    for s in range(SUBLANES):
        xn_ref[pl.ds(s, tm, stride=SUBLANES), :] = xn[:, s * LANES:(s + 1) * LANES]
    xn_hi = xn.astype(BF16)
    xn_lo = (xn - xn_hi.astype(F32)).astype(BF16)
    logits = (jnp.dot(xn_hi, wr_ref[...], preferred_element_type=F32)
              + jnp.dot(xn_hi, wrl_ref[...], preferred_element_type=F32)
              + jnp.dot(xn_lo, wr_ref[...], preferred_element_type=F32)) + br_ref[...]
    lane = lax.broadcasted_iota(jnp.int32, (tm, LANES), 1).astype(F32)
    vals = logits
    tops, idxs = [], []
    for _ in range(TOP_K):
        mx = jnp.max(vals, axis=1, keepdims=True)
        ix = jnp.min(jnp.where(vals == mx, lane, float(LANES)), axis=1, keepdims=True)
        tops.append(mx)
        idxs.append(ix)
        vals = jnp.where(lane == ix, NEG, vals)
    es = [jnp.exp(t - tops[0]) for t in tops]
    tot = es[0] + es[1] + es[2] + es[3]
    idx_out = jnp.zeros((tm, LANES), F32)
    gate_out = jnp.zeros((tm, LANES), F32)
    for kk in range(TOP_K):
        idx_out = jnp.where(lane == float(kk), idxs[kk], idx_out)
        gate_out = jnp.where(lane == float(kk), es[kk] / tot, gate_out)
    idx_ref[...] = idx_out.astype(jnp.int32)
    gate_ref[...] = gate_out


def _post(x2, aos, lses, mem, proj, w_a, w_m, w_o, g_ffn, w_r, w_r_lo, b_r):
    T, D = x2.shape
    tm = POST_TM
    row = lambda w: pl.BlockSpec((tm, w), lambda i: (i, 0))
    full = lambda a: pl.BlockSpec(a.shape, lambda i: (0,) * a.ndim)
    return pl.pallas_call(
        _post_kernel,
        grid=(T // tm,),
        in_specs=[
            row(D), row(ATTN_OUT), row(ATTN_OUT), row(ATTN_OUT), row(LANES), row(LANES), row(LANES), row(M_WIDTH),
            pl.BlockSpec((tm, D), lambda i: (i, COL_GA // D)),
            pl.BlockSpec((tm, D), lambda i: (i, COL_GM // D)),
            full(w_a), full(w_m), full(w_o), full(g_ffn), full(w_r), full(w_r_lo), full(b_r),
        ],
        out_specs=[row(D), pl.BlockSpec((tm * SUBLANES, LANES), lambda i: (i, 0)), row(LANES), row(LANES)],
        out_shape=[jax.ShapeDtypeStruct((T, D), F32), jax.ShapeDtypeStruct((T * SUBLANES, LANES), F32),
                   jax.ShapeDtypeStruct((T, LANES), jnp.int32), jax.ShapeDtypeStruct((T, LANES), F32)],
        compiler_params=_params(("parallel",)),
        name="post",
    )(x2, *aos, *lses, mem, proj, proj, w_a, w_m, w_o, g_ffn, w_r, w_r_lo, b_r)


def _expert_kernel(be_ref, ib_ref, lo_ref, hi_ref, nu_ref, idx_hbm, xn_hbm, wgu_ref, wd_ref, bgu_ref, bd_ref, perm_ref,
                   y_hbm, idx_smem, xbuf, ybuf, xb_ref, wgu_bf, wd_bf, isem, gsem, ssem):
    i = pl.program_id(0)
    n_idx = ib_ref.shape[0]
    n_used = nu_ref[0]
    BLK = MOE_BLOCK
    n_islots = idx_smem.shape[0]
    n_bufs = xbuf.shape[0]

    def tile(t):
        return pl.ds(pl.multiple_of(t * SUBLANES, SUBLANES), SUBLANES)

    def tiles(n):
        return pl.ds(0, pl.multiple_of(n * SUBLANES, SUBLANES))

    def idx_copy(blk):
        slot = blk % n_islots
        return pltpu.make_async_copy(idx_hbm.at[ib_ref[blk]], idx_smem.at[slot], isem.at[slot])

    def gather_start(blk):
        islot, bslot = blk % n_islots, blk % n_bufs
        for r in range(BLK):
            tok = idx_smem[islot, r]
            pltpu.make_async_copy(xn_hbm.at[tile(tok)], xbuf.at[bslot, pl.ds(r * SUBLANES, SUBLANES)],
                                  gsem.at[bslot]).start()

    def gather_wait(blk):
        bslot = blk % n_bufs
        pltpu.make_async_copy(xn_hbm.at[tiles(BLK)], xbuf.at[bslot], gsem.at[bslot]).wait()

    def row_range(blk):
        ok = blk >= 0
        b = jnp.maximum(blk, 0)
        return jnp.where(ok, lo_ref[b], 0), jnp.where(ok, hi_ref[b], 0)

    def scatter_start(blk):
        islot, bslot = blk % n_islots, blk % n_bufs
        lo, hi = row_range(blk)
        n = (hi - lo).astype(jnp.uint32)
        for r in range(BLK):
            @pl.when((r - lo).astype(jnp.uint32) < n)
            def _():
                dst = idx_smem[islot, BLK + r]
                pltpu.make_async_copy(ybuf.at[bslot, pl.ds(r * SUBLANES, SUBLANES)], y_hbm.at[tile(dst)],
                                      ssem.at[bslot]).start()

    def scatter_wait(blk):
        bslot = blk % n_bufs
        lo, hi = row_range(blk)
        n = hi - lo

        @pl.when(n > 0)
        def _():
            pltpu.make_async_copy(ybuf.at[bslot, tiles(n)], y_hbm.at[tiles(n)], ssem.at[bslot]).wait()

    @pl.when(i == 0)
    def _():
        idx_copy(0).start()
        idx_copy(1).start()
        idx_copy(2).start()
        idx_copy(0).wait()
        idx_copy(1).wait()
        gather_start(0)
        gather_start(1)

    @pl.when(i + 3 < n_idx)
    def _():
        idx_copy(i + 3).start()

    @pl.when(i + 2 < n_idx)
    def _():
        idx_copy(i + 2).wait()

    @pl.when(jnp.logical_and(i < n_used, jnp.logical_or(i == 0, be_ref[i] != be_ref[jnp.maximum(i - 1, 0)])))
    def _():
        pw = perm_ref.shape[0]
        for c in range(wgu_ref.shape[2] // pw):
            cols = slice(c * pw, (c + 1) * pw)
            wgu_bf[:, cols] = jnp.dot(wgu_ref[0, :, cols].astype(BF16), perm_ref[...],
                                      preferred_element_type=F32).astype(BF16)
        wd_bf[...] = wd_ref[0].astype(BF16)

    @pl.when(i < n_used)
    def _():
        gather_wait(i)
        scatter_wait(i - 3)

    @pl.when(i < n_used)
    def _():
        bslot = i % n_bufs
        for s in range(SUBLANES):
            xb_ref[:, s * LANES:(s + 1) * LANES] = xbuf[bslot, pl.ds(s, BLK, stride=SUBLANES), :].astype(BF16)
        gather_start(i + 2)
        scatter_start(i - 1)
        hg = jnp.dot(xb_ref[...], wgu_bf[...], preferred_element_type=F32) + bgu_ref[0]
        acts = []
        for c in range(hg.shape[1] // (2 * LANES)):
            gate = jnp.minimum(hg[:, 2 * c * LANES:(2 * c + 1) * LANES], SWIGLU_LIMIT)
            up = jnp.clip(hg[:, (2 * c + 1) * LANES:(2 * c + 2) * LANES], -SWIGLU_LIMIT, SWIGLU_LIMIT)
            acts.append(((up + 1.0) * (gate * _sigmoid(SWIGLU_ALPHA * gate))).astype(BF16))
        act = jnp.concatenate(acts, axis=1)
        y = jnp.dot(act, wd_bf[...], preferred_element_type=F32) + bd_ref[0]
        for s in range(SUBLANES):
            ybuf[bslot, pl.ds(s, BLK, stride=SUBLANES), :] = y[:, s * LANES:(s + 1) * LANES]

    @pl.when(i == n_used)
    def _():
        gather_wait(i)
        gather_wait(i + 1)
        scatter_start(i - 1)
        scatter_wait(i - 3)
        scatter_wait(i - 2)
        scatter_wait(i - 1)


def _experts(item_e, item_blk, item_lo, item_hi, n_used, idx, xn2, wgu, wd, bgu, bd, perm, n_out_rows):
    nb = item_e.shape[0] - 1
    D = wd.shape[2]
    F2 = wgu.shape[2]
    emap = lambda i, be, ib, lo, hi, nu: (be[i], 0, 0)
    grid_spec = pltpu.PrefetchScalarGridSpec(
        num_scalar_prefetch=5,
        grid=(nb,),
        in_specs=[
            pl.BlockSpec(memory_space=pl.ANY),
            pl.BlockSpec(memory_space=pl.ANY),
            pl.BlockSpec((1, D, F2), emap),
            pl.BlockSpec((1, F2 // 2, D), emap),
            pl.BlockSpec((1, 1, F2), emap),
            pl.BlockSpec((1, 1, D), emap),
            pl.BlockSpec(perm.shape, lambda i, be, ib, lo, hi, nu: (0, 0)),
        ],
        out_specs=pl.BlockSpec(memory_space=pl.ANY),
        scratch_shapes=[
            pltpu.SMEM((8, 2 * MOE_BLOCK), jnp.int32),
            pltpu.VMEM((3, MOE_BLOCK * SUBLANES, LANES), F32),
            pltpu.VMEM((3, MOE_BLOCK * SUBLANES, LANES), F32),
            pltpu.VMEM((MOE_BLOCK, D), BF16),
            pltpu.VMEM((D, F2), BF16),
            pltpu.VMEM((F2 // 2, D), BF16),
            pltpu.SemaphoreType.DMA((8,)),
            pltpu.SemaphoreType.DMA((3,)),
            pltpu.SemaphoreType.DMA((3,)),
        ],
    )
    return pl.pallas_call(
        _expert_kernel,
        grid_spec=grid_spec,
        out_shape=jax.ShapeDtypeStruct((n_out_rows * SUBLANES, LANES), F32),
        compiler_params=_params(("arbitrary",)),
        name="experts",
    )(item_e, item_blk, item_lo, item_hi, n_used, idx, xn2, wgu, wd, bgu, bd, perm)


def _final_kernel(h1_ref, y0_ref, y1_ref, y2_ref, y3_ref, gate_ref, g_ref, o_ref):
    tm = h1_ref.shape[0]
    gts = gate_ref[...]
    g4 = [gts[:, kk:kk + 1] for kk in range(TOP_K)]
    pieces = []
    ss = None
    for s in range(SUBLANES):
        h = h1_ref[:, s * LANES:(s + 1) * LANES]
        for kk, y_ref in enumerate((y0_ref, y1_ref, y2_ref, y3_ref)):
            h = h + g4[kk] * y_ref[pl.ds(s, tm, stride=SUBLANES), :]
        pieces.append(h)
        sq = jnp.sum(h * h, axis=-1, keepdims=True)
        ss = sq if ss is None else ss + sq
    inv = lax.rsqrt(ss / h1_ref.shape[1] + RMS_EPS)
    for s in range(SUBLANES):
        o_ref[:, s * LANES:(s + 1) * LANES] = pieces[s] * inv * g_ref[:, s * LANES:(s + 1) * LANES]


def _final(h1, y4, gates, g_final):
    T, D = h1.shape
    tm = FINAL_TM
    nt = T // tm
    yspec = lambda kk: pl.BlockSpec((tm * SUBLANES, LANES), lambda i: (kk * nt + i, 0))
    return pl.pallas_call(
        _final_kernel,
        grid=(nt,),
        in_specs=[pl.BlockSpec((tm, D), lambda i: (i, 0)), yspec(0), yspec(1), yspec(2), yspec(3),
                  pl.BlockSpec((tm, LANES), lambda i: (i, 0)), pl.BlockSpec((1, D), lambda i: (0, 0))],
        out_specs=pl.BlockSpec((tm, D), lambda i: (i, 0)),
        out_shape=jax.ShapeDtypeStruct((T, D), F32),
        compiler_params=_params(("parallel",)),
        name="final",
    )(h1, y4, y4, y4, y4, gates, g_final)


def _rope_tables(positions):
    half = ROPE_DIM // 2
    inv_freq = ROPE_THETA ** (-jnp.arange(half, dtype=F32) / half)
    ang = positions.astype(F32).reshape(-1)[:, None] * inv_freq
    cos, sin = jnp.cos(ang), jnp.sin(ang)
    T = ang.shape[0]
    cos_t = jnp.concatenate([cos, cos, jnp.ones((T, HEAD_DIM - ROPE_DIM), F32)], axis=1)
    sa_t = jnp.concatenate([-sin, jnp.zeros((T, HEAD_DIM - half), F32)], axis=1)
    sb_t = jnp.concatenate([jnp.zeros((T, half), F32), sin, jnp.zeros((T, HEAD_DIM - ROPE_DIM), F32)], axis=1)
    return cos_t, sa_t, sb_t


def _routing(top_idx, T):
    A = T * TOP_K
    BLK = MOE_BLOCK
    n_blocks = A // BLK
    n_items = n_blocks + N_EXPERTS
    flat_e = top_idx.reshape(A)
    row_a = (jnp.sort(flat_e * A + jnp.arange(A, dtype=jnp.int32)) % A).reshape(n_blocks, BLK)
    row_tok = row_a // TOP_K
    row_slot = (row_a % TOP_K) * T + row_tok
    idx = jnp.concatenate([row_tok, row_slot], axis=1)
    experts = jnp.arange(N_EXPERTS, dtype=jnp.int32)
    counts = jnp.sum((flat_e[None, :] == experts[:, None]).astype(jnp.int32), axis=1)
    ends = jnp.cumsum(counts)
    starts = ends - counts
    first_blk = starts // BLK
    per_expert = jnp.where(counts > 0, (ends - 1) // BLK - first_blk + 1, 0)
    item_ends = jnp.cumsum(per_expert)
    item_starts = item_ends - per_expert
    n_used = item_ends[-1].astype(jnp.int32).reshape(1)
    it = jnp.arange(n_items + 2, dtype=jnp.int32)
    item_e = jnp.minimum(jnp.sum((item_ends[None, :] <= it[:, None]).astype(jnp.int32), axis=1), N_EXPERTS - 1)
    item_blk = jnp.clip(first_blk[item_e] + it - item_starts[item_e], 0, n_blocks - 1)
    live = it < item_ends[-1]
    item_lo = jnp.where(live, jnp.clip(starts[item_e] - item_blk * BLK, 0, BLK), 0).astype(jnp.int32)
    item_hi = jnp.where(live, jnp.clip(ends[item_e] - item_blk * BLK, 0, BLK), 0).astype(jnp.int32)
    return item_e.astype(jnp.int32), item_blk.astype(jnp.int32), item_lo, item_hi, n_used, idx, A


def kernel(x, positions, g_mix, w_in, conv_qk, b_if, g_mlstm_norm, w_attn_out, w_mlstm_out, w_mix_out, g_ffn,
           w_router, b_router, w_gate_up, b_gate_up, w_down, b_down, g_final):
    B, S, D = x.shape
    T = B * S
    l = 0
    x2 = x.reshape(T, D)

    w = w_in[l]
    o = np.cumsum((0, ATTN_WIDTH, ATTN_WIDTH, ATTN_WIDTH, M_WIDTH, M_WIDTH, M_WIDTH, M_WIDTH, M_HEADS, M_HEADS,
                   D_MODEL, D_MODEL))
    w_bf = w.astype(BF16)
    seg = lambda a: w_bf[:, int(o[a]):int(o[a + 1])]
    w_main = jnp.concatenate([seg(0), seg(1), seg(9), seg(10), seg(3), seg(4), seg(5), seg(6), seg(2)], axis=1)
    w_if = jnp.pad(w[:, int(o[7]):int(o[9])], ((0, 0), (0, LANES - 2 * M_HEADS))).astype(BF16)
    cos_t, sa_t, sb_t = _rope_tables(positions)

    proj, ifo = _proj(x2, g_mix[l][None, :], w_main, w_if, cos_t, sa_t, sb_t)

    proj3 = proj.reshape(B, S, N_MAIN)
    attn_out = [_attention(proj3, g) for g in range(N_GROUPS)]
    aos = [o.reshape(T, ATTN_OUT) for o, _ in attn_out]
    lses = [l.reshape(T, LANES) for _, l in attn_out]

    nC = S // MLSTM_CHUNK
    gates = ifo[:, :2 * M_HEADS].reshape(B, S, 2 * M_HEADS).transpose(0, 2, 1).reshape(B, 2 * M_HEADS, nC, MLSTM_CHUNK)
    mem = _mlstm(proj3, gates, conv_qk[l][:, :M_WIDTH], conv_qk[l][:, M_WIDTH:], b_if[l],
                 g_mlstm_norm[l][None, :])

    w_r32 = jnp.pad(w_router[l], ((0, 0), (0, LANES - N_EXPERTS)))
    w_r = w_r32.astype(BF16)
    w_r_lo = (w_r32 - w_r.astype(F32)).astype(BF16)
    b_r = jnp.concatenate([b_router[l], jnp.full((LANES - N_EXPERTS,), NEG, F32)])[None, :]
    h1, xn2, top_idx, gates4 = _post(x2, aos, lses, mem.reshape(T, M_WIDTH), proj, w_attn_out[l].astype(BF16),
                                     w_mlstm_out[l].astype(BF16), w_mix_out[l].astype(BF16), g_ffn[l][None, :],
                                     w_r, w_r_lo, b_r)

    item_e, item_blk, item_lo, item_hi, n_used, idx, n_out_rows = _routing(top_idx[:, :TOP_K], T)
    pair = np.arange(LANES)
    perm_np = np.zeros((2 * LANES, 2 * LANES), np.float32)
    perm_np[2 * pair, pair] = 1.0
    perm_np[2 * pair + 1, LANES + pair] = 1.0
    bgu = b_gate_up[l].reshape(N_EXPERTS, 1, -1, LANES, 2).swapaxes(-1, -2).reshape(N_EXPERTS, 1, 2 * D_FF)
    y4 = _experts(item_e, item_blk, item_lo, item_hi, n_used, idx, xn2, w_gate_up[l], w_down[l], bgu,
                  b_down[l][:, None, :], jnp.asarray(perm_np, BF16), n_out_rows)

    out = _final(h1, y4, gates4, g_final[None, :])
    return out.reshape(B, S, D)
```

```python
import functools

import jax
import jax.numpy as jnp
import numpy as np
from jax import lax
from jax.experimental import pallas as pl
from jax.experimental.pallas import tpu as pltpu

F32 = jnp.float32
BF16 = jnp.bfloat16

D_MODEL = 1024
N_GROUPS = 3
GROUP_DILATION = (1, 4, 16)
HEADS_PER_GROUP = 4
HEAD_DIM = 128
ATTN_WIDTH = N_GROUPS * HEADS_PER_GROUP * HEAD_DIM
ATTN_OUT = HEADS_PER_GROUP * HEAD_DIM
ATTN_BLOCK = 128
ROPE_THETA = 500000.0
ROPE_DIM = HEAD_DIM // 4
M_HEADS = 4
M_WIDTH = D_MODEL
M_HEAD_DIM = M_WIDTH // M_HEADS
CONV_WIDTH = 4
N_EXPERTS = 32
TOP_K = 4
D_FF = D_MODEL
SWIGLU_LIMIT = 7.0
SWIGLU_ALPHA = 1.702
RMS_EPS = 1e-5
LN_EPS = 1e-5

LANES = 128
SUBLANES = 8
VMEM_LIMIT = 56 * 1024 * 1024

PROJ_TM = 2048
PROJ_TN = 1536
MLSTM_CHUNK = 128
MLSTM_SEGMENT = 1024
POST_TM = 512
MOE_BLOCK = 512
FINAL_TM = 512
NEG = -1e30

COL_AQ = 0
COL_AK = ATTN_WIDTH
COL_GA = 2 * ATTN_WIDTH
COL_GM = COL_GA + D_MODEL
COL_MQ = COL_GM + D_MODEL
COL_MK = COL_MQ + M_WIDTH
COL_MV = COL_MK + M_WIDTH
COL_MO = COL_MV + M_WIDTH
COL_AV = COL_MO + M_WIDTH
N_MAIN = COL_AV + ATTN_WIDTH


def _sigmoid(x):
    return 0.5 * jnp.tanh(0.5 * x) + 0.5


def _params(sem):
    return pltpu.CompilerParams(dimension_semantics=sem, vmem_limit_bytes=VMEM_LIMIT)


def _proj_kernel(x_ref, g_ref, w_ref, wif_ref, cos_ref, sa_ref, sb_ref, o_ref, if_ref, xn_ref, *, rope_tiles):
    j = pl.program_id(1)

    @pl.when(j == 0)
    def _():
        x = x_ref[...]
        ms = jnp.mean(x * x, axis=-1, keepdims=True)
        xn = (x * lax.rsqrt(ms + RMS_EPS) * g_ref[...]).astype(BF16)
        xn_ref[...] = xn
        if_ref[...] = jnp.dot(xn, wif_ref[...], preferred_element_type=F32)

    piece = 4 * HEAD_DIM
    n_pieces = o_ref.shape[1] // piece

    @pl.when(j < rope_tiles)
    def _():
        c = cos_ref[...]
        sa = sa_ref[...]
        sb = sb_ref[...]
        for p in range(n_pieces):
            acc = jnp.dot(xn_ref[...], w_ref[:, p * piece:(p + 1) * piece], preferred_element_type=F32)
            for h in range(piece // HEAD_DIM):
                a = acc[:, h * HEAD_DIM:(h + 1) * HEAD_DIM]
                y = a * c + pltpu.roll(a, HEAD_DIM - ROPE_DIM // 2, 1) * sa + pltpu.roll(a, ROPE_DIM // 2, 1) * sb
                c0 = p * piece + h * HEAD_DIM
                o_ref[:, c0:c0 + HEAD_DIM] = y.astype(o_ref.dtype)

    @pl.when(j >= rope_tiles)
    def _():
        for p in range(n_pieces):
            cols = slice(p * piece, (p + 1) * piece)
            o_ref[:, cols] = jnp.dot(xn_ref[...], w_ref[:, cols], preferred_element_type=F32).astype(o_ref.dtype)


def _proj(x2, g_mix, w_main, w_if, cos_t, sa_t, sb_t):
    T, D = x2.shape
    N = w_main.shape[1]
    tm, tn = PROJ_TM, PROJ_TN
    grid = (T // tm, N // tn)
    return pl.pallas_call(
        functools.partial(_proj_kernel, rope_tiles=(2 * ATTN_WIDTH) // tn),
        grid=grid,
        in_specs=[
            pl.BlockSpec((tm, D), lambda i, j: (i, 0)),
            pl.BlockSpec((1, D), lambda i, j: (0, 0)),
            pl.BlockSpec((D, tn), lambda i, j: (0, j)),
            pl.BlockSpec((D, LANES), lambda i, j: (0, 0)),
            pl.BlockSpec((tm, LANES), lambda i, j: (i, 0)),
            pl.BlockSpec((tm, LANES), lambda i, j: (i, 0)),
            pl.BlockSpec((tm, LANES), lambda i, j: (i, 0)),
        ],
        out_specs=[
            pl.BlockSpec((tm, tn), lambda i, j: (i, j)),
            pl.BlockSpec((tm, LANES), lambda i, j: (i, 0)),
        ],
        out_shape=[jax.ShapeDtypeStruct((T, N), BF16), jax.ShapeDtypeStruct((T, LANES), F32)],
        scratch_shapes=[pltpu.VMEM((tm, D), BF16)],
        compiler_params=_params(("parallel", "arbitrary")),
        name="proj",
    )(x2, g_mix, w_main, w_if, cos_t, sa_t, sb_t)


def _attn_kernel(q_ref, k_ref, v_ref, o_ref, lse_ref, stage_ref, q_rm, k_rm, v_rm, o_rm, *, dilation):
    S = q_ref.shape[1]
    d = dilation
    L = S // d
    per_class = L // ATTN_BLOCK
    nblk = S // ATTN_BLOCK
    h = pl.program_id(1)

    if d == 1:
        q_all, k_all, v_all = q_ref[0], k_ref[0], v_ref[0]
    else:
        for src, dst in ((q_ref, q_rm), (k_ref, k_rm), (v_ref, v_rm)):
            stage_ref[...] = src[0].astype(F32)
            if d <= SUBLANES:
                for r in range(d):
                    dst[r * L:(r + 1) * L, :] = stage_ref[pl.ds(r, L, stride=d), :].astype(BF16)
            else:
                d0 = int(round(d ** 0.5))
                assert d0 * d0 == d and d0 <= SUBLANES
                for a in range(d0):
                    o_rm[a * (S // d0):(a + 1) * (S // d0), :] = stage_ref[pl.ds(a, S // d0, stride=d0), :]
                for a in range(d0):
                    for b in range(d0):
                        r = a + d0 * b
                        dst[r * L:(r + 1) * L, :] = o_rm[pl.ds(a * (S // d0) + b, L, stride=d0), :].astype(BF16)
        q_all, k_all, v_all = q_rm[...], k_rm[...], v_rm[...]

    @pl.when(h == 0)
    def _():
        lse_ref[...] = jnp.zeros_like(lse_ref)

    B_ = ATTN_BLOCK
    q3 = q_all.reshape(nblk, B_, HEAD_DIM)
    k3 = k_all.reshape(nblk, B_, HEAD_DIM)
    v3 = v_all.reshape(nblk, B_, HEAD_DIM)
    blk = lax.broadcasted_iota(jnp.int32, (nblk, B_, B_), 0)
    row = lax.broadcasted_iota(jnp.int32, (nblk, B_, B_), 1)
    col = lax.broadcasted_iota(jnp.int32, (nblk, B_, B_), 2)
    scale = HEAD_DIM ** -0.5
    dn_qk = (((2,), (2,)), ((0,), (0,)))
    dn_pv = (((2,), (1,)), ((0,), (0,)))
    s_c = lax.dot_general(q3, k3, dn_qk, preferred_element_type=F32) * scale + jnp.where(col <= row, 0.0, NEG)
    m = jnp.max(s_c, axis=2, keepdims=True)
    if per_class > 1:
        k_prev = jnp.concatenate([k3[:1], k3[:-1]], axis=0)
        v_prev = jnp.concatenate([v3[:1], v3[:-1]], axis=0)
        ok_prev = jnp.logical_and(col >= row, blk % per_class != 0)
        s_p = lax.dot_general(q3, k_prev, dn_qk, preferred_element_type=F32) * scale + jnp.where(ok_prev, 0.0, NEG)
        m = jnp.maximum(m, jnp.max(s_p, axis=2, keepdims=True))
    p_c = jnp.exp(s_c - m)
    den = jnp.sum(p_c, axis=2, keepdims=True)
    acc = lax.dot_general(p_c.astype(BF16), v3, dn_pv, preferred_element_type=F32)
    if per_class > 1:
        p_p = jnp.exp(s_p - m)
        den = den + jnp.sum(p_p, axis=2, keepdims=True)
        acc = acc + lax.dot_general(p_p.astype(BF16), v_prev, dn_pv, preferred_element_type=F32)
    out = (acc / den).reshape(S, HEAD_DIM)
    lse = m + jnp.log(den)
    my_lane = lax.broadcasted_iota(jnp.int32, (B_, LANES), 1) == h
    if d == 1:
        o_ref[0] = out.astype(o_ref.dtype)
    else:
        stage_ref[...] = out
    for n in range(nblk):
        sl = slice(n * B_, (n + 1) * B_)
        start = n // per_class + d * (n % per_class) * B_
        rows = pl.ds(start, B_, stride=d) if d > 1 else sl
        if d > 1:
            o_rm[rows, :] = stage_ref[sl, :]
        lse_ref[0, rows, :] = jnp.where(my_lane, lse[n], lse_ref[0, rows, :])
    if d > 1:
        o_ref[0] = o_rm[...].astype(o_ref.dtype)


def _attention(proj3, g):
    B, S, _ = proj3.shape
    Dh = HEAD_DIM

    def col(base):
        return pl.BlockSpec((1, S, Dh), lambda b, h: (b, 0, (base + g * ATTN_OUT) // Dh + h))

    return pl.pallas_call(
        functools.partial(_attn_kernel, dilation=GROUP_DILATION[g]),
        grid=(B, HEADS_PER_GROUP),
        in_specs=[col(COL_AQ), col(COL_AK), col(COL_AV)],
        out_specs=[pl.BlockSpec((1, S, Dh), lambda b, h: (b, 0, h)),
                   pl.BlockSpec((1, S, LANES), lambda b, h: (b, 0, 0))],
        out_shape=[jax.ShapeDtypeStruct((B, S, ATTN_OUT), BF16), jax.ShapeDtypeStruct((B, S, LANES), F32)],
        scratch_shapes=[pltpu.VMEM((S, Dh), F32), pltpu.VMEM((S, Dh), BF16), pltpu.VMEM((S, Dh), BF16),
                        pltpu.VMEM((S, Dh), BF16), pltpu.VMEM((S, Dh), F32)],
        compiler_params=_params(("parallel", "arbitrary")),
        name=f"attn{g}",
    )(proj3, proj3, proj3)


def _mlstm_kernel(bif_ref, mq_ref, mk_ref, mv_ref, mo_ref, g_ref, cwq_ref, cwk_ref, gn_ref, o_ref,
                  pad_ref, qs_ref, ks_ref, c_ref, n_ref, m_ref, hq_ref, hk_ref):
    SEG = mq_ref.shape[1]
    LC = MLSTM_CHUNK
    Dh = M_HEAD_DIM
    H = M_HEADS
    seg = pl.program_id(1)

    @pl.when(seg == 0)
    def _():
        c_ref[...] = jnp.zeros_like(c_ref)
        n_ref[...] = jnp.zeros_like(n_ref)
        m_ref[...] = jnp.zeros_like(m_ref)
        hq_ref[...] = jnp.zeros_like(hq_ref)
        hk_ref[...] = jnp.zeros_like(hk_ref)

    def conv_silu(src_ref, hist_ref, w_ref, cols, dst_ref, scale):
        pad_ref[0:SUBLANES, :] = hist_ref[:, cols]
        pad_ref[SUBLANES:SUBLANES + SEG, :] = src_ref[0, :, cols].astype(F32)
        hist_ref[:, cols] = pad_ref[SEG:SEG + SUBLANES, :]
        rb = LC
        for r0 in range(0, SEG, rb):
            acc = None
            for j in range(CONV_WIDTH):
                off = r0 + SUBLANES - (CONV_WIDTH - 1) + j
                term = w_ref[j:j + 1, cols] * pad_ref[off:off + rb, :]
                acc = term if acc is None else acc + term
            dst_ref[r0:r0 + rb, cols] = (acc * _sigmoid(acc) * scale).astype(dst_ref.dtype)

    for hd in range(H):
        cols = slice(hd * Dh, (hd + 1) * Dh)
        conv_silu(mq_ref, hq_ref, cwq_ref, cols, qs_ref, 1.0)
        conv_silu(mk_ref, hk_ref, cwk_ref, cols, ks_ref, Dh ** -0.5)

    row = lax.broadcasted_iota(jnp.int32, (LC, LC), 0)
    col = lax.broadcasted_iota(jnp.int32, (LC, LC), 1)
    causal = col <= row
    eye = col == row
    lane8 = lax.broadcasted_iota(jnp.int32, (SUBLANES, LC), 1)

    def head_chunk(hd, c, r0, m):
        cols = slice(hd * Dh, (hd + 1) * Dh)
        c_ref_h = c_ref.at[hd]
        gn = gn_ref[:, cols]
        q = qs_ref[pl.ds(r0, LC), cols]
        k = ks_ref[pl.ds(r0, LC), cols]
        v = mv_ref[0, pl.ds(r0, LC), cols]
        i_row = g_ref[0, hd, pl.ds(c, 1), :] + bif_ref[hd]
        f_row = g_ref[0, H + hd, pl.ds(c, 1), :] + bif_ref[H + hd]
        logf = jnp.minimum(f_row, 0.0) - jnp.log(1.0 + jnp.exp(-jnp.abs(f_row)))
        b8 = jnp.broadcast_to(logf, (SUBLANES, LC))
        s = 1
        while s < LC:
            b8 = b8 + jnp.where(lane8 >= s, pltpu.roll(b8, s, 1), 0.0)
            s *= 2
        b_row = b8[0:1, :]
        d_row = i_row - b_row
        b_col = jnp.sum(jnp.where(eye, b_row, 0.0), axis=1, keepdims=True)
        d_col = jnp.sum(jnp.where(eye, d_row, 0.0), axis=1, keepdims=True)
        dm = jnp.where(causal, b_col + d_row, NEG)
        inter = b_col + m
        m_t = jnp.maximum(inter, jnp.max(dm, axis=1, keepdims=True))
        w_intra = jnp.exp(dm - m_t)
        w_inter = jnp.exp(inter - m_t)
        kb = k.astype(BF16)
        a = lax.dot_general(q, kb, (((1,), (1,)), ((), ())), preferred_element_type=F32) * w_intra
        num = (jnp.dot(a.astype(BF16), v, preferred_element_type=F32)
               + w_inter * jnp.dot(q, c_ref_h[...].astype(BF16), preferred_element_type=F32))
        den = (jnp.sum(a, axis=1, keepdims=True)
               + w_inter * jnp.sum(q.astype(F32) * n_ref[:, cols], axis=1, keepdims=True))
        hh = num / jnp.maximum(jnp.abs(den), jnp.exp(-m_t))
        mu = jnp.mean(hh, axis=1, keepdims=True)
        xc = hh - mu
        var = jnp.mean(xc * xc, axis=1, keepdims=True)
        hn = xc * lax.rsqrt(var + LN_EPS) * gn
        gate = _sigmoid(mo_ref[0, pl.ds(r0, LC), cols].astype(F32))
        o_ref[0, pl.ds(r0, LC), cols] = (gate * hn).astype(o_ref.dtype)
        b_last = b_row[:, LC - 1:LC]
        m_new = jnp.maximum(b_last + m, jnp.max(b_last + d_row, axis=1, keepdims=True))
        decay = jnp.exp(b_last + m - m_new)
        kw = k * jnp.exp(b_last + d_col - m_new)
        c_ref_h[...] = decay * c_ref_h[...] + lax.dot_general(
            kw.astype(BF16), v, (((0,), (0,)), ((), ())), preferred_element_type=F32)
        n_ref[:, cols] = decay * n_ref[:, cols] + jnp.sum(kw, axis=0, keepdims=True)
        return m_new

    def chunk(c, ms):
        r0 = pl.multiple_of(c * LC, LC)
        return tuple(head_chunk(hd, c, r0, ms[hd]) for hd in range(H))

    ms = lax.fori_loop(0, SEG // LC, chunk, tuple(m_ref[:, hd:hd + 1] for hd in range(H)))
    for hd in range(H):
        m_ref[:, hd:hd + 1] = ms[hd]


def _mlstm(proj3, gates, conv_q, conv_k, b_if, g_norm):
    B, S, _ = proj3.shape
    W = M_WIDTH
    SEG = MLSTM_SEGMENT
    nseg = S // SEG
    nC = SEG // MLSTM_CHUNK

    def col(base):
        return pl.BlockSpec((1, SEG, W), lambda b, s, pre: (b, s, base // W))

    full = lambda a: pl.BlockSpec(a.shape, lambda b, s, pre: (0,) * a.ndim)
    grid_spec = pltpu.PrefetchScalarGridSpec(
        num_scalar_prefetch=1,
        grid=(B, nseg),
        in_specs=[
            col(COL_MQ), col(COL_MK), col(COL_MV), col(COL_MO),
            pl.BlockSpec((1, 2 * M_HEADS, nC, MLSTM_CHUNK), lambda b, s, pre: (b, 0, s, 0)),
            full(conv_q), full(conv_k), full(g_norm),
        ],
        out_specs=pl.BlockSpec((1, SEG, W), lambda b, s, pre: (b, s, 0)),
        scratch_shapes=[
            pltpu.VMEM((SEG + SUBLANES, M_HEAD_DIM), F32),
            pltpu.VMEM((SEG, W), BF16),
            pltpu.VMEM((SEG, W), F32),
            pltpu.VMEM((M_HEADS, M_HEAD_DIM, M_HEAD_DIM), F32),
            pltpu.VMEM((1, W), F32),
            pltpu.VMEM((1, LANES), F32),
            pltpu.VMEM((SUBLANES, W), F32),
            pltpu.VMEM((SUBLANES, W), F32),
        ],
    )
    return pl.pallas_call(
        _mlstm_kernel,
        grid_spec=grid_spec,
        out_shape=jax.ShapeDtypeStruct((B, S, W), BF16),
        compiler_params=_params(("parallel", "arbitrary")),
        name="mlstm",
    )(b_if, proj3, proj3, proj3, proj3, gates, conv_q, conv_k, g_norm)


def _post_kernel(x_ref, ao0_ref, ao1_ref, ao2_ref, lse0_ref, lse1_ref, lse2_ref, mem_ref, ga_ref, gm_ref,
                 wa_ref, wm_ref, wo_ref, gf_ref, wr_ref, wrl_ref, br_ref, h1_ref, xn_ref, idx_ref, gate_ref):
    tm = x_ref.shape[0]
    ao_refs = (ao0_ref, ao1_ref, ao2_ref)
    lses = (lse0_ref[...], lse1_ref[...], lse2_ref[...])
    heads = []
    for h in range(HEADS_PER_GROUP):
        ls = [l[:, h:h + 1] for l in lses]
        mx = jnp.maximum(jnp.maximum(ls[0], ls[1]), ls[2])
        es = [jnp.exp(l - mx) for l in ls]
        tot = es[0] + es[1] + es[2]
        acc = None
        for g in range(N_GROUPS):
            term = (es[g] / tot) * ao_refs[g][:, h * HEAD_DIM:(h + 1) * HEAD_DIM].astype(F32)
            acc = term if acc is None else acc + term
        heads.append(acc.astype(BF16))
    attn = jnp.concatenate(heads, axis=1)
    ya = jnp.dot(attn, wa_ref[...], preferred_element_type=F32)
    ym = jnp.dot(mem_ref[...], wm_ref[...], preferred_element_type=F32)
    merged = _sigmoid(ga_ref[...].astype(F32)) * ya + _sigmoid(gm_ref[...].astype(F32)) * ym
    h1 = x_ref[...] + jnp.dot(merged.astype(BF16), wo_ref[...], preferred_element_type=F32)
    h1_ref[...] = h1
    ms = jnp.mean(h1 * h1, axis=-1, keepdims=True)
    xn = h1 * lax.rsqrt(ms + RMS_EPS) * gf_ref[...]
    for s in range(SUBLANES):
        xn_ref[pl.ds(s, tm, stride=SUBLANES), :] = xn[:, s * LANES:(s + 1) * LANES]
    xn_hi = xn.astype(BF16)
    xn_lo = (xn - xn_hi.astype(F32)).astype(BF16)
    logits = (jnp.dot(xn_hi, wr_ref[...], preferred_element_type=F32)
              + jnp.dot(xn_hi, wrl_ref[...], preferred_element_type=F32)
              + jnp.dot(xn_lo, wr_ref[...], preferred_element_type=F32)) + br_ref[...]
    lane = lax.broadcasted_iota(jnp.int32, (tm, LANES), 1).astype(F32)
    vals = logits
    tops, idxs = [], []
    for _ in range(TOP_K):
        mx = jnp.max(vals, axis=1, keepdims=True)
        ix = jnp.min(jnp.where(vals == mx, lane, float(LANES)), axis=1, keepdims=True)
        tops.append(mx)
        idxs.append(ix)
        vals = jnp.where(lane == ix, NEG, vals)
    es = [jnp.exp(t - tops[0]) for t in tops]
    tot = es[0] + es[1] + es[2] + es[3]
    idx_out = jnp.zeros((tm, LANES), F32)
    gate_out = jnp.zeros((tm, LANES), F32)
    for kk in range(TOP_K):
        idx_out = jnp.where(lane == float(kk), idxs[kk], idx_out)
        gate_out = jnp.where(lane == float(kk), es[kk] / tot, gate_out)
    idx_ref[...] = idx_out.astype(jnp.int32)
    gate_ref[...] = gate_out


def _post(x2, aos, lses, mem, proj, w_a, w_m, w_o, g_ffn, w_r, w_r_lo, b_r):
    T, D = x2.shape
    tm = POST_TM
    row = lambda w: pl.BlockSpec((tm, w), lambda i: (i, 0))
    full = lambda a: pl.BlockSpec(a.shape, lambda i: (0,) * a.ndim)
    return pl.pallas_call(
        _post_kernel,
        grid=(T // tm,),
        in_specs=[
            row(D), row(ATTN_OUT), row(ATTN_OUT), row(ATTN_OUT), row(LANES), row(LANES), row(LANES), row(M_WIDTH),
            pl.BlockSpec((tm, D), lambda i: (i, COL_GA // D)),
            pl.BlockSpec((tm, D), lambda i: (i, COL_GM // D)),
            full(w_a), full(w_m), full(w_o), full(g_ffn), full(w_r), full(w_r_lo), full(b_r),
        ],
        out_specs=[row(D), pl.BlockSpec((tm * SUBLANES, LANES), lambda i: (i, 0)), row(LANES), row(LANES)],
        out_shape=[jax.ShapeDtypeStruct((T, D), F32), jax.ShapeDtypeStruct((T * SUBLANES, LANES), F32),
                   jax.ShapeDtypeStruct((T, LANES), jnp.int32), jax.ShapeDtypeStruct((T, LANES), F32)],
        compiler_params=_params(("parallel",)),
        name="post",
    )(x2, *aos, *lses, mem, proj, proj, w_a, w_m, w_o, g_ffn, w_r, w_r_lo, b_r)


def _expert_kernel(be_ref, ib_ref, lo_ref, hi_ref, nu_ref, idx_hbm, xn_hbm, wgu_ref, wd_ref, bgu_ref, bd_ref, perm_ref,
                   y_hbm, idx_smem, xbuf, ybuf, xb_ref, wgu_bf, wd_bf, isem, gsem, ssem):
    i = pl.program_id(0)
    n_idx = ib_ref.shape[0]
    n_used = nu_ref[0]
    BLK = MOE_BLOCK
    n_islots = idx_smem.shape[0]
    n_bufs = xbuf.shape[0]

    def tile(t):
        return pl.ds(pl.multiple_of(t * SUBLANES, SUBLANES), SUBLANES)

    def tiles(n):
        return pl.ds(0, pl.multiple_of(n * SUBLANES, SUBLANES))

    def idx_copy(blk):
        slot = blk % n_islots
        return pltpu.make_async_copy(idx_hbm.at[ib_ref[blk]], idx_smem.at[slot], isem.at[slot])

    def gather_start(blk):
        islot, bslot = blk % n_islots, blk % n_bufs
        for r in range(BLK):
            tok = idx_smem[islot, r]
            pltpu.make_async_copy(xn_hbm.at[tile(tok)], xbuf.at[bslot, pl.ds(r * SUBLANES, SUBLANES)],
                                  gsem.at[bslot]).start()

    def gather_wait(blk):
        bslot = blk % n_bufs
        pltpu.make_async_copy(xn_hbm.at[tiles(BLK)], xbuf.at[bslot], gsem.at[bslot]).wait()

    def row_range(blk):
        ok = blk >= 0
        b = jnp.maximum(blk, 0)
        return jnp.where(ok, lo_ref[b], 0), jnp.where(ok, hi_ref[b], 0)

    def scatter_start(blk):
        islot, bslot = blk % n_islots, blk % n_bufs
        lo, hi = row_range(blk)
        n = (hi - lo).astype(jnp.uint32)
        for r in range(BLK):
            @pl.when((r - lo).astype(jnp.uint32) < n)
            def _():
                dst = idx_smem[islot, BLK + r]
                pltpu.make_async_copy(ybuf.at[bslot, pl.ds(r * SUBLANES, SUBLANES)], y_hbm.at[tile(dst)],
                                      ssem.at[bslot]).start()

    def scatter_wait(blk):
        bslot = blk % n_bufs
        lo, hi = row_range(blk)
        n = hi - lo

        @pl.when(n > 0)
        def _():
            pltpu.make_async_copy(ybuf.at[bslot, tiles(n)], y_hbm.at[tiles(n)], ssem.at[bslot]).wait()

    @pl.when(i == 0)
    def _():
        idx_copy(0).start()
        idx_copy(1).start()
        idx_copy(2).start()
        idx_copy(0).wait()
        idx_copy(1).wait()
        gather_start(0)
        gather_start(1)

    @pl.when(i + 3 < n_idx)
    def _():
        idx_copy(i + 3).start()

    @pl.when(i + 2 < n_idx)
    def _():
        idx_copy(i + 2).wait()

    @pl.when(jnp.logical_and(i < n_used, jnp.logical_or(i == 0, be_ref[i] != be_ref[jnp.maximum(i - 1, 0)])))
    def _():
        pw = perm_ref.shape[0]
        for c in range(wgu_ref.shape[2] // pw):
            cols = slice(c * pw, (c + 1) * pw)
            wgu_bf[:, cols] = jnp.dot(wgu_ref[0, :, cols].astype(BF16), perm_ref[...],
                                      preferred_element_type=F32).astype(BF16)
        wd_bf[...] = wd_ref[0].astype(BF16)

    @pl.when(i < n_used)
    def _():
        gather_wait(i)
        scatter_wait(i - 3)

    @pl.when(i < n_used)
    def _():
        bslot = i % n_bufs
        for s in range(SUBLANES):
            xb_ref[:, s * LANES:(s + 1) * LANES] = xbuf[bslot, pl.ds(s, BLK, stride=SUBLANES), :].astype(BF16)
        gather_start(i + 2)
        scatter_start(i - 1)
        hg = jnp.dot(xb_ref[...], wgu_bf[...], preferred_element_type=F32) + bgu_ref[0]
        acts = []
        for c in range(hg.shape[1] // (2 * LANES)):
            gate = jnp.minimum(hg[:, 2 * c * LANES:(2 * c + 1) * LANES], SWIGLU_LIMIT)
            up = jnp.clip(hg[:, (2 * c + 1) * LANES:(2 * c + 2) * LANES], -SWIGLU_LIMIT, SWIGLU_LIMIT)
            acts.append(((up + 1.0) * (gate * _sigmoid(SWIGLU_ALPHA * gate))).astype(BF16))
        act = jnp.concatenate(acts, axis=1)
        y = jnp.dot(act, wd_bf[...], preferred_element_type=F32) + bd_ref[0]
        for s in range(SUBLANES):
            ybuf[bslot, pl.ds(s, BLK, stride=SUBLANES), :] = y[:, s * LANES:(s + 1) * LANES]

    @pl.when(i == n_used)
    def _():
        gather_wait(i)
        gather_wait(i + 1)
        scatter_start(i - 1)
        scatter_wait(i - 3)
        scatter_wait(i - 2)
        scatter_wait(i - 1)


def _experts(item_e, item_blk, item_lo, item_hi, n_used, idx, xn2, wgu, wd, bgu, bd, perm, n_out_rows):
    nb = item_e.shape[0] - 1
    D = wd.shape[2]
    F2 = wgu.shape[2]
    emap = lambda i, be, ib, lo, hi, nu: (be[i], 0, 0)
    grid_spec = pltpu.PrefetchScalarGridSpec(
        num_scalar_prefetch=5,
        grid=(nb,),
        in_specs=[
            pl.BlockSpec(memory_space=pl.ANY),
            pl.BlockSpec(memory_space=pl.ANY),
            pl.BlockSpec((1, D, F2), emap),
            pl.BlockSpec((1, F2 // 2, D), emap),
            pl.BlockSpec((1, 1, F2), emap),
            pl.BlockSpec((1, 1, D), emap),
            pl.BlockSpec(perm.shape, lambda i, be, ib, lo, hi, nu: (0, 0)),
        ],
        out_specs=pl.BlockSpec(memory_space=pl.ANY),
        scratch_shapes=[
            pltpu.SMEM((8, 2 * MOE_BLOCK), jnp.int32),
            pltpu.VMEM((3, MOE_BLOCK * SUBLANES, LANES), F32),
            pltpu.VMEM((3, MOE_BLOCK * SUBLANES, LANES), F32),
            pltpu.VMEM((MOE_BLOCK, D), BF16),
            pltpu.VMEM((D, F2), BF16),
            pltpu.VMEM((F2 // 2, D), BF16),
            pltpu.SemaphoreType.DMA((8,)),
            pltpu.SemaphoreType.DMA((3,)),
            pltpu.SemaphoreType.DMA((3,)),
        ],
    )
    return pl.pallas_call(
        _expert_kernel,
        grid_spec=grid_spec,
        out_shape=jax.ShapeDtypeStruct((n_out_rows * SUBLANES, LANES), F32),
        compiler_params=_params(("arbitrary",)),
        name="experts",
    )(item_e, item_blk, item_lo, item_hi, n_used, idx, xn2, wgu, wd, bgu, bd, perm)


def _final_kernel(h1_ref, y0_ref, y1_ref, y2_ref, y3_ref, gate_ref, g_ref, o_ref):
    tm = h1_ref.shape[0]
    gts = gate_ref[...]
    g4 = [gts[:, kk:kk + 1] for kk in range(TOP_K)]
    pieces = []
    ss = None
    for s in range(SUBLANES):
        h = h1_ref[:, s * LANES:(s + 1) * LANES]
        for kk, y_ref in enumerate((y0_ref, y1_ref, y2_ref, y3_ref)):
            h = h + g4[kk] * y_ref[pl.ds(s, tm, stride=SUBLANES), :]
        pieces.append(h)
        sq = jnp.sum(h * h, axis=-1, keepdims=True)
        ss = sq if ss is None else ss + sq
    inv = lax.rsqrt(ss / h1_ref.shape[1] + RMS_EPS)
    for s in range(SUBLANES):
        o_ref[:, s * LANES:(s + 1) * LANES] = pieces[s] * inv * g_ref[:, s * LANES:(s + 1) * LANES]


def _final(h1, y4, gates, g_final):
    T, D = h1.shape
    tm = FINAL_TM
    nt = T // tm
    yspec = lambda kk: pl.BlockSpec((tm * SUBLANES, LANES), lambda i: (kk * nt + i, 0))
    return pl.pallas_call(
        _final_kernel,
        grid=(nt,),
        in_specs=[pl.BlockSpec((tm, D), lambda i: (i, 0)), yspec(0), yspec(1), yspec(2), yspec(3),
                  pl.BlockSpec((tm, LANES), lambda i: (i, 0)), pl.BlockSpec((1, D), lambda i: (0, 0))],
        out_specs=pl.BlockSpec((tm, D), lambda i: (i, 0)),
        out_shape=jax.ShapeDtypeStruct((T, D), F32),
        compiler_params=_params(("parallel",)),
        name="final",
    )(h1, y4, y4, y4, y4, gates, g_final)


def _rope_tables(positions):
    half = ROPE_DIM // 2
    inv_freq = ROPE_THETA ** (-jnp.arange(half, dtype=F32) / half)
    ang = positions.astype(F32).reshape(-1)[:, None] * inv_freq
    cos, sin = jnp.cos(ang), jnp.sin(ang)
    T = ang.shape[0]
    cos_t = jnp.concatenate([cos, cos, jnp.ones((T, HEAD_DIM - ROPE_DIM), F32)], axis=1)
    sa_t = jnp.concatenate([-sin, jnp.zeros((T, HEAD_DIM - half), F32)], axis=1)
    sb_t = jnp.concatenate([jnp.zeros((T, half), F32), sin, jnp.zeros((T, HEAD_DIM - ROPE_DIM), F32)], axis=1)
    return cos_t, sa_t, sb_t


def _routing(top_idx, T):
    A = T * TOP_K
    BLK = MOE_BLOCK
    n_blocks = A // BLK
    n_items = n_blocks + N_EXPERTS
    flat_e = top_idx.reshape(A)
    row_a = (jnp.sort(flat_e * A + jnp.arange(A, dtype=jnp.int32)) % A).reshape(n_blocks, BLK)
    row_tok = row_a // TOP_K
    row_slot = (row_a % TOP_K) * T + row_tok
    idx = jnp.concatenate([row_tok, row_slot], axis=1)
    experts = jnp.arange(N_EXPERTS, dtype=jnp.int32)
    counts = jnp.sum((flat_e[None, :] == experts[:, None]).astype(jnp.int32), axis=1)
    ends = jnp.cumsum(counts)
    starts = ends - counts
    first_blk = starts // BLK
    per_expert = jnp.where(counts > 0, (ends - 1) // BLK - first_blk + 1, 0)
    item_ends = jnp.cumsum(per_expert)
    item_starts = item_ends - per_expert
    n_used = item_ends[-1].astype(jnp.int32).reshape(1)
    it = jnp.arange(n_items + 2, dtype=jnp.int32)
    item_e = jnp.minimum(jnp.sum((item_ends[None, :] <= it[:, None]).astype(jnp.int32), axis=1), N_EXPERTS - 1)
    item_blk = jnp.clip(first_blk[item_e] + it - item_starts[item_e], 0, n_blocks - 1)
    live = it < item_ends[-1]
    item_lo = jnp.where(live, jnp.clip(starts[item_e] - item_blk * BLK, 0, BLK), 0).astype(jnp.int32)
    item_hi = jnp.where(live, jnp.clip(ends[item_e] - item_blk * BLK, 0, BLK), 0).astype(jnp.int32)
    return item_e.astype(jnp.int32), item_blk.astype(jnp.int32), item_lo, item_hi, n_used, idx, A


def kernel(x, positions, g_mix, w_in, conv_qk, b_if, g_mlstm_norm, w_attn_out, w_mlstm_out, w_mix_out, g_ffn,
           w_router, b_router, w_gate_up, b_gate_up, w_down, b_down, g_final):
    B, S, D = x.shape
    T = B * S
    l = 0
    x2 = x.reshape(T, D)

    w = w_in[l]
    o = np.cumsum((0, ATTN_WIDTH, ATTN_WIDTH, ATTN_WIDTH, M_WIDTH, M_WIDTH, M_WIDTH, M_WIDTH, M_HEADS, M_HEADS,
                   D_MODEL, D_MODEL))
    w_bf = w.astype(BF16)
    seg = lambda a: w_bf[:, int(o[a]):int(o[a + 1])]
    w_main = jnp.concatenate([seg(0), seg(1), seg(9), seg(10), seg(3), seg(4), seg(5), seg(6), seg(2)], axis=1)
    w_if = jnp.pad(w[:, int(o[7]):int(o[9])], ((0, 0), (0, LANES - 2 * M_HEADS))).astype(BF16)
    cos_t, sa_t, sb_t = _rope_tables(positions)

    proj, ifo = _proj(x2, g_mix[l][None, :], w_main, w_if, cos_t, sa_t, sb_t)

    proj3 = proj.reshape(B, S, N_MAIN)
    attn_out = [_attention(proj3, g) for g in range(N_GROUPS)]
    aos = [o.reshape(T, ATTN_OUT) for o, _ in attn_out]
    lses = [l.reshape(T, LANES) for _, l in attn_out]

    nC = S // MLSTM_CHUNK
    gates = ifo[:, :2 * M_HEADS].reshape(B, S, 2 * M_HEADS).transpose(0, 2, 1).reshape(B, 2 * M_HEADS, nC, MLSTM_CHUNK)
    mem = _mlstm(proj3, gates, conv_qk[l][:, :M_WIDTH], conv_qk[l][:, M_WIDTH:], b_if[l],
                 g_mlstm_norm[l][None, :])

    w_r32 = jnp.pad(w_router[l], ((0, 0), (0, LANES - N_EXPERTS)))
    w_r = w_r32.astype(BF16)
    w_r_lo = (w_r32 - w_r.astype(F32)).astype(BF16)
    b_r = jnp.concatenate([b_router[l], jnp.full((LANES - N_EXPERTS,), NEG, F32)])[None, :]
    h1, xn2, top_idx, gates4 = _post(x2, aos, lses, mem.reshape(T, M_WIDTH), proj, w_attn_out[l].astype(BF16),
                                     w_mlstm_out[l].astype(BF16), w_mix_out[l].astype(BF16), g_ffn[l][None, :],
                                     w_r, w_r_lo, b_r)

    item_e, item_blk, item_lo, item_hi, n_used, idx, n_out_rows = _routing(top_idx[:, :TOP_K], T)
    pair = np.arange(LANES)
    perm_np = np.zeros((2 * LANES, 2 * LANES), np.float32)
    perm_np[2 * pair, pair] = 1.0
    perm_np[2 * pair + 1, LANES + pair] = 1.0
    bgu = b_gate_up[l].reshape(N_EXPERTS, 1, -1, LANES, 2).swapaxes(-1, -2).reshape(N_EXPERTS, 1, 2 * D_FF)
    y4 = _experts(item_e, item_blk, item_lo, item_hi, n_used, idx, xn2, w_gate_up[l], w_down[l], bgu,
                  b_down[l][:, None, :], jnp.asarray(perm_np, BF16), n_out_rows)

    out = _final(h1, y4, gates4, g_final[None, :])
    return out.reshape(B, S, D)
```
